```python
import math
import jax, jax.numpy as jnp
from jax import lax
import numpy as np

D_MODEL = 1024
BATCH = 8
SEQ = 8192
DEPTH = 4
DEC_BATCH = 8
DEC_SEQ = 32
PAST_LEN = 2048

CHUNK = 64
QBLOCK = CHUNK
GDN_HEADS = 4
GDN_DK = 128
GDN_DV = 128
CONV_W = 4
DSA_HEADS = 8
DSA_DH = 64
IDX_HEADS = 4
IDX_DIM = 64
TOPK_MAX = 256
ROPE_THETA = 500000.0
D_FF = 4096
N_EXPERTS = 8
TOP_K_EXPERTS = 2
D_FF_EXPERT = 3584
N_DENSE = (DEPTH + 1) // 2
N_MOE = DEPTH // 2
DEEPNORM_ALPHA = (2.0 * DEPTH) ** 0.25
DEEPNORM_BETA = (8.0 * DEPTH) ** -0.25
LN_EPS = 1e-5
RMS_EPS = 1e-6
L2_EPS = 1e-6
GDN_QK = GDN_HEADS * GDN_DK
GDN_VW = GDN_HEADS * GDN_DV
GDN_CONV_CH = 2 * GDN_QK + GDN_VW
DSA_W = DSA_HEADS * DSA_DH
MIX_W = GDN_VW + DSA_W
SPLITS = (GDN_CONV_CH, GDN_VW, GDN_HEADS, GDN_HEADS, DSA_W, DSA_W, DSA_W, IDX_HEADS * IDX_DIM, IDX_DIM, IDX_HEADS)
N_IN = sum(SPLITS)

kernel_name = "hymba_gdn_dsa_deepnorm_stream_step"


def layer_norm(x, g, b):
    xf = x.astype(jnp.float32)
    mu = jnp.mean(xf, -1, keepdims=True)
    xc = xf - mu
    var = jnp.mean(xc * xc, -1, keepdims=True)
    return (xc * lax.rsqrt(var + LN_EPS) * g.astype(jnp.float32) + b.astype(jnp.float32)).astype(x.dtype)


def rope_partial(x, pos):
    d = x.shape[-1]
    rot = d // 4
    half = rot // 2
    inv_freq = ROPE_THETA ** (-jnp.arange(half, dtype=jnp.float32) / half)
    ang = pos.astype(jnp.float32)[:, None] * inv_freq[None, :]
    cos = jnp.cos(ang)[:, None, :]
    sin = jnp.sin(ang)[:, None, :]
    xf = x.astype(jnp.float32)
    x1 = xf[..., :half]
    x2 = xf[..., half:rot]
    return jnp.concatenate([x1 * cos - x2 * sin, x2 * cos + x1 * sin, xf[..., rot:]], -1).astype(x.dtype)


def causal_conv_silu(u, buf, w):
    t = u.shape[1]
    up = jnp.concatenate([buf.astype(u.dtype), u], 1)
    y = up[:, 0:t] * w[0]
    for i in range(1, CONV_W):
        y = y + up[:, i:i + t] * w[i]
    return jax.nn.silu(y), up[:, t:]


def l2_normalize(x):
    return x * lax.rsqrt(jnp.sum(x * x, -1, keepdims=True) + L2_EPS)


def gdn_core(q, k, v, g, beta, s0, chunk):
    b, t, h, dk = q.shape
    dv = v.shape[-1]
    n = t // chunk

    def to_chunks(a):
        a = a.reshape((b, n, chunk) + a.shape[2:])
        return jnp.moveaxis(jnp.moveaxis(a, 3, 2), 1, 0)

    q, k, v, g, beta = to_chunks(q), to_chunks(k), to_chunks(v), to_chunks(g), to_chunks(beta)
    gc = jnp.cumsum(g, axis=-1)
    causal = jnp.tril(jnp.ones((chunk, chunk), bool))
    strict = jnp.tril(jnp.ones((chunk, chunk), bool), -1)
    decay = jnp.exp(jnp.where(causal, gc[..., :, None] - gc[..., None, :], -jnp.inf))
    kk = jnp.einsum('nbhid,nbhjd->nbhij', k, k)
    a_mat = jnp.where(strict, beta[..., :, None] * kk * decay, 0.0)
    lower = a_mat + jnp.eye(chunk, dtype=jnp.float32)
    rhs = jnp.concatenate([v * beta[..., None], k * (beta * jnp.exp(gc))[..., None]], -1)
    sol = lax.linalg.triangular_solve(lower, rhs, left_side=True, lower=True, unit_diagonal=True)
    u0, wk = sol[..., :dv], sol[..., dv:]
    qk = jnp.einsum('nbhid,nbhjd->nbhij', q, k) * decay
    q_dec = q * jnp.exp(gc)[..., None]
    g_last = gc[..., -1]
    k_dec = k * jnp.exp(g_last[..., None] - gc)[..., None]

    def step(s, xs):
        u0c, wkc, qkc, qdc, kdc, glc = xs
        u = u0c - jnp.einsum('bhcd,bhde->bhce', wkc, s)
        o = jnp.einsum('bhcd,bhde->bhce', qdc, s) + jnp.einsum('bhij,bhje->bhie', qkc, u)
        s = s * jnp.exp(glc)[..., None, None] + jnp.einsum('bhcd,bhce->bhde', kdc, u)
        return s, o

    s_fin, o = lax.scan(step, s0, (u0, wk, qk, q_dec, k_dec, g_last))
    o = jnp.moveaxis(jnp.moveaxis(o, 0, 1), 2, 3).reshape(b, t, h, dv)
    return o, s_fin


def dsa_block(q, qi, wi, q_pos, k, v, ki, topk):
    n_keys = k.shape[1]
    limit = (q_pos // CHUNK + 1) * CHUNK
    visible = jnp.arange(n_keys)[None, :] < limit[:, None]
    s = jnp.einsum('bqhd,bld->bqhl', qi, ki).astype(jnp.float32) * (IDX_DIM ** -0.5)
    score = jnp.einsum('bqhl,bqh->bql', jax.nn.relu(s), wi.astype(jnp.float32))
    score = jnp.where(visible[None], score, -jnp.inf)
    _, idx = lax.top_k(score, topk)
    valid = idx < limit[None, :, None]
    gather = jax.vmap(lambda rows, ids: rows[ids])
    kg = gather(k, idx)
    vg = gather(v, idx)
    logits = jnp.einsum('bqhd,bqkhd->bhqk', q, kg).astype(jnp.float32) * (DSA_DH ** -0.5)
    logits = jnp.where(valid[:, None], logits, -jnp.inf)
    p = jax.nn.softmax(logits, axis=-1)
    return jnp.einsum('bhqk,bqkhd->bqhd', p.astype(vg.dtype), vg)


def token_mixer(h, pos, conv_buf, s0, past_k, past_v, past_ik, w_in, conv_w, a_log, dt_bias, gdn_norm_w, w_out, gdn_chunk, q_block):
    b, t, _ = h.shape
    proj = h @ w_in
    points = []
    acc = 0
    for width in SPLITS[:-1]:
        acc += width
        points.append(acc)
    qkv_g, z_g, b_g, a_g, q_d, k_d, v_d, q_i, k_i, w_i = jnp.split(proj, points, axis=-1)

    qkv_c, new_buf = causal_conv_silu(qkv_g, conv_buf, conv_w)
    qg, kg, vg = jnp.split(qkv_c.astype(jnp.float32), [GDN_QK, 2 * GDN_QK], axis=-1)
    qg = l2_normalize(qg.reshape(b, t, GDN_HEADS, GDN_DK)) * (GDN_DK ** -0.5)
    kg = l2_normalize(kg.reshape(b, t, GDN_HEADS, GDN_DK))
    vg = vg.reshape(b, t, GDN_HEADS, GDN_DV)
    beta = jax.nn.sigmoid(b_g.astype(jnp.float32))
    g = -jnp.exp(a_log.astype(jnp.float32)) * jax.nn.softplus(a_g.astype(jnp.float32) + dt_bias.astype(jnp.float32))
    o_g, s_new = gdn_core(qg, kg, vg, g, beta, s0.astype(jnp.float32), gdn_chunk)
    zf = z_g.astype(jnp.float32).reshape(b, t, GDN_HEADS, GDN_DV)
    o_g = o_g * lax.rsqrt(jnp.mean(o_g * o_g, -1, keepdims=True) + RMS_EPS) * gdn_norm_w.astype(jnp.float32) * jax.nn.silu(zf)
    o_g = o_g.reshape(b, t, GDN_VW).astype(h.dtype)

    qd = rope_partial(q_d.reshape(b, t, DSA_HEADS, DSA_DH), pos)
    kd = rope_partial(k_d.reshape(b, t, DSA_HEADS, DSA_DH), pos)
    vd = v_d.reshape(b, t, DSA_HEADS, DSA_DH)
    qi = rope_partial(q_i.reshape(b, t, IDX_HEADS, IDX_DIM), pos)
    ki = rope_partial(k_i[:, :, None, :], pos)[:, :, 0]
    wi = w_i * (IDX_HEADS ** -0.5)
    k_all = jnp.concatenate([past_k, kd], 1)
    v_all = jnp.concatenate([past_v, vd], 1)
    ik_all = jnp.concatenate([past_ik, ki], 1)
    topk = min(TOPK_MAX, k_all.shape[1] // 4)
    nb = t // q_block

    def blocks(a):
        return jnp.moveaxis(a.reshape((b, nb, q_block) + a.shape[2:]), 1, 0)

    o_d = lax.map(lambda xs: dsa_block(xs[0], xs[1], xs[2], xs[3], k_all, v_all, ik_all, topk),
                  (blocks(qd), blocks(qi), blocks(wi), pos.reshape(nb, q_block)))
    o_d = jnp.moveaxis(o_d, 0, 1).reshape(b, t, DSA_W)

    y = jnp.concatenate([o_g, o_d], -1) @ w_out
    return y, kd, vd, ki, new_buf, s_new.astype(h.dtype)


def swiglu(x, wg, wu, wd):
    return (jax.nn.silu(x @ wg) * (x @ wu)) @ wd


def moe_swiglu(x, w_router, wg, wu, wd):
    logits = (x @ w_router).astype(jnp.float32)
    top_v, top_i = lax.top_k(logits, TOP_K_EXPERTS)
    top_w = jax.nn.softmax(top_v, axis=-1)
    gates = jnp.sum(jax.nn.one_hot(top_i, N_EXPERTS, dtype=jnp.float32) * top_w[..., None], axis=-2)
    y = jnp.zeros(x.shape, jnp.float32)
    for e in range(N_EXPERTS):
        y = y + gates[..., e:e + 1] * swiglu(x, wg[e], wu[e], wd[e]).astype(jnp.float32)
    return y.astype(x.dtype)


def trunk(x, pos, conv_bufs, s0s, past_ks, past_vs, past_iks, gdn_chunk, q_block,
          ln_in_g, ln_in_b, w_in, conv_w, gdn_a_log, gdn_dt_bias, gdn_norm_w, w_out,
          ln1_g, ln1_b, ln2_g, ln2_b, ffn_wg, ffn_wu, ffn_wd, moe_router, moe_wg, moe_wu, moe_wd):
    h = layer_norm(x, ln_in_g, ln_in_b)
    new_k, new_v, new_ik, new_conv, new_s = [], [], [], [], []
    for l in range(DEPTH):
        mix, kn, vn, ikn, bufn, sn = token_mixer(h, pos, conv_bufs[l], s0s[l], past_ks[l], past_vs[l], past_iks[l],
                                                 w_in[l], conv_w[l], gdn_a_log[l], gdn_dt_bias[l], gdn_norm_w[l], w_out[l],
                                                 gdn_chunk, q_block)
        h = layer_norm(DEEPNORM_ALPHA * h + mix, ln1_g[l], ln1_b[l])
        if l % 2 == 0:
            f = swiglu(h, ffn_wg[l // 2], ffn_wu[l // 2], ffn_wd[l // 2])
        else:
            f = moe_swiglu(h, moe_router[l // 2], moe_wg[l // 2], moe_wu[l // 2], moe_wd[l // 2])
        h = layer_norm(DEEPNORM_ALPHA * h + f, ln2_g[l], ln2_b[l])
        new_k.append(kn)
        new_v.append(vn)
        new_ik.append(ikn)
        new_conv.append(bufn)
        new_s.append(sn)
    return h, jnp.stack(new_k), jnp.stack(new_v), jnp.stack(new_ik), jnp.stack(new_s), jnp.stack(new_conv)


def setup_inputs(seed: int = 0) -> dict:
    key = jax.random.key(seed)
    ks = jax.random.split(key, 32)
    f32 = jnp.float32

    def nrm(k, shape, scale):
        return jax.random.normal(k, shape, f32) * scale

    x_prompt = nrm(ks[0], (BATCH, SEQ, D_MODEL), 1.0)
    x_sample = nrm(ks[1], (DEC_BATCH, DEC_SEQ, D_MODEL), 1.0)
    cache_k = nrm(ks[2], (DEPTH, DEC_BATCH, PAST_LEN, DSA_HEADS, DSA_DH), 1.0)
    cache_v = nrm(ks[3], (DEPTH, DEC_BATCH, PAST_LEN, DSA_HEADS, DSA_DH), 1.0)
    cache_idx_k = nrm(ks[4], (DEPTH, DEC_BATCH, PAST_LEN, IDX_DIM), 1.0)
    state_gdn = nrm(ks[5], (DEPTH, DEC_BATCH, GDN_HEADS, GDN_DK, GDN_DV), 0.1)
    state_conv = nrm(ks[6], (DEPTH, DEC_BATCH, CONV_W - 1, GDN_CONV_CH), 1.0)
    ln_in_g = 1.0 + nrm(ks[7], (D_MODEL,), 0.02)
    ln_in_b = nrm(ks[8], (D_MODEL,), 0.02)
    col_scale = jnp.concatenate([
        jnp.ones((2 * GDN_QK,), f32), jnp.full((GDN_VW,), DEEPNORM_BETA, f32),
        jnp.ones((GDN_VW + 2 * GDN_HEADS + 2 * DSA_W,), f32), jnp.full((DSA_W,), DEEPNORM_BETA, f32),
        jnp.ones((IDX_HEADS * IDX_DIM + IDX_DIM + IDX_HEADS,), f32)])
    w_in = nrm(ks[9], (DEPTH, D_MODEL, N_IN), D_MODEL ** -0.5) * col_scale
    conv_w = nrm(ks[10], (DEPTH, CONV_W, GDN_CONV_CH), CONV_W ** -0.5)
    gdn_a_log = jnp.log(jax.random.uniform(ks[11], (DEPTH, GDN_HEADS), f32, 1.0, 16.0))
    dt = jnp.exp(jax.random.uniform(ks[12], (DEPTH, GDN_HEADS), f32, math.log(1e-3), math.log(1e-1)))
    gdn_dt_bias = dt + jnp.log(-jnp.expm1(-dt))
    gdn_norm_w = 1.0 + nrm(ks[13], (DEPTH, GDN_DV), 0.02)
    w_out = nrm(ks[14], (DEPTH, MIX_W, D_MODEL), MIX_W ** -0.5 * DEEPNORM_BETA)
    ln1_g = 1.0 + nrm(ks[15], (DEPTH, D_MODEL), 0.02)
    ln1_b = nrm(ks[16], (DEPTH, D_MODEL), 0.02)
    ln2_g = 1.0 + nrm(ks[17], (DEPTH, D_MODEL), 0.02)
    ln2_b = nrm(ks[18], (DEPTH, D_MODEL), 0.02)
    ffn_wg = nrm(ks[19], (N_DENSE, D_MODEL, D_FF), D_MODEL ** -0.5)
    ffn_wu = nrm(ks[20], (N_DENSE, D_MODEL, D_FF), D_MODEL ** -0.5)
    ffn_wd = nrm(ks[21], (N_DENSE, D_FF, D_MODEL), D_FF ** -0.5 * DEEPNORM_BETA)
    moe_router = nrm(ks[22], (N_MOE, D_MODEL, N_EXPERTS), D_MODEL ** -0.5)
    moe_wg = nrm(ks[23], (N_MOE, N_EXPERTS, D_MODEL, D_FF_EXPERT), D_MODEL ** -0.5)
    moe_wu = nrm(ks[24], (N_MOE, N_EXPERTS, D_MODEL, D_FF_EXPERT), D_MODEL ** -0.5)
    moe_wd = nrm(ks[25], (N_MOE, N_EXPERTS, D_FF_EXPERT, D_MODEL), D_FF_EXPERT ** -0.5 * DEEPNORM_BETA)
    return {"x_prompt": x_prompt, "x_sample": x_sample, "cache_k": cache_k, "cache_v": cache_v,
            "cache_idx_k": cache_idx_k, "state_gdn": state_gdn, "state_conv": state_conv,
            "ln_in_g": ln_in_g, "ln_in_b": ln_in_b, "w_in": w_in, "conv_w": conv_w,
            "gdn_a_log": gdn_a_log, "gdn_dt_bias": gdn_dt_bias, "gdn_norm_w": gdn_norm_w, "w_out": w_out,
            "ln1_g": ln1_g, "ln1_b": ln1_b, "ln2_g": ln2_g, "ln2_b": ln2_b,
            "ffn_wg": ffn_wg, "ffn_wu": ffn_wu, "ffn_wd": ffn_wd,
            "moe_router": moe_router, "moe_wg": moe_wg, "moe_wu": moe_wu, "moe_wd": moe_wd}


def reference(x_prompt, x_sample, cache_k, cache_v, cache_idx_k, state_gdn, state_conv,
              ln_in_g, ln_in_b, w_in, conv_w, gdn_a_log, gdn_dt_bias, gdn_norm_w, w_out,
              ln1_g, ln1_b, ln2_g, ln2_b, ffn_wg, ffn_wu, ffn_wd, moe_router, moe_wg, moe_wu, moe_wd):
    bp, tp = x_prompt.shape[0], x_prompt.shape[1]
    dt = x_prompt.dtype
    empty_k = jnp.zeros((DEPTH, bp, 0, DSA_HEADS, DSA_DH), dt)
    empty_ik = jnp.zeros((DEPTH, bp, 0, IDX_DIM), dt)
    zero_conv = jnp.zeros((DEPTH, bp, CONV_W - 1, GDN_CONV_CH), dt)
    zero_s = jnp.zeros((DEPTH, bp, GDN_HEADS, GDN_DK, GDN_DV), dt)
    y_prompt, k_p, v_p, ik_p, s_p, conv_p = trunk(
        x_prompt, jnp.arange(tp), zero_conv, zero_s, empty_k, empty_k, empty_ik, CHUNK, QBLOCK,
        ln_in_g, ln_in_b, w_in, conv_w, gdn_a_log, gdn_dt_bias, gdn_norm_w, w_out,
        ln1_g, ln1_b, ln2_g, ln2_b, ffn_wg, ffn_wu, ffn_wd, moe_router, moe_wg, moe_wu, moe_wd)
    past = cache_k.shape[2]
    ts = x_sample.shape[1]
    y_sample, k_s, v_s, ik_s, s_s, conv_s = trunk(
        x_sample, past + jnp.arange(ts), state_conv, state_gdn, cache_k, cache_v, cache_idx_k, ts, ts,
        ln_in_g, ln_in_b, w_in, conv_w, gdn_a_log, gdn_dt_bias, gdn_norm_w, w_out,
        ln1_g, ln1_b, ln2_g, ln2_b, ffn_wg, ffn_wu, ffn_wd, moe_router, moe_wg, moe_wu, moe_wd)
    return (y_prompt, y_sample, k_p, v_p, ik_p, s_p, conv_p, k_s, v_s, ik_s, s_s, conv_s)
```

```python
import functools

import jax
import jax.numpy as jnp
from jax import lax
from jax.experimental import pallas as pl
from jax.experimental.pallas import tpu as pltpu

CHUNK = 64
CONV_W = 4
GDN_HEADS = 4
GDN_DK = 128
GDN_DV = 128
DSA_HEADS = 8
DSA_DH = 64
IDX_HEADS = 4
IDX_DIM = 64
TOPK_MAX = 256
ROPE_THETA = 500000.0
TOP_K_EXPERTS = 2
LN_EPS = 1e-5
RMS_EPS = 1e-6
L2_EPS = 1e-6

GDN_QK = GDN_HEADS * GDN_DK
GDN_VW = GDN_HEADS * GDN_DV
GDN_CONV_CH = 2 * GDN_QK + GDN_VW
DSA_W = DSA_HEADS * DSA_DH
IDX_W = IDX_HEADS * IDX_DIM

LANES = 128
SUBLANES = 8
VMEM_LIMIT = 56 * 1024 * 1024

DSA_Q_BLOCK = 256
DSA_KEY_TILE = 512

MXU_DTYPE = jnp.bfloat16

C_QKV = 0
C_Z = C_QKV + GDN_CONV_CH
C_QD = C_Z + GDN_VW
C_KD = C_QD + DSA_W
C_VD = C_KD + DSA_W
C_QI = C_VD + DSA_W
C_KI = C_QI + IDX_W
C_SM = C_KI + LANES
C_END = C_SM + LANES
SM_BETA = 0
SM_DECAY = GDN_HEADS
SM_WI = 2 * GDN_HEADS

INT_MIN = -2 ** 31
NEG_BIG = -1e30

f32 = jnp.float32


def _mm(a, b):
    return jnp.dot(a.astype(MXU_DTYPE), b.astype(MXU_DTYPE), preferred_element_type=f32)


def _mm_nt(a, b):
    return lax.dot_general(a.astype(MXU_DTYPE), b.astype(MXU_DTYPE), (((1,), (1,)), ((), ())),
                           preferred_element_type=f32)


def _mm_tn(a, b):
    return lax.dot_general(a.astype(MXU_DTYPE), b.astype(MXU_DTYPE), (((0,), (0,)), ((), ())),
                           preferred_element_type=f32)


def _mm_f32(a, b):
    return jnp.dot(a, b, precision=lax.Precision.HIGHEST, preferred_element_type=f32)


def _sigmoid(x):
    return 1.0 / (1.0 + jnp.exp(-x))


def _layer_norm(r, g, b):
    mu = jnp.mean(r, axis=-1, keepdims=True)
    xc = r - mu
    var = jnp.mean(xc * xc, axis=-1, keepdims=True)
    return xc * lax.rsqrt(var + LN_EPS) * g + b


def _params(*sem):
    return pltpu.CompilerParams(dimension_semantics=sem, vmem_limit_bytes=VMEM_LIMIT)


def _row_tile(n, want):
    t = min(n, want)
    assert n % t == 0, (n, t)
    return t


def _ln_kernel(x_ref, g_ref, b_ref, o_ref):
    o_ref[...] = _layer_norm(x_ref[...], g_ref[...], b_ref[...])


def _ln_call(x, g, b):
    n, d = x.shape
    tm = _row_tile(n, 1024)
    return pl.pallas_call(
        _ln_kernel,
        out_shape=jax.ShapeDtypeStruct((n, d), f32),
        grid=(n // tm,),
        in_specs=[pl.BlockSpec((tm, d), lambda i: (i, 0)),
                  pl.BlockSpec((1, d), lambda i: (0, 0)),
                  pl.BlockSpec((1, d), lambda i: (0, 0))],
        out_specs=pl.BlockSpec((tm, d), lambda i: (i, 0)),
        compiler_params=_params("parallel"),
        name="ln_in",
    )(x, g.reshape(1, d), b.reshape(1, d))


def _rope(x, cos_t, sin_lo, sin_hi):
    w = x.shape[1]
    reps = w // LANES
    c = jnp.tile(cos_t, (1, reps))
    lo = jnp.tile(sin_lo, (1, reps))
    hi = jnp.tile(sin_hi, (1, reps))
    half = DSA_DH // 8
    from_above = pltpu.roll(x, w - half, 1)
    from_below = pltpu.roll(x, half, 1)
    return x * c + from_above * lo + from_below * hi


def _inproj_kernel(x_ref, w_ref, cos_ref, lo_ref, hi_ref,
                   qkv_o, z_o, k_o, v_o, ki_o, qb_o, kb_o, vb_o, qib_o, kib_o, sm_o):
    xb = x_ref[...].astype(MXU_DTYPE)
    cos_t, lo, hi = cos_ref[...], lo_ref[...], hi_ref[...]

    def proj(c0, c1):
        return jnp.dot(xb, w_ref[:, c0:c1], preferred_element_type=f32)

    qkv_o[...] = proj(C_QKV, C_Z)
    z_o[...] = proj(C_Z, C_QD)
    q = _rope(proj(C_QD, C_KD), cos_t, lo, hi) * (DSA_DH ** -0.5)
    qb_o[...] = q.astype(qb_o.dtype)
    k = _rope(proj(C_KD, C_VD), cos_t, lo, hi)
    k_o[...] = k
    kb_o[...] = k.astype(kb_o.dtype)
    v = proj(C_VD, C_QI)
    v_o[...] = v
    vb_o[...] = v.astype(vb_o.dtype)
    qi = _rope(proj(C_QI, C_KI), cos_t, lo, hi) * (IDX_DIM ** -0.5)
    qib_o[...] = qi.astype(qib_o.dtype)
    ki = _rope(proj(C_KI, C_SM), cos_t, lo, hi)
    ki_o[...] = ki[:, :IDX_DIM]
    kib_o[...] = ki.astype(kib_o.dtype)
    sm_o[...] = proj(C_SM, C_END)


def _inproj_call(h, w_packed, tabs, seq_len):
    n, d = h.shape
    tm = _row_tile(n, 512)
    cos_t, lo, hi = tabs
    if tm > seq_len:
        assert tm % seq_len == 0
        cos_t, lo, hi = (jnp.tile(a, (tm // seq_len, 1)) for a in (cos_t, lo, hi))
        nblk = 1
    else:
        assert seq_len % tm == 0
        nblk = seq_len // tm
    tab_spec = pl.BlockSpec((tm, LANES), lambda i: (i % nblk, 0))

    def rows(w):
        return pl.BlockSpec((tm, w), lambda i: (i, 0))

    widths = [(GDN_CONV_CH, f32), (GDN_VW, f32), (DSA_W, f32), (DSA_W, f32), (IDX_DIM, f32),
              (DSA_W, MXU_DTYPE), (DSA_W, MXU_DTYPE), (DSA_W, MXU_DTYPE), (IDX_W, MXU_DTYPE),
              (LANES, MXU_DTYPE), (LANES, f32)]
    return pl.pallas_call(
        _inproj_kernel,
        out_shape=[jax.ShapeDtypeStruct((n, w), dt) for w, dt in widths],
        grid=(n // tm,),
        in_specs=[rows(d),
                  pl.BlockSpec((d, C_END), lambda i: (0, 0), pipeline_mode=pl.Buffered(1)),
                  tab_spec, tab_spec, tab_spec],
        out_specs=[rows(w) for w, _ in widths],
        compiler_params=_params("parallel"),
        name="inproj",
    )(h, w_packed, cos_t, lo, hi)


def _unit_lower_inverse(a, size):
    r = lax.broadcasted_iota(jnp.int32, (size, size), 0)
    c = lax.broadcasted_iota(jnp.int32, (size, size), 1)

    def same_block(n):
        return (r // n) == (c // n)

    base = SUBLANES
    a0 = jnp.where(same_block(base), a, 0.0)
    p = _mm_f32(a0, a0)
    x = jnp.where(r == c, 1.0, 0.0) - a0
    x = x + _mm_f32(x, p)
    p = _mm_f32(p, p)
    x = x + _mm_f32(x, p)
    n = base
    while n < size:
        off = jnp.where(same_block(2 * n), jnp.where(same_block(n), 0.0, a), 0.0)
        x = x - _mm_f32(_mm_f32(x, off), x)
        n *= 2
    return x


def _gdn_kernel(qkv_ref, z_ref, sm_ref, cw_ref, cb_ref, s0_ref, par_ref, nw_ref,
                o_ref, s_ref, ubuf, *, chunk):
    step = pl.program_id(1)

    @pl.when(step == 0)
    def _():
        ubuf[0:SUBLANES, :] = cb_ref[0]
        s_ref[0] = s0_ref[0]

    ubuf[SUBLANES:SUBLANES + chunk, :] = qkv_ref[0]
    y = ubuf[SUBLANES - 3:SUBLANES - 3 + chunk, :] * cw_ref[0:1, :]
    for i in range(1, CONV_W):
        y = y + ubuf[SUBLANES - 3 + i:SUBLANES - 3 + i + chunk, :] * cw_ref[i:i + 1, :]
    ubuf[0:SUBLANES, :] = ubuf[chunk:chunk + SUBLANES, :]
    y = y * _sigmoid(y)

    sm = sm_ref[0]
    beta_all = _sigmoid(sm)
    xg = sm + par_ref[1:2, :]
    softplus = jnp.maximum(xg, 0.0) + jnp.log1p(jnp.exp(-jnp.abs(xg)))
    g_all = -jnp.exp(par_ref[0:1, :]) * softplus

    r = lax.broadcasted_iota(jnp.int32, (chunk, chunk), 0)
    c = lax.broadcasted_iota(jnp.int32, (chunk, chunk), 1)
    causal = r >= c
    strict = r > c
    gc_all = _mm_f32(jnp.where(causal, 1.0, 0.0), g_all)
    gc_rows = gc_all.T
    z = z_ref[0]
    nw = nw_ref[...]

    for h in range(GDN_HEADS):
        lane = slice(h * GDN_DK, (h + 1) * GDN_DK)
        q = y[:, h * GDN_DK:(h + 1) * GDN_DK]
        k = y[:, GDN_QK + h * GDN_DK:GDN_QK + (h + 1) * GDN_DK]
        v = y[:, 2 * GDN_QK + h * GDN_DV:2 * GDN_QK + (h + 1) * GDN_DV]
        q = q * lax.rsqrt(jnp.sum(q * q, -1, keepdims=True) + L2_EPS) * (GDN_DK ** -0.5)
        k = k * lax.rsqrt(jnp.sum(k * k, -1, keepdims=True) + L2_EPS)
        beta = beta_all[:, SM_BETA + h:SM_BETA + h + 1]
        gc = gc_all[:, SM_DECAY + h:SM_DECAY + h + 1]
        gc_row = gc_rows[SM_DECAY + h:SM_DECAY + h + 1, :]
        g_last = gc_all[chunk - 1:chunk, SM_DECAY + h:SM_DECAY + h + 1]
        decay = jnp.exp(jnp.where(causal, gc - gc_row, -jnp.inf))
        a_mat = jnp.where(strict, beta * _mm_nt(k, k) * decay, 0.0)
        t_inv = _unit_lower_inverse(a_mat, chunk)
        rhs = jnp.concatenate([v * beta, k * (beta * jnp.exp(gc))], axis=-1)
        sol = _mm_f32(t_inv, rhs)
        u0, wk = sol[:, :GDN_DV], sol[:, GDN_DV:]
        qk = _mm_nt(q, k) * decay
        q_dec = q * jnp.exp(gc)
        k_dec = k * jnp.exp(g_last - gc)
        s = s_ref[0, h]
        u = u0 - _mm(wk, s)
        o = _mm(q_dec, s) + _mm(qk, u)
        s_ref[0, h] = s * jnp.exp(g_last) + _mm_tn(k_dec, u)
        zh = z[:, lane]
        o = o * lax.rsqrt(jnp.mean(o * o, -1, keepdims=True) + RMS_EPS) * nw * (zh * _sigmoid(zh))
        o_ref[0, :, lane] = o.astype(o_ref.dtype)


def _gdn_call(qkv, z, sm, conv_w, conv_buf8, s0, par, norm_w, chunk):
    b, t, _ = qkv.shape
    n = t // chunk
    return pl.pallas_call(
        functools.partial(_gdn_kernel, chunk=chunk),
        out_shape=[jax.ShapeDtypeStruct((b, t, GDN_VW), MXU_DTYPE),
                   jax.ShapeDtypeStruct((b, GDN_HEADS, GDN_DK, GDN_DV), f32)],
        grid=(b, n),
        in_specs=[pl.BlockSpec((1, chunk, GDN_CONV_CH), lambda i, j: (i, j, 0)),
                  pl.BlockSpec((1, chunk, GDN_VW), lambda i, j: (i, j, 0)),
                  pl.BlockSpec((1, chunk, LANES), lambda i, j: (i, j, 0)),
                  pl.BlockSpec((CONV_W, GDN_CONV_CH), lambda i, j: (0, 0)),
                  pl.BlockSpec((1, SUBLANES, GDN_CONV_CH), lambda i, j: (i, 0, 0)),
                  pl.BlockSpec((1, GDN_HEADS, GDN_DK, GDN_DV), lambda i, j: (i, 0, 0, 0)),
                  pl.BlockSpec((2, LANES), lambda i, j: (0, 0)),
                  pl.BlockSpec((1, GDN_DV), lambda i, j: (0, 0))],
        out_specs=[pl.BlockSpec((1, chunk, GDN_VW), lambda i, j: (i, j, 0)),
                   pl.BlockSpec((1, GDN_HEADS, GDN_DK, GDN_DV), lambda i, j: (i, 0, 0, 0))],
        scratch_shapes=[pltpu.VMEM((SUBLANES + chunk, GDN_CONV_CH), f32)],
        compiler_params=_params("parallel", "arbitrary"),
        name="gdn",
    )(qkv, z, sm, conv_w, conv_buf8, s0, par, norm_w)


def _dsa_kernel(q_ref, qi_ref, sm_ref, k_ref, v_ref, ki_ref, o_ref,
                ikey, m_s, l_s, acc_s, *, qb, tk, pos0, n_keys, topk):
    blk = pl.program_id(1)
    row = lax.broadcasted_iota(jnp.int32, (qb, 1), 0)
    q_pos = pos0 + blk * qb + row
    limit = jnp.minimum((q_pos // CHUNK + 1) * CHUNK, n_keys)
    last_limit = jnp.minimum(((pos0 + (blk + 1) * qb - 1) // CHUNK + 1) * CHUNK, n_keys)
    n_tiles = (last_limit + tk - 1) // tk
    lane = lax.broadcasted_iota(jnp.int32, (qb, LANES), 1)
    lower_half = lane < DSA_DH
    col = lax.broadcasted_iota(jnp.int32, (qb, tk), 1)
    groups = tk // LANES

    sm = sm_ref[0]
    qi_heads = []
    for h in range(IDX_HEADS):
        pair = qi_ref[0, :, (h // 2) * LANES:(h // 2 + 1) * LANES]
        keep = lower_half if h % 2 == 0 else jnp.logical_not(lower_half)
        qi_heads.append(jnp.where(keep, pair, jnp.zeros_like(pair)))
    wi = [sm[:, SM_WI + h:SM_WI + h + 1] * (IDX_HEADS ** -0.5) for h in range(IDX_HEADS)]

    def score_tile(kt, carry):
        ki_t = ki_ref[0, pl.ds(pl.multiple_of(kt * tk, tk), tk), :]
        score = jnp.zeros((qb, tk), f32)
        for h in range(IDX_HEADS):
            score = score + jnp.maximum(_mm_nt(qi_heads[h], ki_t), 0.0) * wi[h]
        bits = pltpu.bitcast(score, jnp.int32)
        key = bits ^ ((bits >> 31) & 0x7FFFFFFF)
        ikey[kt] = jnp.where(kt * tk + col < limit, key, INT_MIN)
        return carry

    lax.fori_loop(0, n_tiles, score_tile, 0)

    def count(pred):
        def body(kt, acc):
            hit = jnp.where(pred(ikey[kt]), 1, 0)
            part = hit[:, 0:LANES]
            for g in range(1, groups):
                part = part + hit[:, g * LANES:(g + 1) * LANES]
            return acc + part
        acc = lax.fori_loop(0, n_tiles, body, jnp.zeros((qb, LANES), jnp.int32))
        return jnp.sum(acc, axis=1, keepdims=True)

    def bit_step(i, u):
        cand = u | (jnp.int32(1) << (31 - i))
        cand_s = cand ^ INT_MIN
        return jnp.where(count(lambda kk: kk >= cand_s) >= topk, cand, u)

    u = lax.fori_loop(0, 32, bit_step, jnp.zeros((qb, 1), jnp.int32))
    thr = jnp.maximum(u ^ INT_MIN, INT_MIN + 1)
    n_gt = count(lambda kk: kk > thr)
    n_ge = count(lambda kk: kk >= thr)
    need = (topk - n_gt).astype(f32)
    surplus = jnp.max(jnp.where(n_ge > topk, 1, 0))

    q_heads = []
    for h in range(DSA_HEADS):
        pair = q_ref[0, :, (h // 2) * LANES:(h // 2 + 1) * LANES]
        keep = lower_half if h % 2 == 0 else jnp.logical_not(lower_half)
        q_heads.append(jnp.where(keep, pair, jnp.zeros_like(pair)))

    def attend(exact_ties):
        m_s[...] = jnp.full(m_s.shape, NEG_BIG, f32)
        l_s[...] = jnp.zeros(l_s.shape, f32)
        acc_s[...] = jnp.zeros(acc_s.shape, f32)
        if exact_ties:
            tr = lax.broadcasted_iota(jnp.int32, (tk, tk), 0)
            tc = lax.broadcasted_iota(jnp.int32, (tk, tk), 1)
            prefix = jnp.where(tr <= tc, 1.0, 0.0).astype(MXU_DTYPE)

        def body(kt, taken):
            key = ikey[kt]
            if exact_ties:
                eq = key == thr
                rank = taken + _mm(jnp.where(eq, 1.0, 0.0), prefix)
                sel = jnp.logical_or(key > thr, jnp.logical_and(eq, rank <= need))
                taken = rank[:, tk - 1:tk]
            else:
                sel = key >= thr
            start = pl.multiple_of(kt * tk, tk)
            for h in range(DSA_HEADS):
                pr = slice((h // 2) * LANES, (h // 2 + 1) * LANES)
                logits = _mm_nt(q_heads[h], k_ref[0, pl.ds(start, tk), pr])
                m_old = m_s[h]
                m_new = jnp.maximum(m_old, jnp.max(jnp.where(sel, logits, NEG_BIG), axis=1, keepdims=True))
                alpha = jnp.exp(m_old - m_new)
                p = jnp.where(sel, jnp.exp(logits - m_new), 0.0)
                l_s[h] = alpha * l_s[h] + jnp.sum(p, axis=1, keepdims=True)
                acc_s[h] = alpha * acc_s[h] + _mm(p, v_ref[0, pl.ds(start, tk), pr])
                m_s[h] = m_new
            return taken

        lax.fori_loop(0, n_tiles, body, jnp.zeros((qb, 1), f32))
        for pi in range(DSA_HEADS // 2):
            even = acc_s[2 * pi] / l_s[2 * pi]
            odd = acc_s[2 * pi + 1] / l_s[2 * pi + 1]
            o_ref[0, :, pi * LANES:(pi + 1) * LANES] = jnp.where(lower_half, even, odd).astype(o_ref.dtype)

    @pl.when(surplus == 0)
    def _():
        attend(False)

    @pl.when(surplus != 0)
    def _():
        attend(True)


def _dsa_call(qb_arr, qib_arr, sm, k_all, v_all, ki_all, pos0, n_keys, qb, tk):
    b, t, _ = qb_arr.shape
    lp = k_all.shape[1]
    assert lp % tk == 0 and t % qb == 0
    topk = min(TOPK_MAX, n_keys // 4)
    kern = functools.partial(_dsa_kernel, qb=qb, tk=tk, pos0=pos0, n_keys=n_keys, topk=topk)
    resident = dict(pipeline_mode=pl.Buffered(1))
    return pl.pallas_call(
        kern,
        out_shape=jax.ShapeDtypeStruct((b, t, DSA_W), MXU_DTYPE),
        grid=(b, t // qb),
        in_specs=[pl.BlockSpec((1, qb, DSA_W), lambda i, j: (i, j, 0)),
                  pl.BlockSpec((1, qb, IDX_W), lambda i, j: (i, j, 0)),
                  pl.BlockSpec((1, qb, LANES), lambda i, j: (i, j, 0)),
                  pl.BlockSpec((1, lp, DSA_W), lambda i, j: (i, 0, 0), **resident),
                  pl.BlockSpec((1, lp, DSA_W), lambda i, j: (i, 0, 0), **resident),
                  pl.BlockSpec((1, lp, LANES), lambda i, j: (i, 0, 0), **resident)],
        out_specs=pl.BlockSpec((1, qb, DSA_W), lambda i, j: (i, j, 0)),
        scratch_shapes=[pltpu.VMEM((lp // tk, qb, tk), jnp.int32),
                        pltpu.VMEM((DSA_HEADS, qb, 1), f32),
                        pltpu.VMEM((DSA_HEADS, qb, 1), f32),
                        pltpu.VMEM((DSA_HEADS, qb, LANES), f32)],
        compiler_params=_params("parallel", "arbitrary"),
        name="dsa",
    )(qb_arr, qib_arr, sm, k_all, v_all, ki_all)


def _outproj_kernel(h_ref, og_ref, od_ref, w_ref, g_ref, b_ref, o_ref, *, alpha):
    mix = (jnp.dot(og_ref[...], w_ref[0:GDN_VW, :], preferred_element_type=f32)
           + jnp.dot(od_ref[...], w_ref[GDN_VW:, :], preferred_element_type=f32))
    o_ref[...] = _layer_norm(alpha * h_ref[...] + mix, g_ref[...], b_ref[...])


def _outproj_call(h, og, od, w, g, b, alpha):
    n, d = h.shape
    tm = _row_tile(n, 512)
    return pl.pallas_call(
        functools.partial(_outproj_kernel, alpha=alpha),
        out_shape=jax.ShapeDtypeStruct((n, d), f32),
        grid=(n // tm,),
        in_specs=[pl.BlockSpec((tm, d), lambda i: (i, 0)),
                  pl.BlockSpec((tm, GDN_VW), lambda i: (i, 0)),
                  pl.BlockSpec((tm, DSA_W), lambda i: (i, 0)),
                  pl.BlockSpec((GDN_VW + DSA_W, d), lambda i: (0, 0)),
                  pl.BlockSpec((1, d), lambda i: (0, 0)),
                  pl.BlockSpec((1, d), lambda i: (0, 0))],
        out_specs=pl.BlockSpec((tm, d), lambda i: (i, 0)),
        compiler_params=_params("parallel"),
        name="outproj_ln",
    )(h, og, od, w, g.reshape(1, d), b.reshape(1, d))


def _ffn_kernel(x_ref, wg_ref, wu_ref, wd_ref, g_ref, b_ref, o_ref, acc, *, alpha):
    j = pl.program_id(1)

    @pl.when(j == 0)
    def _():
        acc[...] = jnp.zeros(acc.shape, f32)

    xb = x_ref[...].astype(MXU_DTYPE)
    gate = jnp.dot(xb, wg_ref[...], preferred_element_type=f32)
    up = jnp.dot(xb, wu_ref[...], preferred_element_type=f32)
    act = gate * _sigmoid(gate) * up
    acc[...] += jnp.dot(act.astype(MXU_DTYPE), wd_ref[...], preferred_element_type=f32)

    @pl.when(j == pl.num_programs(1) - 1)
    def _():
        o_ref[...] = _layer_norm(alpha * x_ref[...] + acc[...], g_ref[...], b_ref[...])


def _ffn_call(x, wg, wu, wd, g, b, alpha):
    n, d = x.shape
    f = wg.shape[1]
    tm = _row_tile(n, 1024)
    tf = _row_tile(f, 512)
    return pl.pallas_call(
        functools.partial(_ffn_kernel, alpha=alpha),
        out_shape=jax.ShapeDtypeStruct((n, d), f32),
        grid=(n // tm, f // tf),
        in_specs=[pl.BlockSpec((tm, d), lambda i, j: (i, 0)),
                  pl.BlockSpec((d, tf), lambda i, j: (0, j)),
                  pl.BlockSpec((d, tf), lambda i, j: (0, j)),
                  pl.BlockSpec((tf, d), lambda i, j: (j, 0)),
                  pl.BlockSpec((1, d), lambda i, j: (0, 0)),
                  pl.BlockSpec((1, d), lambda i, j: (0, 0))],
        out_specs=pl.BlockSpec((tm, d), lambda i, j: (i, 0)),
        scratch_shapes=[pltpu.VMEM((tm, d), f32)],
        compiler_params=_params("parallel", "arbitrary"),
        name="ffn_ln",
    )(x, wg, wu, wd, g.reshape(1, d), b.reshape(1, d))


def _moe_kernel(x_ref, wr_ref, wg_ref, wu_ref, wd_ref, g_ref, b_ref, o_ref,
                acc, gates, gcol, *, alpha, n_experts):
    e = pl.program_id(1)
    j = pl.program_id(2)
    tm = x_ref.shape[0]
    lane = lax.broadcasted_iota(jnp.int32, (tm, LANES), 1)

    @pl.when(jnp.logical_and(e == 0, j == 0))
    def _():
        acc[...] = jnp.zeros(acc.shape, f32)
        logits = _mm_f32(x_ref[...], wr_ref[...])
        logits = jnp.where(lane < n_experts, logits, -jnp.inf)
        m1 = jnp.max(logits, axis=1, keepdims=True)
        i1 = jnp.min(jnp.where(logits == m1, lane, LANES), axis=1, keepdims=True)
        rest = jnp.where(lane == i1, -jnp.inf, logits)
        m2 = jnp.max(rest, axis=1, keepdims=True)
        i2 = jnp.min(jnp.where(rest == m2, lane, LANES), axis=1, keepdims=True)
        e2 = jnp.exp(m2 - m1)
        w1 = 1.0 / (1.0 + e2)
        gates[...] = jnp.where(lane == i1, w1, 0.0) + jnp.where(lane == i2, e2 * w1, 0.0)

    @pl.when(j == 0)
    def _():
        gcol[...] = jnp.sum(jnp.where(lane == e, gates[...], 0.0), axis=1, keepdims=True)

    xb = x_ref[...].astype(MXU_DTYPE)
    gate = jnp.dot(xb, wg_ref[0], preferred_element_type=f32)
    up = jnp.dot(xb, wu_ref[0], preferred_element_type=f32)
    act = gate * _sigmoid(gate) * up * gcol[...]
    acc[...] += jnp.dot(act.astype(MXU_DTYPE), wd_ref[0], preferred_element_type=f32)

    @pl.when(jnp.logical_and(e == n_experts - 1, j == pl.num_programs(2) - 1))
    def _():
        o_ref[...] = _layer_norm(alpha * x_ref[...] + acc[...], g_ref[...], b_ref[...])


def _moe_call(x, wr, wg, wu, wd, g, b, alpha):
    n, d = x.shape
    n_experts, _, f = wg.shape
    tm = _row_tile(n, 1024)
    tf = _row_tile(f, 512)
    return pl.pallas_call(
        functools.partial(_moe_kernel, alpha=alpha, n_experts=n_experts),
        out_shape=jax.ShapeDtypeStruct((n, d), f32),
        grid=(n // tm, n_experts, f // tf),
        in_specs=[pl.BlockSpec((tm, d), lambda i, e, j: (i, 0)),
                  pl.BlockSpec((d, LANES), lambda i, e, j: (0, 0)),
                  pl.BlockSpec((1, d, tf), lambda i, e, j: (e, 0, j)),
                  pl.BlockSpec((1, d, tf), lambda i, e, j: (e, 0, j)),
                  pl.BlockSpec((1, tf, d), lambda i, e, j: (e, j, 0)),
                  pl.BlockSpec((1, d), lambda i, e, j: (0, 0)),
                  pl.BlockSpec((1, d), lambda i, e, j: (0, 0))],
        out_specs=pl.BlockSpec((tm, d), lambda i, e, j: (i, 0)),
        scratch_shapes=[pltpu.VMEM((tm, d), f32),
                        pltpu.VMEM((tm, LANES), f32),
                        pltpu.VMEM((tm, 1), f32)],
        compiler_params=_params("parallel", "arbitrary", "arbitrary"),
        name="moe_ln",
    )(x, wr, wg, wu, wd, g.reshape(1, d), b.reshape(1, d))


def _pack_w_in(w_in):
    o_z = GDN_CONV_CH
    o_b = o_z + GDN_VW
    o_a = o_b + GDN_HEADS
    o_qd = o_a + GDN_HEADS
    o_kd = o_qd + DSA_W
    o_vd = o_kd + DSA_W
    o_qi = o_vd + DSA_W
    o_ki = o_qi + IDX_W
    o_wi = o_ki + IDX_DIM
    assert w_in.shape[-1] == o_wi + IDX_HEADS
    sl = lambda a, n: w_in[..., a:a + n]
    pad = jnp.zeros(w_in.shape[:-1] + (LANES - 2 * GDN_HEADS - IDX_HEADS,), w_in.dtype)
    packed = jnp.concatenate([
        sl(0, GDN_CONV_CH), sl(o_z, GDN_VW), sl(o_qd, DSA_W), sl(o_kd, DSA_W), sl(o_vd, DSA_W),
        sl(o_qi, IDX_W), sl(o_ki, IDX_DIM), sl(o_ki, IDX_DIM),
        sl(o_b, GDN_HEADS), sl(o_a, GDN_HEADS), sl(o_wi, IDX_HEADS), pad], axis=-1)
    assert packed.shape[-1] == C_END
    return packed.astype(MXU_DTYPE)


def _rope_tables(pos):
    half = DSA_DH // 8
    inv_freq = ROPE_THETA ** (-jnp.arange(half, dtype=f32) / half)
    ang = pos.astype(f32)[:, None] * inv_freq[None, :]
    cos, sin = jnp.cos(ang), jnp.sin(ang)
    t = pos.shape[0]
    ones = jnp.ones((t, DSA_DH - 2 * half), f32)
    zeros = jnp.zeros((t, DSA_DH - 2 * half), f32)
    z8 = jnp.zeros((t, half), f32)
    cos_t = jnp.concatenate([cos, cos, ones], -1)
    lo = jnp.concatenate([-sin, z8, zeros], -1)
    hi = jnp.concatenate([z8, sin, zeros], -1)
    return tuple(jnp.tile(a, (1, LANES // DSA_DH)) for a in (cos_t, lo, hi))


def _gdn_params(a_log, dt_bias):
    row = lambda v: jnp.zeros((LANES,), f32).at[SM_DECAY:SM_DECAY + GDN_HEADS].set(v.astype(f32))
    return jnp.stack([row(a_log), row(dt_bias)])


def _trunk(x, pos0, conv_bufs, s0s, past_k, past_v, past_ik, chunk, wts, depth):
    b, t, d = x.shape
    n = b * t
    alpha = (2.0 * depth) ** 0.25
    tabs = _rope_tables(pos0 + jnp.arange(t))
    past = 0 if past_k is None else past_k.shape[2]
    n_keys = past + t
    assert t >= CONV_W - 1
    qb = min(t, DSA_Q_BLOCK)
    tk = DSA_KEY_TILE
    lp = -(-n_keys // tk) * tk

    def with_past(new, old, l):
        parts = [] if old is None else [old[l].reshape(b, past, -1).astype(MXU_DTYPE)]
        parts.append(new.reshape(b, t, -1))
        if lp > n_keys:
            parts.append(jnp.zeros((b, lp - n_keys, new.shape[-1]), MXU_DTYPE))
        return parts[0] if len(parts) == 1 else jnp.concatenate(parts, axis=1)

    h = _ln_call(x.reshape(n, d), wts["ln_in_g"], wts["ln_in_b"])
    new_k, new_v, new_ik, new_conv, new_s = [], [], [], [], []
    for l in range(depth):
        (qkv, z, k_f, v_f, ki_f, q_b, k_b, v_b, qi_b, ki_b, sm) = _inproj_call(h, wts["w_in"][l], tabs, t)
        qkv3 = qkv.reshape(b, t, GDN_CONV_CH)
        buf8 = jnp.concatenate([jnp.zeros((b, SUBLANES - (CONV_W - 1), GDN_CONV_CH), f32), conv_bufs[l]], axis=1)
        o_g, s_new = _gdn_call(qkv3, z.reshape(b, t, GDN_VW), sm.reshape(b, t, LANES), wts["conv_w"][l],
                               buf8, s0s[l], _gdn_params(wts["gdn_a_log"][l], wts["gdn_dt_bias"][l]),
                               wts["gdn_norm_w"][l].reshape(1, GDN_DV), chunk)
        ki_dup = None if past_ik is None else jnp.concatenate([past_ik, past_ik], axis=-1)
        o_d = _dsa_call(q_b.reshape(b, t, DSA_W), qi_b.reshape(b, t, IDX_W), sm.reshape(b, t, LANES),
                        with_past(k_b, past_k, l), with_past(v_b, past_v, l), with_past(ki_b, ki_dup, l),
                        past, n_keys, qb, tk)
        h = _outproj_call(h, o_g.reshape(n, GDN_VW), o_d.reshape(n, DSA_W), wts["w_out"][l],
                          wts["ln1_g"][l], wts["ln1_b"][l], alpha)
        if l % 2 == 0:
            i = l // 2
            h = _ffn_call(h, wts["ffn_wg"][i], wts["ffn_wu"][i], wts["ffn_wd"][i],
                          wts["ln2_g"][l], wts["ln2_b"][l], alpha)
        else:
            i = l // 2
            h = _moe_call(h, wts["moe_router"][i], wts["moe_wg"][i], wts["moe_wu"][i], wts["moe_wd"][i],
                          wts["ln2_g"][l], wts["ln2_b"][l], alpha)
        new_k.append(k_f.reshape(b, t, DSA_HEADS, DSA_DH))
        new_v.append(v_f.reshape(b, t, DSA_HEADS, DSA_DH))
        new_ik.append(ki_f.reshape(b, t, IDX_DIM))
        new_conv.append(qkv3[:, t - (CONV_W - 1):])
        new_s.append(s_new)
    return (h.reshape(b, t, d), jnp.stack(new_k), jnp.stack(new_v), jnp.stack(new_ik),
            jnp.stack(new_s), jnp.stack(new_conv))


def kernel(x_prompt, x_sample, cache_k, cache_v, cache_idx_k, state_gdn, state_conv, ln_in_g, ln_in_b, w_in, conv_w, gdn_a_log, gdn_dt_bias, gdn_norm_w, w_out, ln1_g, ln1_b, ln2_g, ln2_b, ffn_wg, ffn_wu, ffn_wd, moe_router, moe_wg, moe_wu, moe_wd):
    depth = w_in.shape[0]
    n_experts = moe_router.shape[-1]
    bp, tp, d = x_prompt.shape
    cast = lambda a: a.astype(MXU_DTYPE)
    router = jnp.concatenate(
        [moe_router.astype(f32), jnp.zeros(moe_router.shape[:-1] + (LANES - n_experts,), f32)], axis=-1)
    wts = dict(ln_in_g=ln_in_g, ln_in_b=ln_in_b, w_in=_pack_w_in(w_in), conv_w=conv_w,
               gdn_a_log=gdn_a_log, gdn_dt_bias=gdn_dt_bias, gdn_norm_w=gdn_norm_w, w_out=cast(w_out),
               ln1_g=ln1_g, ln1_b=ln1_b, ln2_g=ln2_g, ln2_b=ln2_b,
               ffn_wg=cast(ffn_wg), ffn_wu=cast(ffn_wu), ffn_wd=cast(ffn_wd),
               moe_router=router, moe_wg=cast(moe_wg), moe_wu=cast(moe_wu), moe_wd=cast(moe_wd))
    zero_conv = jnp.zeros((depth, bp, CONV_W - 1, GDN_CONV_CH), f32)
    zero_s = jnp.zeros((depth, bp, GDN_HEADS, GDN_DK, GDN_DV), f32)
    y_p, k_p, v_p, ik_p, s_p, conv_p = _trunk(
        x_prompt, 0, zero_conv, zero_s, None, None, None, CHUNK, wts, depth)
    ts = x_sample.shape[1]
    past = cache_k.shape[2]
    y_s, k_s, v_s, ik_s, s_s, conv_s = _trunk(
        x_sample, past, state_conv, state_gdn, cache_k, cache_v, cache_idx_k, ts, wts, depth)
    return (y_p, y_s, k_p, v_p, ik_p, s_p, conv_p, k_s, v_s, ik_s, s_s, conv_s)
```

```python
import functools

import jax
import jax.numpy as jnp
from jax import lax
from jax.experimental import pallas as pl
from jax.experimental.pallas import tpu as pltpu

CHUNK = 64
CONV_W = 4
GDN_HEADS = 4
GDN_DK = 128
GDN_DV = 128
DSA_HEADS = 8
DSA_DH = 64
IDX_HEADS = 4
IDX_DIM = 64
TOPK_MAX = 256
ROPE_THETA = 500000.0
TOP_K_EXPERTS = 2
LN_EPS = 1e-5
RMS_EPS = 1e-6
L2_EPS = 1e-6

GDN_QK = GDN_HEADS * GDN_DK
GDN_VW = GDN_HEADS * GDN_DV
GDN_CONV_CH = 2 * GDN_QK + GDN_VW
DSA_W = DSA_HEADS * DSA_DH
IDX_W = IDX_HEADS * IDX_DIM

LANES = 128
SUBLANES = 8
VMEM_LIMIT = 56 * 1024 * 1024

DSA_Q_BLOCK = 256
DSA_KEY_TILE = 512
GDN_CHUNKS_PER_STEP = 4

MXU_DTYPE = jnp.bfloat16

C_QKV = 0
C_Z = C_QKV + GDN_CONV_CH
C_QD = C_Z + GDN_VW
C_KD = C_QD + DSA_W
C_VD = C_KD + DSA_W
C_QI = C_VD + DSA_W
C_KI = C_QI + IDX_W
C_SM = C_KI + LANES
C_END = C_SM + LANES
SM_BETA = 0
SM_DECAY = GDN_HEADS
SM_WI = 2 * GDN_HEADS

INT_MIN = -2 ** 31
NEG_BIG = -3e38
M_FLOOR = -1e38
LOG2E = 1.4426950408889634

f32 = jnp.float32


def _mm(a, b):
    return jnp.dot(a.astype(MXU_DTYPE), b.astype(MXU_DTYPE), preferred_element_type=f32)


def _mm_nt(a, b):
    return lax.dot_general(a.astype(MXU_DTYPE), b.astype(MXU_DTYPE), (((1,), (1,)), ((), ())),
                           preferred_element_type=f32)


def _mm_tn(a, b):
    return lax.dot_general(a.astype(MXU_DTYPE), b.astype(MXU_DTYPE), (((0,), (0,)), ((), ())),
                           preferred_element_type=f32)


def _mm_f32(a, b):
    return jnp.dot(a, b, precision=lax.Precision.HIGHEST, preferred_element_type=f32)


def _split(x):
    hi = x.astype(MXU_DTYPE)
    lo = (x - hi.astype(f32)).astype(MXU_DTYPE)
    return hi, lo


def _mm_x3(a, b):
    a_hi, a_lo = _split(a)
    b_hi, b_lo = _split(b)
    dot = functools.partial(jnp.dot, preferred_element_type=f32)
    return dot(a_hi, b_hi) + (dot(a_hi, b_lo) + dot(a_lo, b_hi))


def _sigmoid(x):
    return 1.0 / (1.0 + jnp.exp(-x))


def _layer_norm(r, g, b):
    mu = jnp.mean(r, axis=-1, keepdims=True)
    xc = r - mu
    var = jnp.mean(xc * xc, axis=-1, keepdims=True)
    return xc * lax.rsqrt(var + LN_EPS) * g + b


def _params(*sem):
    return pltpu.CompilerParams(dimension_semantics=sem, vmem_limit_bytes=VMEM_LIMIT)


def _row_tile(n, want):
    t = min(n, want)
    assert n % t == 0, (n, t)
    return t


def _ln_kernel(x_ref, g_ref, b_ref, o_ref):
    o_ref[...] = _layer_norm(x_ref[...], g_ref[...], b_ref[...])


def _ln_call(x, g, b):
    n, d = x.shape
    tm = _row_tile(n, 1024)
    return pl.pallas_call(
        _ln_kernel,
        out_shape=jax.ShapeDtypeStruct((n, d), f32),
        grid=(n // tm,),
        in_specs=[pl.BlockSpec((tm, d), lambda i: (i, 0)),
                  pl.BlockSpec((1, d), lambda i: (0, 0)),
                  pl.BlockSpec((1, d), lambda i: (0, 0))],
        out_specs=pl.BlockSpec((tm, d), lambda i: (i, 0)),
        compiler_params=_params("parallel"),
        name="ln_in",
    )(x, g.reshape(1, d), b.reshape(1, d))


def _rope(x, cos_t, sin_lo, sin_hi):
    w = x.shape[1]
    reps = w // LANES
    c = jnp.tile(cos_t, (1, reps))
    lo = jnp.tile(sin_lo, (1, reps))
    hi = jnp.tile(sin_hi, (1, reps))
    half = DSA_DH // 8
    from_above = pltpu.roll(x, w - half, 1)
    from_below = pltpu.roll(x, half, 1)
    return x * c + from_above * lo + from_below * hi


def _inproj_kernel(x_ref, w_ref, cos_ref, lo_ref, hi_ref,
                   qkv_o, z_o, k_o, v_o, ki_o, qb_o, kb_o, vb_o, qib_o, kib_o, sm_o):
    xb = x_ref[...].astype(MXU_DTYPE)
    cos_t, lo, hi = cos_ref[...], lo_ref[...], hi_ref[...]

    def proj(c0, c1):
        return jnp.dot(xb, w_ref[:, c0:c1], preferred_element_type=f32)

    qkv_o[...] = proj(C_QKV, C_Z)
    z_o[...] = proj(C_Z, C_QD)
    q = _rope(proj(C_QD, C_KD), cos_t, lo, hi) * (DSA_DH ** -0.5 * LOG2E)
    qb_o[...] = q.astype(qb_o.dtype)
    k = _rope(proj(C_KD, C_VD), cos_t, lo, hi)
    k_o[...] = k
    kb_o[...] = k.astype(kb_o.dtype)
    v = proj(C_VD, C_QI)
    v_o[...] = v
    vb_o[...] = v.astype(vb_o.dtype)
    qi = _rope(proj(C_QI, C_KI), cos_t, lo, hi) * (IDX_DIM ** -0.5)
    qib_o[...] = qi.astype(qib_o.dtype)
    ki = _rope(proj(C_KI, C_SM), cos_t, lo, hi)
    ki_o[...] = ki[:, :IDX_DIM]
    kib_o[...] = ki.astype(kib_o.dtype)
    sm_o[...] = proj(C_SM, C_END)


def _inproj_call(h, w_packed, tabs, seq_len):
    n, d = h.shape
    tm = _row_tile(n, 512)
    cos_t, lo, hi = tabs
    if tm > seq_len:
        assert tm % seq_len == 0
        cos_t, lo, hi = (jnp.tile(a, (tm // seq_len, 1)) for a in (cos_t, lo, hi))
        nblk = 1
    else:
        assert seq_len % tm == 0
        nblk = seq_len // tm
    tab_spec = pl.BlockSpec((tm, LANES), lambda i: (i % nblk, 0))

    def rows(w):
        return pl.BlockSpec((tm, w), lambda i: (i, 0))

    widths = [(GDN_CONV_CH, f32), (GDN_VW, f32), (DSA_W, f32), (DSA_W, f32), (IDX_DIM, f32),
              (DSA_W, MXU_DTYPE), (DSA_W, MXU_DTYPE), (DSA_W, MXU_DTYPE), (IDX_W, MXU_DTYPE),
              (LANES, MXU_DTYPE), (LANES, f32)]
    return pl.pallas_call(
        _inproj_kernel,
        out_shape=[jax.ShapeDtypeStruct((n, w), dt) for w, dt in widths],
        grid=(n // tm,),
        in_specs=[rows(d),
                  pl.BlockSpec((d, C_END), lambda i: (0, 0), pipeline_mode=pl.Buffered(1)),
                  tab_spec, tab_spec, tab_spec],
        out_specs=[rows(w) for w, _ in widths],
        compiler_params=_params("parallel"),
        name="inproj",
    )(h, w_packed, cos_t, lo, hi)


def _unit_lower_inverse(a, size):
    r = lax.broadcasted_iota(jnp.int32, (size, size), 0)
    c = lax.broadcasted_iota(jnp.int32, (size, size), 1)

    def same_block(n):
        return (r // n) == (c // n)

    base = SUBLANES
    a0 = jnp.where(same_block(base), a, 0.0)
    p = _mm_x3(a0, a0)
    x = jnp.where(r == c, 1.0, 0.0) - a0
    x = x + _mm_x3(x, p)
    p = _mm_x3(p, p)
    x = x + _mm_x3(x, p)
    n = base
    while n < size:
        off = jnp.where(same_block(2 * n), jnp.where(same_block(n), 0.0, a), 0.0)
        x = x - _mm_x3(_mm_x3(x, off), x)
        n *= 2
    return x


def _chunk_cumsum(g, chunk):
    rows = g.shape[0]
    r = lax.broadcasted_iota(jnp.int32, (rows, rows), 0)
    c = lax.broadcasted_iota(jnp.int32, (rows, rows), 1)
    tri = jnp.where(jnp.logical_and(r // chunk == c // chunk, r >= c), 1.0, 0.0).astype(MXU_DTYPE)
    dot = functools.partial(jnp.dot, preferred_element_type=f32)
    g1 = g.astype(MXU_DTYPE)
    rem = g - g1.astype(f32)
    g2 = rem.astype(MXU_DTYPE)
    g3 = (rem - g2.astype(f32)).astype(MXU_DTYPE)
    return dot(tri, g1) + (dot(tri, g2) + dot(tri, g3))


def _gdn_kernel(qkv_ref, z_ref, sm_ref, cw_ref, cb_ref, s0_ref, par_ref, nw_ref,
                o_ref, s_ref, ubuf, *, chunk, group):
    step = pl.program_id(1)
    rows = chunk * group

    @pl.when(step == 0)
    def _():
        ubuf[0:SUBLANES, :] = cb_ref[0]
        s_ref[0] = s0_ref[0]

    ubuf[SUBLANES:SUBLANES + rows, :] = qkv_ref[0]
    y = ubuf[SUBLANES - 3:SUBLANES - 3 + rows, :] * cw_ref[0:1, :]
    for i in range(1, CONV_W):
        y = y + ubuf[SUBLANES - 3 + i:SUBLANES - 3 + i + rows, :] * cw_ref[i:i + 1, :]
    ubuf[0:SUBLANES, :] = ubuf[rows:rows + SUBLANES, :]
    y = y * _sigmoid(y)

    sm = sm_ref[0]
    beta_all = _sigmoid(sm)
    xg = sm + par_ref[1:2, :]
    softplus = jnp.maximum(xg, 0.0) + jnp.log1p(jnp.exp(-jnp.abs(xg)))
    g_all = -jnp.exp(par_ref[0:1, :]) * softplus
    gc_all = _chunk_cumsum(g_all, chunk)
    gc_rows = gc_all.T

    r = lax.broadcasted_iota(jnp.int32, (chunk, chunk), 0)
    c = lax.broadcasted_iota(jnp.int32, (chunk, chunk), 1)
    causal = r >= c
    strict = r > c
    z = z_ref[0]
    nw = nw_ref[...]

    for h in range(GDN_HEADS):
        lane = slice(h * GDN_DK, (h + 1) * GDN_DK)
        gl = SM_DECAY + h
        local = []
        for g in range(group):
            rs = slice(g * chunk, (g + 1) * chunk)
            q = y[rs, h * GDN_DK:(h + 1) * GDN_DK]
            k = y[rs, GDN_QK + h * GDN_DK:GDN_QK + (h + 1) * GDN_DK]
            v = y[rs, 2 * GDN_QK + h * GDN_DV:2 * GDN_QK + (h + 1) * GDN_DV]
            q = q * lax.rsqrt(jnp.sum(q * q, -1, keepdims=True) + L2_EPS) * (GDN_DK ** -0.5)
            k = k * lax.rsqrt(jnp.sum(k * k, -1, keepdims=True) + L2_EPS)
            beta = beta_all[rs, SM_BETA + h:SM_BETA + h + 1]
            gc = gc_all[rs, gl:gl + 1]
            gc_row = gc_rows[gl:gl + 1, rs]
            g_last = gc_all[(g + 1) * chunk - 1:(g + 1) * chunk, gl:gl + 1]
            decay = jnp.exp(jnp.where(causal, gc - gc_row, -jnp.inf))
            a_mat = jnp.where(strict, beta * _mm_nt(k, k) * decay, 0.0)
            t_inv = _unit_lower_inverse(a_mat, chunk)
            rhs = jnp.concatenate([v * beta, k * (beta * jnp.exp(gc))], axis=-1)
            sol = _mm_x3(t_inv, rhs)
            local.append(dict(u0=sol[:, :GDN_DV], wk=sol[:, GDN_DV:], qk=_mm_nt(q, k) * decay,
                              q_dec=q * jnp.exp(gc), k_dec=k * jnp.exp(g_last - gc), g_last=g_last))
        s = s_ref[0, h]
        for g in range(group):
            rs = slice(g * chunk, (g + 1) * chunk)
            lc = local[g]
            u = lc["u0"] - _mm(lc["wk"], s)
            o = _mm(lc["q_dec"], s) + _mm(lc["qk"], u)
            s = s * jnp.exp(lc["g_last"]) + _mm_tn(lc["k_dec"], u)
            zh = z[rs, lane]
            o = o * lax.rsqrt(jnp.mean(o * o, -1, keepdims=True) + RMS_EPS) * nw * (zh * _sigmoid(zh))
            o_ref[0, rs, lane] = o.astype(o_ref.dtype)
        s_ref[0, h] = s


def _gdn_call(qkv, z, sm, conv_w, conv_buf8, s0, par, norm_w, chunk):
    b, t, _ = qkv.shape
    group = min(GDN_CHUNKS_PER_STEP, t // chunk)
    rows = chunk * group
    assert t % rows == 0
    return pl.pallas_call(
        functools.partial(_gdn_kernel, chunk=chunk, group=group),
        out_shape=[jax.ShapeDtypeStruct((b, t, GDN_VW), MXU_DTYPE),
                   jax.ShapeDtypeStruct((b, GDN_HEADS, GDN_DK, GDN_DV), f32)],
        grid=(b, t // rows),
        in_specs=[pl.BlockSpec((1, rows, GDN_CONV_CH), lambda i, j: (i, j, 0)),
                  pl.BlockSpec((1, rows, GDN_VW), lambda i, j: (i, j, 0)),
                  pl.BlockSpec((1, rows, LANES), lambda i, j: (i, j, 0)),
                  pl.BlockSpec((CONV_W, GDN_CONV_CH), lambda i, j: (0, 0)),
                  pl.BlockSpec((1, SUBLANES, GDN_CONV_CH), lambda i, j: (i, 0, 0)),
                  pl.BlockSpec((1, GDN_HEADS, GDN_DK, GDN_DV), lambda i, j: (i, 0, 0, 0)),
                  pl.BlockSpec((2, LANES), lambda i, j: (0, 0)),
                  pl.BlockSpec((1, GDN_DV), lambda i, j: (0, 0))],
        out_specs=[pl.BlockSpec((1, rows, GDN_VW), lambda i, j: (i, j, 0)),
                   pl.BlockSpec((1, GDN_HEADS, GDN_DK, GDN_DV), lambda i, j: (i, 0, 0, 0))],
        scratch_shapes=[pltpu.VMEM((SUBLANES + rows, GDN_CONV_CH), f32)],
        compiler_params=_params("parallel", "arbitrary"),
        name="gdn",
    )(qkv, z, sm, conv_w, conv_buf8, s0, par, norm_w)


def _dsa_kernel(q_ref, qi_ref, sm_ref, k_ref, v_ref, ki_ref, o_ref,
                ikey, m_s, l_s, acc_s, *, qb, tk, pos0, n_keys, topk):
    blk = pl.program_id(1)
    row = lax.broadcasted_iota(jnp.int32, (qb, 1), 0)
    q_pos = pos0 + blk * qb + row
    limit = jnp.minimum((q_pos // CHUNK + 1) * CHUNK, n_keys)
    last_limit = jnp.minimum(((pos0 + (blk + 1) * qb - 1) // CHUNK + 1) * CHUNK, n_keys)
    n_tiles = (last_limit + tk - 1) // tk
    lane = lax.broadcasted_iota(jnp.int32, (qb, LANES), 1)
    lower_half = lane < DSA_DH
    col = lax.broadcasted_iota(jnp.int32, (qb, tk), 1)
    groups = tk // LANES

    sm = sm_ref[0]
    qi_heads = []
    for h in range(IDX_HEADS):
        pair = qi_ref[0, :, (h // 2) * LANES:(h // 2 + 1) * LANES]
        keep = lower_half if h % 2 == 0 else jnp.logical_not(lower_half)
        qi_heads.append(jnp.where(keep, pair, jnp.zeros_like(pair)))
    wi = [sm[:, SM_WI + h:SM_WI + h + 1] * (IDX_HEADS ** -0.5) for h in range(IDX_HEADS)]

    def score_tile(kt, carry):
        ki_t = ki_ref[0, pl.ds(pl.multiple_of(kt * tk, tk), tk), :]
        score = jnp.zeros((qb, tk), f32)
        for h in range(IDX_HEADS):
            score = score + jnp.maximum(_mm_nt(qi_heads[h], ki_t), 0.0) * wi[h]
        bits = pltpu.bitcast(score, jnp.int32)
        key = bits ^ ((bits >> 31) & 0x7FFFFFFF)
        ikey[kt] = jnp.where(kt * tk + col < limit, key, INT_MIN)
        return carry

    lax.fori_loop(0, n_tiles, score_tile, 0)

    def count(pred):
        def body(kt, acc):
            hit = jnp.where(pred(ikey[kt]), 1, 0)
            part = hit[:, 0:LANES]
            for g in range(1, groups):
                part = part + hit[:, g * LANES:(g + 1) * LANES]
            return acc + part
        acc = lax.fori_loop(0, n_tiles, body, jnp.zeros((qb, LANES), jnp.int32))
        return jnp.sum(acc, axis=1, keepdims=True)

    def bit_step(i, u):
        cand = u | (jnp.int32(1) << (31 - i))
        cand_s = cand ^ INT_MIN
        return jnp.where(count(lambda kk: kk >= cand_s) >= topk, cand, u)

    u = lax.fori_loop(0, 32, bit_step, jnp.zeros((qb, 1), jnp.int32))
    thr = jnp.maximum(u ^ INT_MIN, INT_MIN + 1)
    n_gt = count(lambda kk: kk > thr)
    n_ge = count(lambda kk: kk >= thr)
    need = (topk - n_gt).astype(f32)
    surplus = jnp.max(jnp.where(n_ge > topk, 1, 0))

    q_heads = []
    for h in range(DSA_HEADS):
        pair = q_ref[0, :, (h // 2) * LANES:(h // 2 + 1) * LANES]
        keep = lower_half if h % 2 == 0 else jnp.logical_not(lower_half)
        q_heads.append(jnp.where(keep, pair, jnp.zeros_like(pair)))

    def attend(exact_ties):
        m_s[...] = jnp.full(m_s.shape, M_FLOOR, f32)
        l_s[...] = jnp.zeros(l_s.shape, f32)
        acc_s[...] = jnp.zeros(acc_s.shape, f32)
        if exact_ties:
            tr = lax.broadcasted_iota(jnp.int32, (tk, tk), 0)
            tc = lax.broadcasted_iota(jnp.int32, (tk, tk), 1)
            prefix = jnp.where(tr <= tc, 1.0, 0.0).astype(MXU_DTYPE)

        def body(kt, taken):
            key = ikey[kt]
            if exact_ties:
                eq = key == thr
                rank = taken + _mm(jnp.where(eq, 1.0, 0.0), prefix)
                sel = jnp.logical_or(key > thr, jnp.logical_and(eq, rank <= need))
                taken = rank[:, tk - 1:tk]
            else:
                sel = key >= thr
            bias = jnp.where(sel, 0.0, NEG_BIG)
            start = pl.multiple_of(kt * tk, tk)
            for h in range(DSA_HEADS):
                pr = slice((h // 2) * LANES, (h // 2 + 1) * LANES)
                logits = _mm_nt(q_heads[h], k_ref[0, pl.ds(start, tk), pr]) + bias
                m_old = m_s[h]
                m_new = jnp.maximum(m_old, jnp.max(logits, axis=1, keepdims=True))
                alpha = jnp.exp2(m_old - m_new)
                p = jnp.exp2(logits - m_new)
                l_s[h] = alpha * l_s[h] + jnp.sum(p, axis=1, keepdims=True)
                acc_s[h] = alpha * acc_s[h] + _mm(p, v_ref[0, pl.ds(start, tk), pr])
                m_s[h] = m_new
            return taken

        lax.fori_loop(0, n_tiles, body, jnp.zeros((qb, 1), f32))
        for pi in range(DSA_HEADS // 2):
            even = acc_s[2 * pi] / l_s[2 * pi]
            odd = acc_s[2 * pi + 1] / l_s[2 * pi + 1]
            o_ref[0, :, pi * LANES:(pi + 1) * LANES] = jnp.where(lower_half, even, odd).astype(o_ref.dtype)

    @pl.when(surplus == 0)
    def _():
        attend(False)

    @pl.when(surplus != 0)
    def _():
        attend(True)


def _dsa_call(qb_arr, qib_arr, sm, k_all, v_all, ki_all, pos0, n_keys, qb, tk):
    b, t, _ = qb_arr.shape
    lp = k_all.shape[1]
    assert lp % tk == 0 and t % qb == 0
    topk = min(TOPK_MAX, n_keys // 4)
    kern = functools.partial(_dsa_kernel, qb=qb, tk=tk, pos0=pos0, n_keys=n_keys, topk=topk)
    resident = dict(pipeline_mode=pl.Buffered(1))
    return pl.pallas_call(
        kern,
        out_shape=jax.ShapeDtypeStruct((b, t, DSA_W), MXU_DTYPE),
        grid=(b, t // qb),
        in_specs=[pl.BlockSpec((1, qb, DSA_W), lambda i, j: (i, j, 0)),
                  pl.BlockSpec((1, qb, IDX_W), lambda i, j: (i, j, 0)),
                  pl.BlockSpec((1, qb, LANES), lambda i, j: (i, j, 0)),
                  pl.BlockSpec((1, lp, DSA_W), lambda i, j: (i, 0, 0), **resident),
                  pl.BlockSpec((1, lp, DSA_W), lambda i, j: (i, 0, 0), **resident),
                  pl.BlockSpec((1, lp, LANES), lambda i, j: (i, 0, 0), **resident)],
        out_specs=pl.BlockSpec((1, qb, DSA_W), lambda i, j: (i, j, 0)),
        scratch_shapes=[pltpu.VMEM((lp // tk, qb, tk), jnp.int32),
                        pltpu.VMEM((DSA_HEADS, qb, 1), f32),
                        pltpu.VMEM((DSA_HEADS, qb, 1), f32),
                        pltpu.VMEM((DSA_HEADS, qb, LANES), f32)],
        compiler_params=_params("parallel", "arbitrary"),
        name="dsa",
    )(qb_arr, qib_arr, sm, k_all, v_all, ki_all)


def _outproj_kernel(h_ref, og_ref, od_ref, w_ref, g_ref, b_ref, o_ref, *, alpha):
    mix = (jnp.dot(og_ref[...], w_ref[0:GDN_VW, :], preferred_element_type=f32)
           + jnp.dot(od_ref[...], w_ref[GDN_VW:, :], preferred_element_type=f32))
    o_ref[...] = _layer_norm(alpha * h_ref[...] + mix, g_ref[...], b_ref[...])


def _outproj_call(h, og, od, w, g, b, alpha):
    n, d = h.shape
    tm = _row_tile(n, 512)
    return pl.pallas_call(
        functools.partial(_outproj_kernel, alpha=alpha),
        out_shape=jax.ShapeDtypeStruct((n, d), f32),
        grid=(n // tm,),
        in_specs=[pl.BlockSpec((tm, d), lambda i: (i, 0)),
                  pl.BlockSpec((tm, GDN_VW), lambda i: (i, 0)),
                  pl.BlockSpec((tm, DSA_W), lambda i: (i, 0)),
                  pl.BlockSpec((GDN_VW + DSA_W, d), lambda i: (0, 0)),
                  pl.BlockSpec((1, d), lambda i: (0, 0)),
                  pl.BlockSpec((1, d), lambda i: (0, 0))],
        out_specs=pl.BlockSpec((tm, d), lambda i: (i, 0)),
        compiler_params=_params("parallel"),
        name="outproj_ln",
    )(h, og, od, w, g.reshape(1, d), b.reshape(1, d))


def _ffn_kernel(x_ref, wg_ref, wu_ref, wd_ref, g_ref, b_ref, o_ref, acc, *, alpha):
    j = pl.program_id(1)

    @pl.when(j == 0)
    def _():
        acc[...] = jnp.zeros(acc.shape, f32)

    xb = x_ref[...].astype(MXU_DTYPE)
    gate = jnp.dot(xb, wg_ref[...], preferred_element_type=f32)
    up = jnp.dot(xb, wu_ref[...], preferred_element_type=f32)
    act = gate * _sigmoid(gate) * up
    acc[...] += jnp.dot(act.astype(MXU_DTYPE), wd_ref[...], preferred_element_type=f32)

    @pl.when(j == pl.num_programs(1) - 1)
    def _():
        o_ref[...] = _layer_norm(alpha * x_ref[...] + acc[...], g_ref[...], b_ref[...])


def _ffn_call(x, wg, wu, wd, g, b, alpha):
    n, d = x.shape
    f = wg.shape[1]
    tm = _row_tile(n, 1024)
    tf = _row_tile(f, 512)
    return pl.pallas_call(
        functools.partial(_ffn_kernel, alpha=alpha),
        out_shape=jax.ShapeDtypeStruct((n, d), f32),
        grid=(n // tm, f // tf),
        in_specs=[pl.BlockSpec((tm, d), lambda i, j: (i, 0)),
                  pl.BlockSpec((d, tf), lambda i, j: (0, j)),
                  pl.BlockSpec((d, tf), lambda i, j: (0, j)),
                  pl.BlockSpec((tf, d), lambda i, j: (j, 0)),
                  pl.BlockSpec((1, d), lambda i, j: (0, 0)),
                  pl.BlockSpec((1, d), lambda i, j: (0, 0))],
        out_specs=pl.BlockSpec((tm, d), lambda i, j: (i, 0)),
        scratch_shapes=[pltpu.VMEM((tm, d), f32)],
        compiler_params=_params("parallel", "arbitrary"),
        name="ffn_ln",
    )(x, wg, wu, wd, g.reshape(1, d), b.reshape(1, d))


def _moe_kernel(x_ref, wr_ref, wg_ref, wu_ref, wd_ref, g_ref, b_ref, o_ref,
                acc, gates, gcol, *, alpha, n_experts):
    e = pl.program_id(1)
    j = pl.program_id(2)
    tm = x_ref.shape[0]
    lane = lax.broadcasted_iota(jnp.int32, (tm, LANES), 1)

    @pl.when(jnp.logical_and(e == 0, j == 0))
    def _():
        acc[...] = jnp.zeros(acc.shape, f32)
        logits = _mm_f32(x_ref[...], wr_ref[...])
        logits = jnp.where(lane < n_experts, logits, -jnp.inf)
        m1 = jnp.max(logits, axis=1, keepdims=True)
        i1 = jnp.min(jnp.where(logits == m1, lane, LANES), axis=1, keepdims=True)
        rest = jnp.where(lane == i1, -jnp.inf, logits)
        m2 = jnp.max(rest, axis=1, keepdims=True)
        i2 = jnp.min(jnp.where(rest == m2, lane, LANES), axis=1, keepdims=True)
        e2 = jnp.exp(m2 - m1)
        w1 = 1.0 / (1.0 + e2)
        gates[...] = jnp.where(lane == i1, w1, 0.0) + jnp.where(lane == i2, e2 * w1, 0.0)

    @pl.when(j == 0)
    def _():
        gcol[...] = jnp.sum(jnp.where(lane == e, gates[...], 0.0), axis=1, keepdims=True)

    xb = x_ref[...].astype(MXU_DTYPE)
    gate = jnp.dot(xb, wg_ref[0], preferred_element_type=f32)
    up = jnp.dot(xb, wu_ref[0], preferred_element_type=f32)
    act = gate * _sigmoid(gate) * up * gcol[...]
    acc[...] += jnp.dot(act.astype(MXU_DTYPE), wd_ref[0], preferred_element_type=f32)

    @pl.when(jnp.logical_and(e == n_experts - 1, j == pl.num_programs(2) - 1))
    def _():
        o_ref[...] = _layer_norm(alpha * x_ref[...] + acc[...], g_ref[...], b_ref[...])


def _moe_call(x, wr, wg, wu, wd, g, b, alpha):
    n, d = x.shape
    n_experts, _, f = wg.shape
    tm = _row_tile(n, 1024)
    tf = _row_tile(f, 512)
    return pl.pallas_call(
        functools.partial(_moe_kernel, alpha=alpha, n_experts=n_experts),
        out_shape=jax.ShapeDtypeStruct((n, d), f32),
        grid=(n // tm, n_experts, f // tf),
        in_specs=[pl.BlockSpec((tm, d), lambda i, e, j: (i, 0)),
                  pl.BlockSpec((d, LANES), lambda i, e, j: (0, 0)),
                  pl.BlockSpec((1, d, tf), lambda i, e, j: (e, 0, j)),
                  pl.BlockSpec((1, d, tf), lambda i, e, j: (e, 0, j)),
                  pl.BlockSpec((1, tf, d), lambda i, e, j: (e, j, 0)),
                  pl.BlockSpec((1, d), lambda i, e, j: (0, 0)),
                  pl.BlockSpec((1, d), lambda i, e, j: (0, 0))],
        out_specs=pl.BlockSpec((tm, d), lambda i, e, j: (i, 0)),
        scratch_shapes=[pltpu.VMEM((tm, d), f32),
                        pltpu.VMEM((tm, LANES), f32),
                        pltpu.VMEM((tm, 1), f32)],
        compiler_params=_params("parallel", "arbitrary", "arbitrary"),
        name="moe_ln",
    )(x, wr, wg, wu, wd, g.reshape(1, d), b.reshape(1, d))


def _pack_w_in(w_in):
    o_z = GDN_CONV_CH
    o_b = o_z + GDN_VW
    o_a = o_b + GDN_HEADS
    o_qd = o_a + GDN_HEADS
    o_kd = o_qd + DSA_W
    o_vd = o_kd + DSA_W
    o_qi = o_vd + DSA_W
    o_ki = o_qi + IDX_W
    o_wi = o_ki + IDX_DIM
    assert w_in.shape[-1] == o_wi + IDX_HEADS
    sl = lambda a, n: w_in[..., a:a + n]
    pad = jnp.zeros(w_in.shape[:-1] + (LANES - 2 * GDN_HEADS - IDX_HEADS,), w_in.dtype)
    packed = jnp.concatenate([
        sl(0, GDN_CONV_CH), sl(o_z, GDN_VW), sl(o_qd, DSA_W), sl(o_kd, DSA_W), sl(o_vd, DSA_W),
        sl(o_qi, IDX_W), sl(o_ki, IDX_DIM), sl(o_ki, IDX_DIM),
        sl(o_b, GDN_HEADS), sl(o_a, GDN_HEADS), sl(o_wi, IDX_HEADS), pad], axis=-1)
    assert packed.shape[-1] == C_END
    return packed.astype(MXU_DTYPE)


def _rope_tables(pos):
    half = DSA_DH // 8
    inv_freq = ROPE_THETA ** (-jnp.arange(half, dtype=f32) / half)
    ang = pos.astype(f32)[:, None] * inv_freq[None, :]
    cos, sin = jnp.cos(ang), jnp.sin(ang)
    t = pos.shape[0]
    ones = jnp.ones((t, DSA_DH - 2 * half), f32)
    zeros = jnp.zeros((t, DSA_DH - 2 * half), f32)
    z8 = jnp.zeros((t, half), f32)
    cos_t = jnp.concatenate([cos, cos, ones], -1)
    lo = jnp.concatenate([-sin, z8, zeros], -1)
    hi = jnp.concatenate([z8, sin, zeros], -1)
    return tuple(jnp.tile(a, (1, LANES // DSA_DH)) for a in (cos_t, lo, hi))


def _gdn_params(a_log, dt_bias):
    row = lambda v: jnp.zeros((LANES,), f32).at[SM_DECAY:SM_DECAY + GDN_HEADS].set(v.astype(f32))
    return jnp.stack([row(a_log), row(dt_bias)])


def _trunk(x, pos0, conv_bufs, s0s, past_k, past_v, past_ik, chunk, wts, depth):
    b, t, d = x.shape
    n = b * t
    alpha = (2.0 * depth) ** 0.25
    tabs = _rope_tables(pos0 + jnp.arange(t))
    past = 0 if past_k is None else past_k.shape[2]
    n_keys = past + t
    assert t >= CONV_W - 1
    qb = min(t, DSA_Q_BLOCK)
    tk = DSA_KEY_TILE
    lp = -(-n_keys // tk) * tk

    def with_past(new, old, l):
        parts = [] if old is None else [old[l].reshape(b, past, -1).astype(MXU_DTYPE)]
        parts.append(new.reshape(b, t, -1))
        if lp > n_keys:
            parts.append(jnp.zeros((b, lp - n_keys, new.shape[-1]), MXU_DTYPE))
        return parts[0] if len(parts) == 1 else jnp.concatenate(parts, axis=1)

    h = _ln_call(x.reshape(n, d), wts["ln_in_g"], wts["ln_in_b"])
    new_k, new_v, new_ik, new_conv, new_s = [], [], [], [], []
    for l in range(depth):
        (qkv, z, k_f, v_f, ki_f, q_b, k_b, v_b, qi_b, ki_b, sm) = _inproj_call(h, wts["w_in"][l], tabs, t)
        qkv3 = qkv.reshape(b, t, GDN_CONV_CH)
        buf8 = jnp.concatenate([jnp.zeros((b, SUBLANES - (CONV_W - 1), GDN_CONV_CH), f32), conv_bufs[l]], axis=1)
        o_g, s_new = _gdn_call(qkv3, z.reshape(b, t, GDN_VW), sm.reshape(b, t, LANES), wts["conv_w"][l],
                               buf8, s0s[l], _gdn_params(wts["gdn_a_log"][l], wts["gdn_dt_bias"][l]),
                               wts["gdn_norm_w"][l].reshape(1, GDN_DV), chunk)
        ki_dup = None if past_ik is None else jnp.concatenate([past_ik, past_ik], axis=-1)
        o_d = _dsa_call(q_b.reshape(b, t, DSA_W), qi_b.reshape(b, t, IDX_W), sm.reshape(b, t, LANES),
                        with_past(k_b, past_k, l), with_past(v_b, past_v, l), with_past(ki_b, ki_dup, l),
                        past, n_keys, qb, tk)
        h = _outproj_call(h, o_g.reshape(n, GDN_VW), o_d.reshape(n, DSA_W), wts["w_out"][l],
                          wts["ln1_g"][l], wts["ln1_b"][l], alpha)
        if l % 2 == 0:
            i = l // 2
            h = _ffn_call(h, wts["ffn_wg"][i], wts["ffn_wu"][i], wts["ffn_wd"][i],
                          wts["ln2_g"][l], wts["ln2_b"][l], alpha)
        else:
            i = l // 2
            h = _moe_call(h, wts["moe_router"][i], wts["moe_wg"][i], wts["moe_wu"][i], wts["moe_wd"][i],
                          wts["ln2_g"][l], wts["ln2_b"][l], alpha)
        new_k.append(k_f.reshape(b, t, DSA_HEADS, DSA_DH))
        new_v.append(v_f.reshape(b, t, DSA_HEADS, DSA_DH))
        new_ik.append(ki_f.reshape(b, t, IDX_DIM))
        new_conv.append(qkv3[:, t - (CONV_W - 1):])
        new_s.append(s_new)
    return (h.reshape(b, t, d), jnp.stack(new_k), jnp.stack(new_v), jnp.stack(new_ik),
            jnp.stack(new_s), jnp.stack(new_conv))


def kernel(x_prompt, x_sample, cache_k, cache_v, cache_idx_k, state_gdn, state_conv, ln_in_g, ln_in_b, w_in, conv_w, gdn_a_log, gdn_dt_bias, gdn_norm_w, w_out, ln1_g, ln1_b, ln2_g, ln2_b, ffn_wg, ffn_wu, ffn_wd, moe_router, moe_wg, moe_wu, moe_wd):
    depth = w_in.shape[0]
    n_experts = moe_router.shape[-1]
    bp, tp, d = x_prompt.shape
    cast = lambda a: a.astype(MXU_DTYPE)
    router = jnp.concatenate(
        [moe_router.astype(f32), jnp.zeros(moe_router.shape[:-1] + (LANES - n_experts,), f32)], axis=-1)
    wts = dict(ln_in_g=ln_in_g, ln_in_b=ln_in_b, w_in=_pack_w_in(w_in), conv_w=conv_w,
               gdn_a_log=gdn_a_log, gdn_dt_bias=gdn_dt_bias, gdn_norm_w=gdn_norm_w, w_out=cast(w_out),
               ln1_g=ln1_g, ln1_b=ln1_b, ln2_g=ln2_g, ln2_b=ln2_b,
               ffn_wg=cast(ffn_wg), ffn_wu=cast(ffn_wu), ffn_wd=cast(ffn_wd),
               moe_router=router, moe_wg=cast(moe_wg), moe_wu=cast(moe_wu), moe_wd=cast(moe_wd))
    zero_conv = jnp.zeros((depth, bp, CONV_W - 1, GDN_CONV_CH), f32)
    zero_s = jnp.zeros((depth, bp, GDN_HEADS, GDN_DK, GDN_DV), f32)
    y_p, k_p, v_p, ik_p, s_p, conv_p = _trunk(
        x_prompt, 0, zero_conv, zero_s, None, None, None, CHUNK, wts, depth)
    ts = x_sample.shape[1]
    past = cache_k.shape[2]
    y_s, k_s, v_s, ik_s, s_s, conv_s = _trunk(
        x_sample, past, state_conv, state_gdn, cache_k, cache_v, cache_idx_k, ts, wts, depth)
    return (y_p, y_s, k_p, v_p, ik_p, s_p, conv_p, k_s, v_s, ik_s, s_s, conv_s)
```

```python
import functools

import jax
import jax.numpy as jnp
from jax import lax
from jax.experimental import pallas as pl
from jax.experimental.pallas import tpu as pltpu

CHUNK = 64
CONV_W = 4
GDN_HEADS = 4
GDN_DK = 128
GDN_DV = 128
DSA_HEADS = 8
DSA_DH = 64
IDX_HEADS = 4
IDX_DIM = 64
TOPK_MAX = 256
ROPE_THETA = 500000.0
TOP_K_EXPERTS = 2
LN_EPS = 1e-5
RMS_EPS = 1e-6
L2_EPS = 1e-6

GDN_QK = GDN_HEADS * GDN_DK
GDN_VW = GDN_HEADS * GDN_DV
GDN_CONV_CH = 2 * GDN_QK + GDN_VW
DSA_W = DSA_HEADS * DSA_DH
IDX_W = IDX_HEADS * IDX_DIM

LANES = 128
SUBLANES = 8
VMEM_LIMIT = 56 * 1024 * 1024

DSA_Q_BLOCK = 256
DSA_KEY_TILE = 2048
GDN_CHUNKS_PER_STEP = 4

MXU_DTYPE = jnp.bfloat16

C_QKV = 0
C_Z = C_QKV + GDN_CONV_CH
C_QD = C_Z + GDN_VW
C_KD = C_QD + DSA_W
C_VD = C_KD + DSA_W
C_QI = C_VD + DSA_W
C_KI = C_QI + IDX_W
C_SM = C_KI + LANES
C_END = C_SM + LANES
SM_BETA = 0
SM_DECAY = GDN_HEADS
SM_WI = 2 * GDN_HEADS

INT_MIN = -2 ** 31
HALF_BIAS = 2 ** 15
TIE_CHUNK = 512
NEG_BIG = -3e38
M_FLOOR = -1e38
LOG2E = 1.4426950408889634

f32 = jnp.float32


def _mm(a, b):
    return jnp.dot(a.astype(MXU_DTYPE), b.astype(MXU_DTYPE), preferred_element_type=f32)


def _mm_nt(a, b):
    return lax.dot_general(a.astype(MXU_DTYPE), b.astype(MXU_DTYPE), (((1,), (1,)), ((), ())),
                           preferred_element_type=f32)


def _mm_tn(a, b):
    return lax.dot_general(a.astype(MXU_DTYPE), b.astype(MXU_DTYPE), (((0,), (0,)), ((), ())),
                           preferred_element_type=f32)


def _mm_f32(a, b):
    return jnp.dot(a, b, precision=lax.Precision.HIGHEST, preferred_element_type=f32)


def _split(x):
    hi = x.astype(MXU_DTYPE)
    lo = (x - hi.astype(f32)).astype(MXU_DTYPE)
    return hi, lo


def _mm_x3(a, b):
    a_hi, a_lo = _split(a)
    b_hi, b_lo = _split(b)
    dot = functools.partial(jnp.dot, preferred_element_type=f32)
    return dot(a_hi, b_hi) + (dot(a_hi, b_lo) + dot(a_lo, b_hi))


def _sigmoid(x):
    return 1.0 / (1.0 + jnp.exp(-x))


def _layer_norm(r, g, b):
    mu = jnp.mean(r, axis=-1, keepdims=True)
    xc = r - mu
    var = jnp.mean(xc * xc, axis=-1, keepdims=True)
    return xc * lax.rsqrt(var + LN_EPS) * g + b


def _params(*sem):
    return pltpu.CompilerParams(dimension_semantics=sem, vmem_limit_bytes=VMEM_LIMIT)


def _row_tile(n, want):
    t = min(n, want)
    assert n % t == 0, (n, t)
    return t


def _ln_kernel(x_ref, g_ref, b_ref, o_ref):
    o_ref[...] = _layer_norm(x_ref[...], g_ref[...], b_ref[...])


def _ln_call(x, g, b):
    n, d = x.shape
    tm = _row_tile(n, 1024)
    return pl.pallas_call(
        _ln_kernel,
        out_shape=jax.ShapeDtypeStruct((n, d), f32),
        grid=(n // tm,),
        in_specs=[pl.BlockSpec((tm, d), lambda i: (i, 0)),
                  pl.BlockSpec((1, d), lambda i: (0, 0)),
                  pl.BlockSpec((1, d), lambda i: (0, 0))],
        out_specs=pl.BlockSpec((tm, d), lambda i: (i, 0)),
        compiler_params=_params("parallel"),
        name="ln_in",
    )(x, g.reshape(1, d), b.reshape(1, d))


def _rope(x, cos_t, sin_lo, sin_hi):
    w = x.shape[1]
    reps = w // LANES
    c = jnp.tile(cos_t, (1, reps))
    lo = jnp.tile(sin_lo, (1, reps))
    hi = jnp.tile(sin_hi, (1, reps))
    half = DSA_DH // 8
    from_above = pltpu.roll(x, w - half, 1)
    from_below = pltpu.roll(x, half, 1)
    return x * c + from_above * lo + from_below * hi


def _inproj_kernel(x_ref, w_ref, cos_ref, lo_ref, hi_ref,
                   qkv_o, z_o, k_o, v_o, ki_o, qb_o, kb_o, vb_o, qib_o, kib_o, sm_o):
    xb = x_ref[...].astype(MXU_DTYPE)
    cos_t, lo, hi = cos_ref[...], lo_ref[...], hi_ref[...]

    def proj(c0, c1):
        return jnp.dot(xb, w_ref[:, c0:c1], preferred_element_type=f32)

    qkv_o[...] = proj(C_QKV, C_Z)
    z_o[...] = proj(C_Z, C_QD)
    q = _rope(proj(C_QD, C_KD), cos_t, lo, hi) * (DSA_DH ** -0.5 * LOG2E)
    qb_o[...] = q.astype(qb_o.dtype)
    k = _rope(proj(C_KD, C_VD), cos_t, lo, hi)
    k_o[...] = k
    kb_o[...] = k.astype(kb_o.dtype)
    v = proj(C_VD, C_QI)
    v_o[...] = v
    vb_o[...] = v.astype(vb_o.dtype)
    qi = _rope(proj(C_QI, C_KI), cos_t, lo, hi) * (IDX_DIM ** -0.5)
    qib_o[...] = qi.astype(qib_o.dtype)
    ki = _rope(proj(C_KI, C_SM), cos_t, lo, hi)
    ki_o[...] = ki[:, :IDX_DIM]
    kib_o[...] = ki.astype(kib_o.dtype)
    sm_o[...] = proj(C_SM, C_END)


def _inproj_call(h, w_packed, tabs, seq_len):
    n, d = h.shape
    tm = _row_tile(n, 512)
    cos_t, lo, hi = tabs
    if tm > seq_len:
        assert tm % seq_len == 0
        cos_t, lo, hi = (jnp.tile(a, (tm // seq_len, 1)) for a in (cos_t, lo, hi))
        nblk = 1
    else:
        assert seq_len % tm == 0
        nblk = seq_len // tm
    tab_spec = pl.BlockSpec((tm, LANES), lambda i: (i % nblk, 0))

    def rows(w):
        return pl.BlockSpec((tm, w), lambda i: (i, 0))

    widths = [(GDN_CONV_CH, f32), (GDN_VW, f32), (DSA_W, f32), (DSA_W, f32), (IDX_DIM, f32),
              (DSA_W, MXU_DTYPE), (DSA_W, MXU_DTYPE), (DSA_W, MXU_DTYPE), (IDX_W, MXU_DTYPE),
              (LANES, MXU_DTYPE), (LANES, f32)]
    return pl.pallas_call(
        _inproj_kernel,
        out_shape=[jax.ShapeDtypeStruct((n, w), dt) for w, dt in widths],
        grid=(n // tm,),
        in_specs=[rows(d),
                  pl.BlockSpec((d, C_END), lambda i: (0, 0), pipeline_mode=pl.Buffered(1)),
                  tab_spec, tab_spec, tab_spec],
        out_specs=[rows(w) for w, _ in widths],
        compiler_params=_params("parallel"),
        name="inproj",
    )(h, w_packed, cos_t, lo, hi)


def _wide_to_blocks(w, chunk):
    size = w.shape[1]
    r = lax.broadcasted_iota(jnp.int32, (size, size), 0)
    c = lax.broadcasted_iota(jnp.int32, (size, size), 1)
    tiled = jnp.concatenate([w] * (size // chunk), axis=0)
    return jnp.where(r // chunk == c // chunk, tiled, jnp.zeros_like(tiled))


def _blocks_to_wide(sq, chunk):
    size = sq.shape[1]
    blk = lax.broadcasted_iota(jnp.int32, (chunk, size), 1) // chunk
    out = jnp.zeros((chunk, size), sq.dtype)
    for h in range(size // chunk):
        out = jnp.where(blk == h, sq[h * chunk:(h + 1) * chunk, :], out)
    return out


def _wide_matmul(x, p, chunk):
    x_hi, x_lo = _split(x)
    p_hi, p_lo = _split(p)
    dot = functools.partial(jnp.dot, preferred_element_type=f32)
    both = dot(jnp.concatenate([x_hi, x_lo], axis=0), _wide_to_blocks(p_hi, chunk))
    return both[:chunk] + both[chunk:] + dot(x_hi, _wide_to_blocks(p_lo, chunk))


def _unit_lower_inverse_wide(a, chunk):
    size = a.shape[1]
    r = lax.broadcasted_iota(jnp.int32, (chunk, size), 0)
    c = lax.broadcasted_iota(jnp.int32, (chunk, size), 1) % chunk

    def same_block(n):
        return (r // n) == (c // n)

    base = SUBLANES
    a0 = jnp.where(same_block(base), a, 0.0)
    p = _wide_matmul(a0, a0, chunk)
    x = jnp.where(r == c, 1.0, 0.0) - a0
    x = x + _wide_matmul(x, p, chunk)
    p = _wide_matmul(p, p, chunk)
    x = x + _wide_matmul(x, p, chunk)
    n = base
    while n < chunk:
        off = jnp.where(same_block(2 * n), jnp.where(same_block(n), 0.0, a), 0.0)
        x = x - _wide_matmul(_wide_matmul(x, off, chunk), x, chunk)
        n *= 2
    return x


def _chunk_cumsum(g, chunk):
    rows = g.shape[0]
    r = lax.broadcasted_iota(jnp.int32, (rows, rows), 0)
    c = lax.broadcasted_iota(jnp.int32, (rows, rows), 1)
    tri = jnp.where(jnp.logical_and(r // chunk == c // chunk, r >= c), 1.0, 0.0).astype(MXU_DTYPE)
    dot = functools.partial(jnp.dot, preferred_element_type=f32)
    g1 = g.astype(MXU_DTYPE)
    rem = g - g1.astype(f32)
    g2 = rem.astype(MXU_DTYPE)
    g3 = (rem - g2.astype(f32)).astype(MXU_DTYPE)
    return dot(tri, g1) + (dot(tri, g2) + dot(tri, g3))


def _gdn_kernel(qkv_ref, z_ref, sm_ref, cw_ref, cb_ref, s0_ref, par_ref, nw_ref,
                o_ref, s_ref, ubuf, *, chunk, group):
    step = pl.program_id(1)
    rows = chunk * group

    @pl.when(step == 0)
    def _():
        ubuf[0:SUBLANES, :] = cb_ref[0]
        s_ref[0] = s0_ref[0]

    ubuf[SUBLANES:SUBLANES + rows, :] = qkv_ref[0]
    y = ubuf[SUBLANES - 3:SUBLANES - 3 + rows, :] * cw_ref[0:1, :]
    for i in range(1, CONV_W):
        y = y + ubuf[SUBLANES - 3 + i:SUBLANES - 3 + i + rows, :] * cw_ref[i:i + 1, :]
    ubuf[0:SUBLANES, :] = ubuf[rows:rows + SUBLANES, :]
    y = y * _sigmoid(y)

    sm = sm_ref[0]
    beta_all = _sigmoid(sm)
    xg = sm + par_ref[1:2, :]
    softplus = jnp.maximum(xg, 0.0) + jnp.log1p(jnp.exp(-jnp.abs(xg)))
    g_all = -jnp.exp(par_ref[0:1, :]) * softplus
    gc_all = _chunk_cumsum(g_all, chunk)
    gc_rows = gc_all.T

    heads = GDN_HEADS
    size = heads * chunk
    w_row = lax.broadcasted_iota(jnp.int32, (chunk, size), 0)
    w_lane = lax.broadcasted_iota(jnp.int32, (chunk, size), 1)
    w_col = w_lane % chunk
    w_head = w_lane // chunk
    causal_w = w_row >= w_col
    strict_w = w_row > w_col
    z = z_ref[0]
    nw = nw_ref[...]
    dot = functools.partial(jnp.dot, preferred_element_type=f32)

    def stack(a, rs, col0, width):
        return jnp.concatenate([a[rs, col0 + h * width:col0 + (h + 1) * width] for h in range(heads)], axis=0)

    def column_stack(a, rs, lane0):
        return jnp.concatenate([a[rs, lane0 + h:lane0 + h + 1] for h in range(heads)], axis=0)

    def column_wide(a, rs, lane0):
        out = jnp.zeros((chunk, size), f32)
        for h in range(heads):
            out = jnp.where(w_head == h, a[rs, lane0 + h:lane0 + h + 1], out)
        return out

    local = []
    for g in range(group):
        rs = slice(g * chunk, (g + 1) * chunk)
        last = (g + 1) * chunk - 1
        q = stack(y, rs, 0, GDN_DK)
        k = stack(y, rs, GDN_QK, GDN_DK)
        v = stack(y, rs, 2 * GDN_QK, GDN_DV)
        q = q * lax.rsqrt(jnp.sum(q * q, -1, keepdims=True) + L2_EPS) * (GDN_DK ** -0.5)
        k = k * lax.rsqrt(jnp.sum(k * k, -1, keepdims=True) + L2_EPS)
        beta_s = column_stack(beta_all, rs, SM_BETA)
        gc_s = column_stack(gc_all, rs, SM_DECAY)
        g_last = [gc_all[last:last + 1, SM_DECAY + h:SM_DECAY + h + 1] for h in range(heads)]
        g_last_s = jnp.concatenate([jnp.broadcast_to(gl, (chunk, 1)) for gl in g_last], axis=0)
        gc_row_w = jnp.concatenate([gc_rows[SM_DECAY + h:SM_DECAY + h + 1, rs] for h in range(heads)], axis=1)
        decay_w = jnp.exp(jnp.where(causal_w, column_wide(gc_all, rs, SM_DECAY) - gc_row_w, -jnp.inf))
        kk_w = _blocks_to_wide(_mm_nt(k, k), chunk)
        a_w = jnp.where(strict_w, column_wide(beta_all, rs, SM_BETA) * kk_w * decay_w, 0.0)
        t_w = _unit_lower_inverse_wide(a_w, chunk)
        rhs = jnp.concatenate([v * beta_s, k * (beta_s * jnp.exp(gc_s))], axis=-1)
        t_hi, t_lo = _split(t_w)
        r_hi, r_lo = _split(rhs)
        tb_hi = _wide_to_blocks(t_hi, chunk)
        both = dot(jnp.concatenate([tb_hi, _wide_to_blocks(t_lo, chunk)], axis=0), r_hi)
        sol = both[:size] + both[size:] + dot(tb_hi, r_lo)
        local.append(dict(u0=sol[:, :GDN_DV], wk=sol[:, GDN_DV:],
                          qk=_mm_nt(q, k) * _wide_to_blocks(decay_w, chunk),
                          q_dec=q * jnp.exp(gc_s), k_dec=k * jnp.exp(g_last_s - gc_s), g_last=g_last))

    s = [s_ref[0, h] for h in range(heads)]
    for g in range(group):
        rs = slice(g * chunk, (g + 1) * chunk)
        lc = local[g]
        u_parts, o_parts = [], []
        for h in range(heads):
            hs = slice(h * chunk, (h + 1) * chunk)
            both = _mm(jnp.concatenate([lc["wk"][hs], lc["q_dec"][hs]], axis=0), s[h])
            u_parts.append(lc["u0"][hs] - both[:chunk])
            o_parts.append(both[chunk:])
        u = jnp.concatenate(u_parts, axis=0)
        o = jnp.concatenate(o_parts, axis=0) + _mm(lc["qk"], u)
        for h in range(heads):
            hs = slice(h * chunk, (h + 1) * chunk)
            s[h] = s[h] * jnp.exp(lc["g_last"][h]) + _mm_tn(lc["k_dec"][hs], u[hs])
        z_s = stack(z, rs, 0, GDN_DV)
        o = o * lax.rsqrt(jnp.mean(o * o, -1, keepdims=True) + RMS_EPS) * nw * (z_s * _sigmoid(z_s))
        for h in range(heads):
            o_ref[0, rs, h * GDN_DV:(h + 1) * GDN_DV] = o[h * chunk:(h + 1) * chunk].astype(o_ref.dtype)
    for h in range(heads):
        s_ref[0, h] = s[h]


def _gdn_call(qkv, z, sm, conv_w, conv_buf8, s0, par, norm_w, chunk):
    b, t, _ = qkv.shape
    group = min(GDN_CHUNKS_PER_STEP, t // chunk)
    rows = chunk * group
    assert t % rows == 0
    return pl.pallas_call(
        functools.partial(_gdn_kernel, chunk=chunk, group=group),
        out_shape=[jax.ShapeDtypeStruct((b, t, GDN_VW), MXU_DTYPE),
                   jax.ShapeDtypeStruct((b, GDN_HEADS, GDN_DK, GDN_DV), f32)],
        grid=(b, t // rows),
        in_specs=[pl.BlockSpec((1, rows, GDN_CONV_CH), lambda i, j: (i, j, 0)),
                  pl.BlockSpec((1, rows, GDN_VW), lambda i, j: (i, j, 0)),
                  pl.BlockSpec((1, rows, LANES), lambda i, j: (i, j, 0)),
                  pl.BlockSpec((CONV_W, GDN_CONV_CH), lambda i, j: (0, 0)),
                  pl.BlockSpec((1, SUBLANES, GDN_CONV_CH), lambda i, j: (i, 0, 0)),
                  pl.BlockSpec((1, GDN_HEADS, GDN_DK, GDN_DV), lambda i, j: (i, 0, 0, 0)),
                  pl.BlockSpec((2, LANES), lambda i, j: (0, 0)),
                  pl.BlockSpec((1, GDN_DV), lambda i, j: (0, 0))],
        out_specs=[pl.BlockSpec((1, rows, GDN_VW), lambda i, j: (i, j, 0)),
                   pl.BlockSpec((1, GDN_HEADS, GDN_DK, GDN_DV), lambda i, j: (i, 0, 0, 0))],
        scratch_shapes=[pltpu.VMEM((SUBLANES + rows, GDN_CONV_CH), f32)],
        compiler_params=_params("parallel", "arbitrary"),
        name="gdn",
    )(qkv, z, sm, conv_w, conv_buf8, s0, par, norm_w)


def _dsa_kernel(q_ref, qi_ref, sm_ref, k_ref, v_ref, ki_ref, o_ref,
                key_hi, key_lo, m_s, l_s, acc_s, *, qb, tk, pos0, n_keys, topk):
    blk = pl.program_id(1)
    row = lax.broadcasted_iota(jnp.int32, (qb, 1), 0)
    q_pos = pos0 + blk * qb + row
    limit = jnp.minimum((q_pos // CHUNK + 1) * CHUNK, n_keys)
    last_limit = jnp.minimum(((pos0 + (blk + 1) * qb - 1) // CHUNK + 1) * CHUNK, n_keys)
    n_tiles = (last_limit + tk - 1) // tk
    lane = lax.broadcasted_iota(jnp.int32, (qb, LANES), 1)
    lower_half = lane < DSA_DH
    col = lax.broadcasted_iota(jnp.int32, (qb, tk), 1)
    groups = tk // LANES

    sm = sm_ref[0]
    qi_heads = []
    for h in range(IDX_HEADS):
        pair = qi_ref[0, :, (h // 2) * LANES:(h // 2 + 1) * LANES]
        keep = lower_half if h % 2 == 0 else jnp.logical_not(lower_half)
        qi_heads.append(jnp.where(keep, pair, jnp.zeros_like(pair)))
    wi = [sm[:, SM_WI + h:SM_WI + h + 1] * (IDX_HEADS ** -0.5) for h in range(IDX_HEADS)]

    def score_tile(kt, carry):
        ki_t = ki_ref[0, pl.ds(pl.multiple_of(kt * tk, tk), tk), :]
        score = jnp.zeros((qb, tk), f32)
        for h in range(IDX_HEADS):
            score = score + jnp.maximum(_mm_nt(qi_heads[h], ki_t), 0.0) * wi[h]
        bits = pltpu.bitcast(score, jnp.int32)
        key = bits ^ ((bits >> 31) & 0x7FFFFFFF)
        key = jnp.where(kt * tk + col < limit, key, INT_MIN)
        key_hi[kt] = (key >> 16).astype(jnp.int16)
        key_lo[kt] = ((key & 0xFFFF) - HALF_BIAS).astype(jnp.int16)
        return carry

    lax.fori_loop(0, n_tiles, score_tile, 0)

    one16, zero16 = jnp.int16(1), jnp.int16(0)

    def count(hit):
        def body(kt, acc):
            h16 = hit(kt)
            part = h16[:, 0:LANES]
            for g in range(1, groups):
                part = part + h16[:, g * LANES:(g + 1) * LANES]
            return acc + part
        acc = lax.fori_loop(0, n_tiles, body, jnp.zeros((qb, LANES), jnp.int16))
        return jnp.sum(acc.astype(jnp.int32), axis=1, keepdims=True)

    def kth_largest_half(ref, want):
        def bit_step(i, u):
            cand = u | (jnp.int32(1) << (15 - i))
            cand16 = (cand - HALF_BIAS).astype(jnp.int16)
            n_ge_cand = count(lambda kt: jnp.where(ref[kt] >= cand16, one16, zero16))
            return jnp.where(n_ge_cand >= want, cand, u)
        return lax.fori_loop(0, 16, bit_step, jnp.zeros((qb, 1), jnp.int32))

    hi_u = kth_largest_half(key_hi, topk)
    thr_hi = (hi_u - HALF_BIAS).astype(jnp.int16)
    n_hi_gt = count(lambda kt: jnp.where(key_hi[kt] > thr_hi, one16, zero16))

    def mask_low(kt, carry):
        key_lo[kt] = jnp.where(key_hi[kt] == thr_hi, key_lo[kt], jnp.int16(-HALF_BIAS))
        return carry

    lax.fori_loop(0, n_tiles, mask_low, 0)
    lo_u = kth_largest_half(key_lo, topk - n_hi_gt)
    lo_u = jnp.where(jnp.logical_and(hi_u == 0, lo_u == 0), 1, lo_u)
    thr_lo = (lo_u - HALF_BIAS).astype(jnp.int16)

    def tie_band(kt, low_pred):
        return jnp.where(key_hi[kt] == thr_hi, jnp.where(low_pred(key_lo[kt]), one16, zero16), zero16)

    n_gt = n_hi_gt + count(lambda kt: tie_band(kt, lambda lo: lo > thr_lo))
    n_ge = n_hi_gt + count(lambda kt: tie_band(kt, lambda lo: lo >= thr_lo))
    need = (topk - n_gt).astype(f32)
    surplus = jnp.max(jnp.where(n_ge > topk, 1, 0))

    q_pairs = []
    for pi in range(DSA_HEADS // 2):
        pair = q_ref[0, :, pi * LANES:(pi + 1) * LANES]
        zero = jnp.zeros_like(pair)
        q_pairs.append(jnp.concatenate([jnp.where(lower_half, pair, zero), jnp.where(lower_half, zero, pair)], axis=0))

    def attend(exact_ties):
        m_s[...] = jnp.full(m_s.shape, M_FLOOR, f32)
        l_s[...] = jnp.zeros(l_s.shape, f32)
        acc_s[...] = jnp.zeros(acc_s.shape, f32)
        tc_w = min(tk, TIE_CHUNK)
        if exact_ties:
            tr = lax.broadcasted_iota(jnp.int32, (tc_w, tc_w), 0)
            tc = lax.broadcasted_iota(jnp.int32, (tc_w, tc_w), 1)
            prefix = jnp.where(tr <= tc, 1.0, 0.0).astype(MXU_DTYPE)

        def body(kt, taken):
            if exact_ties:
                hi = key_hi[kt].astype(jnp.int32)
                lo = key_lo[kt].astype(jnp.int32)
                t_hi = thr_hi.astype(jnp.int32)
                t_lo = thr_lo.astype(jnp.int32)
                in_band = hi == t_hi
                above = jnp.where(hi > t_hi, 1.0, jnp.where(in_band, jnp.where(lo > t_lo, 1.0, 0.0), 0.0))
                tie = jnp.where(in_band, jnp.where(lo == t_lo, 1.0, 0.0), 0.0)
                parts = []
                for c0 in range(0, tk, tc_w):
                    tie_c = tie[:, c0:c0 + tc_w]
                    rank = taken + _mm(tie_c, prefix)
                    parts.append(above[:, c0:c0 + tc_w] + jnp.where(rank <= need, tie_c, 0.0))
                    taken = rank[:, tc_w - 1:tc_w]
                sel = parts[0] if len(parts) == 1 else jnp.concatenate(parts, axis=1)
            else:
                sel = tie_band(kt, lambda lo: lo >= thr_lo)
                sel = jnp.where(key_hi[kt] > thr_hi, one16, sel).astype(f32)
            bias = (1.0 - sel) * NEG_BIG
            bias2 = jnp.concatenate([bias, bias], axis=0)
            start = pl.multiple_of(kt * tk, tk)
            for pi in range(DSA_HEADS // 2):
                pr = slice(pi * LANES, (pi + 1) * LANES)
                logits = _mm_nt(q_pairs[pi], k_ref[0, pl.ds(start, tk), pr]) + bias2
                m_old = m_s[pi]
                m_new = jnp.maximum(m_old, jnp.max(logits, axis=1, keepdims=True))
                alpha = jnp.exp2(m_old - m_new)
                p = jnp.exp2(logits - m_new)
                l_s[pi] = alpha * l_s[pi] + jnp.sum(p, axis=1, keepdims=True)
                acc_s[pi] = alpha * acc_s[pi] + _mm(p, v_ref[0, pl.ds(start, tk), pr])
                m_s[pi] = m_new
            return taken

        lax.fori_loop(0, n_tiles, body, jnp.zeros((qb, 1), f32))
        for pi in range(DSA_HEADS // 2):
            out = acc_s[pi] / l_s[pi]
            o_ref[0, :, pi * LANES:(pi + 1) * LANES] = jnp.where(lower_half, out[:qb], out[qb:]).astype(o_ref.dtype)

    @pl.when(surplus == 0)
    def _():
        attend(False)

    @pl.when(surplus != 0)
    def _():
        attend(True)


def _dsa_call(qb_arr, qib_arr, sm, k_all, v_all, ki_all, pos0, n_keys, qb, tk):
    b, t, _ = qb_arr.shape
    lp = k_all.shape[1]
    assert lp % tk == 0 and t % qb == 0
    topk = min(TOPK_MAX, n_keys // 4)
    kern = functools.partial(_dsa_kernel, qb=qb, tk=tk, pos0=pos0, n_keys=n_keys, topk=topk)
    resident = dict(pipeline_mode=pl.Buffered(1))
    return pl.pallas_call(
        kern,
        out_shape=jax.ShapeDtypeStruct((b, t, DSA_W), MXU_DTYPE),
        grid=(b, t // qb),
        in_specs=[pl.BlockSpec((1, qb, DSA_W), lambda i, j: (i, j, 0)),
                  pl.BlockSpec((1, qb, IDX_W), lambda i, j: (i, j, 0)),
                  pl.BlockSpec((1, qb, LANES), lambda i, j: (i, j, 0)),
                  pl.BlockSpec((1, lp, DSA_W), lambda i, j: (i, 0, 0), **resident),
                  pl.BlockSpec((1, lp, DSA_W), lambda i, j: (i, 0, 0), **resident),
                  pl.BlockSpec((1, lp, LANES), lambda i, j: (i, 0, 0), **resident)],
        out_specs=pl.BlockSpec((1, qb, DSA_W), lambda i, j: (i, j, 0)),
        scratch_shapes=[pltpu.VMEM((lp // tk, qb, tk), jnp.int16),
                        pltpu.VMEM((lp // tk, qb, tk), jnp.int16),
                        pltpu.VMEM((DSA_HEADS // 2, 2 * qb, 1), f32),
                        pltpu.VMEM((DSA_HEADS // 2, 2 * qb, 1), f32),
                        pltpu.VMEM((DSA_HEADS // 2, 2 * qb, LANES), f32)],
        compiler_params=_params("parallel", "arbitrary"),
        name="dsa",
    )(qb_arr, qib_arr, sm, k_all, v_all, ki_all)


def _outproj_kernel(h_ref, og_ref, od_ref, w_ref, g_ref, b_ref, o_ref, *, alpha):
    mix = (jnp.dot(og_ref[...], w_ref[0:GDN_VW, :], preferred_element_type=f32)
           + jnp.dot(od_ref[...], w_ref[GDN_VW:, :], preferred_element_type=f32))
    o_ref[...] = _layer_norm(alpha * h_ref[...] + mix, g_ref[...], b_ref[...])


def _outproj_call(h, og, od, w, g, b, alpha):
    n, d = h.shape
    tm = _row_tile(n, 512)
    return pl.pallas_call(
        functools.partial(_outproj_kernel, alpha=alpha),
        out_shape=jax.ShapeDtypeStruct((n, d), f32),
        grid=(n // tm,),
        in_specs=[pl.BlockSpec((tm, d), lambda i: (i, 0)),
                  pl.BlockSpec((tm, GDN_VW), lambda i: (i, 0)),
                  pl.BlockSpec((tm, DSA_W), lambda i: (i, 0)),
                  pl.BlockSpec((GDN_VW + DSA_W, d), lambda i: (0, 0)),
                  pl.BlockSpec((1, d), lambda i: (0, 0)),
                  pl.BlockSpec((1, d), lambda i: (0, 0))],
        out_specs=pl.BlockSpec((tm, d), lambda i: (i, 0)),
        compiler_params=_params("parallel"),
        name="outproj_ln",
    )(h, og, od, w, g.reshape(1, d), b.reshape(1, d))


def _ffn_kernel(x_ref, wg_ref, wu_ref, wd_ref, g_ref, b_ref, o_ref, acc, *, alpha):
    j = pl.program_id(1)

    @pl.when(j == 0)
    def _():
        acc[...] = jnp.zeros(acc.shape, f32)

    xb = x_ref[...].astype(MXU_DTYPE)
    gate = jnp.dot(xb, wg_ref[...], preferred_element_type=f32)
    up = jnp.dot(xb, wu_ref[...], preferred_element_type=f32)
    act = gate * _sigmoid(gate) * up
    acc[...] += jnp.dot(act.astype(MXU_DTYPE), wd_ref[...], preferred_element_type=f32)

    @pl.when(j == pl.num_programs(1) - 1)
    def _():
        o_ref[...] = _layer_norm(alpha * x_ref[...] + acc[...], g_ref[...], b_ref[...])


def _ffn_call(x, wg, wu, wd, g, b, alpha):
    n, d = x.shape
    f = wg.shape[1]
    tm = _row_tile(n, 1024)
    tf = _row_tile(f, 512)
    return pl.pallas_call(
        functools.partial(_ffn_kernel, alpha=alpha),
        out_shape=jax.ShapeDtypeStruct((n, d), f32),
        grid=(n // tm, f // tf),
        in_specs=[pl.BlockSpec((tm, d), lambda i, j: (i, 0)),
                  pl.BlockSpec((d, tf), lambda i, j: (0, j)),
                  pl.BlockSpec((d, tf), lambda i, j: (0, j)),
                  pl.BlockSpec((tf, d), lambda i, j: (j, 0)),
                  pl.BlockSpec((1, d), lambda i, j: (0, 0)),
                  pl.BlockSpec((1, d), lambda i, j: (0, 0))],
        out_specs=pl.BlockSpec((tm, d), lambda i, j: (i, 0)),
        scratch_shapes=[pltpu.VMEM((tm, d), f32)],
        compiler_params=_params("parallel", "arbitrary"),
        name="ffn_ln",
    )(x, wg, wu, wd, g.reshape(1, d), b.reshape(1, d))


def _moe_kernel(x_ref, wr_ref, wg_ref, wu_ref, wd_ref, g_ref, b_ref, o_ref,
                acc, gates, gcol, *, alpha, n_experts):
    e = pl.program_id(1)
    j = pl.program_id(2)
    tm = x_ref.shape[0]
    lane = lax.broadcasted_iota(jnp.int32, (tm, LANES), 1)

    @pl.when(jnp.logical_and(e == 0, j == 0))
    def _():
        acc[...] = jnp.zeros(acc.shape, f32)
        logits = _mm_f32(x_ref[...], wr_ref[...])
        logits = jnp.where(lane < n_experts, logits, -jnp.inf)
        m1 = jnp.max(logits, axis=1, keepdims=True)
        i1 = jnp.min(jnp.where(logits == m1, lane, LANES), axis=1, keepdims=True)
        rest = jnp.where(lane == i1, -jnp.inf, logits)
        m2 = jnp.max(rest, axis=1, keepdims=True)
        i2 = jnp.min(jnp.where(rest == m2, lane, LANES), axis=1, keepdims=True)
        e2 = jnp.exp(m2 - m1)
        w1 = 1.0 / (1.0 + e2)
        gates[...] = jnp.where(lane == i1, w1, 0.0) + jnp.where(lane == i2, e2 * w1, 0.0)

    @pl.when(j == 0)
    def _():
        gcol[...] = jnp.sum(jnp.where(lane == e, gates[...], 0.0), axis=1, keepdims=True)

    xb = x_ref[...].astype(MXU_DTYPE)
    gate = jnp.dot(xb, wg_ref[0], preferred_element_type=f32)
    up = jnp.dot(xb, wu_ref[0], preferred_element_type=f32)
    act = gate * _sigmoid(gate) * up * gcol[...]
    acc[...] += jnp.dot(act.astype(MXU_DTYPE), wd_ref[0], preferred_element_type=f32)

    @pl.when(jnp.logical_and(e == n_experts - 1, j == pl.num_programs(2) - 1))
    def _():
        o_ref[...] = _layer_norm(alpha * x_ref[...] + acc[...], g_ref[...], b_ref[...])


def _moe_call(x, wr, wg, wu, wd, g, b, alpha):
    n, d = x.shape
    n_experts, _, f = wg.shape
    tm = _row_tile(n, 1024)
    tf = _row_tile(f, 512)
    return pl.pallas_call(
        functools.partial(_moe_kernel, alpha=alpha, n_experts=n_experts),
        out_shape=jax.ShapeDtypeStruct((n, d), f32),
        grid=(n // tm, n_experts, f // tf),
        in_specs=[pl.BlockSpec((tm, d), lambda i, e, j: (i, 0)),
                  pl.BlockSpec((d, LANES), lambda i, e, j: (0, 0)),
                  pl.BlockSpec((1, d, tf), lambda i, e, j: (e, 0, j)),
                  pl.BlockSpec((1, d, tf), lambda i, e, j: (e, 0, j)),
                  pl.BlockSpec((1, tf, d), lambda i, e, j: (e, j, 0)),
                  pl.BlockSpec((1, d), lambda i, e, j: (0, 0)),
                  pl.BlockSpec((1, d), lambda i, e, j: (0, 0))],
        out_specs=pl.BlockSpec((tm, d), lambda i, e, j: (i, 0)),
        scratch_shapes=[pltpu.VMEM((tm, d), f32),
                        pltpu.VMEM((tm, LANES), f32),
                        pltpu.VMEM((tm, 1), f32)],
        compiler_params=_params("parallel", "arbitrary", "arbitrary"),
        name="moe_ln",
    )(x, wr, wg, wu, wd, g.reshape(1, d), b.reshape(1, d))


def _pack_w_in(w_in):
    o_z = GDN_CONV_CH
    o_b = o_z + GDN_VW
    o_a = o_b + GDN_HEADS
    o_qd = o_a + GDN_HEADS
    o_kd = o_qd + DSA_W
    o_vd = o_kd + DSA_W
    o_qi = o_vd + DSA_W
    o_ki = o_qi + IDX_W
    o_wi = o_ki + IDX_DIM
    assert w_in.shape[-1] == o_wi + IDX_HEADS
    sl = lambda a, n: w_in[..., a:a + n]
    pad = jnp.zeros(w_in.shape[:-1] + (LANES - 2 * GDN_HEADS - IDX_HEADS,), w_in.dtype)
    packed = jnp.concatenate([
        sl(0, GDN_CONV_CH), sl(o_z, GDN_VW), sl(o_qd, DSA_W), sl(o_kd, DSA_W), sl(o_vd, DSA_W),
        sl(o_qi, IDX_W), sl(o_ki, IDX_DIM), sl(o_ki, IDX_DIM),
        sl(o_b, GDN_HEADS), sl(o_a, GDN_HEADS), sl(o_wi, IDX_HEADS), pad], axis=-1)
    assert packed.shape[-1] == C_END
    return packed.astype(MXU_DTYPE)


def _rope_tables(pos):
    half = DSA_DH // 8
    inv_freq = ROPE_THETA ** (-jnp.arange(half, dtype=f32) / half)
    ang = pos.astype(f32)[:, None] * inv_freq[None, :]
    cos, sin = jnp.cos(ang), jnp.sin(ang)
    t = pos.shape[0]
    ones = jnp.ones((t, DSA_DH - 2 * half), f32)
    zeros = jnp.zeros((t, DSA_DH - 2 * half), f32)
    z8 = jnp.zeros((t, half), f32)
    cos_t = jnp.concatenate([cos, cos, ones], -1)
    lo = jnp.concatenate([-sin, z8, zeros], -1)
    hi = jnp.concatenate([z8, sin, zeros], -1)
    return tuple(jnp.tile(a, (1, LANES // DSA_DH)) for a in (cos_t, lo, hi))


def _gdn_params(a_log, dt_bias):
    row = lambda v: jnp.zeros((LANES,), f32).at[SM_DECAY:SM_DECAY + GDN_HEADS].set(v.astype(f32))
    return jnp.stack([row(a_log), row(dt_bias)])


def _trunk(x, pos0, conv_bufs, s0s, past_k, past_v, past_ik, chunk, wts, depth):
    b, t, d = x.shape
    n = b * t
    alpha = (2.0 * depth) ** 0.25
    tabs = _rope_tables(pos0 + jnp.arange(t))
    past = 0 if past_k is None else past_k.shape[2]
    n_keys = past + t
    assert t >= CONV_W - 1
    qb = min(t, DSA_Q_BLOCK)
    tk = DSA_KEY_TILE
    lp = -(-n_keys // tk) * tk

    def with_past(new, old, l):
        parts = [] if old is None else [old[l].reshape(b, past, -1).astype(MXU_DTYPE)]
        parts.append(new.reshape(b, t, -1))
        if lp > n_keys:
            parts.append(jnp.zeros((b, lp - n_keys, new.shape[-1]), MXU_DTYPE))
        return parts[0] if len(parts) == 1 else jnp.concatenate(parts, axis=1)

    h = _ln_call(x.reshape(n, d), wts["ln_in_g"], wts["ln_in_b"])
    new_k, new_v, new_ik, new_conv, new_s = [], [], [], [], []
    for l in range(depth):
        (qkv, z, k_f, v_f, ki_f, q_b, k_b, v_b, qi_b, ki_b, sm) = _inproj_call(h, wts["w_in"][l], tabs, t)
        qkv3 = qkv.reshape(b, t, GDN_CONV_CH)
        buf8 = jnp.concatenate([jnp.zeros((b, SUBLANES - (CONV_W - 1), GDN_CONV_CH), f32), conv_bufs[l]], axis=1)
        o_g, s_new = _gdn_call(qkv3, z.reshape(b, t, GDN_VW), sm.reshape(b, t, LANES), wts["conv_w"][l],
                               buf8, s0s[l], _gdn_params(wts["gdn_a_log"][l], wts["gdn_dt_bias"][l]),
                               wts["gdn_norm_w"][l].reshape(1, GDN_DV), chunk)
        ki_dup = None if past_ik is None else jnp.concatenate([past_ik, past_ik], axis=-1)
        o_d = _dsa_call(q_b.reshape(b, t, DSA_W), qi_b.reshape(b, t, IDX_W), sm.reshape(b, t, LANES),
                        with_past(k_b, past_k, l), with_past(v_b, past_v, l), with_past(ki_b, ki_dup, l),
                        past, n_keys, qb, tk)
        h = _outproj_call(h, o_g.reshape(n, GDN_VW), o_d.reshape(n, DSA_W), wts["w_out"][l],
                          wts["ln1_g"][l], wts["ln1_b"][l], alpha)
        if l % 2 == 0:
            i = l // 2
            h = _ffn_call(h, wts["ffn_wg"][i], wts["ffn_wu"][i], wts["ffn_wd"][i],
                          wts["ln2_g"][l], wts["ln2_b"][l], alpha)
        else:
            i = l // 2
            h = _moe_call(h, wts["moe_router"][i], wts["moe_wg"][i], wts["moe_wu"][i], wts["moe_wd"][i],
                          wts["ln2_g"][l], wts["ln2_b"][l], alpha)
        new_k.append(k_f.reshape(b, t, DSA_HEADS, DSA_DH))
        new_v.append(v_f.reshape(b, t, DSA_HEADS, DSA_DH))
        new_ik.append(ki_f.reshape(b, t, IDX_DIM))
        new_conv.append(qkv3[:, t - (CONV_W - 1):])
        new_s.append(s_new)
    return (h.reshape(b, t, d), jnp.stack(new_k), jnp.stack(new_v), jnp.stack(new_ik),
            jnp.stack(new_s), jnp.stack(new_conv))


def kernel(x_prompt, x_sample, cache_k, cache_v, cache_idx_k, state_gdn, state_conv, ln_in_g, ln_in_b, w_in, conv_w, gdn_a_log, gdn_dt_bias, gdn_norm_w, w_out, ln1_g, ln1_b, ln2_g, ln2_b, ffn_wg, ffn_wu, ffn_wd, moe_router, moe_wg, moe_wu, moe_wd):
    depth = w_in.shape[0]
    n_experts = moe_router.shape[-1]
    bp, tp, d = x_prompt.shape
    cast = lambda a: a.astype(MXU_DTYPE)
    router = jnp.concatenate(
        [moe_router.astype(f32), jnp.zeros(moe_router.shape[:-1] + (LANES - n_experts,), f32)], axis=-1)
    wts = dict(ln_in_g=ln_in_g, ln_in_b=ln_in_b, w_in=_pack_w_in(w_in), conv_w=conv_w,
               gdn_a_log=gdn_a_log, gdn_dt_bias=gdn_dt_bias, gdn_norm_w=gdn_norm_w, w_out=cast(w_out),
               ln1_g=ln1_g, ln1_b=ln1_b, ln2_g=ln2_g, ln2_b=ln2_b,
               ffn_wg=cast(ffn_wg), ffn_wu=cast(ffn_wu), ffn_wd=cast(ffn_wd),
               moe_router=router, moe_wg=cast(moe_wg), moe_wu=cast(moe_wu), moe_wd=cast(moe_wd))
    zero_conv = jnp.zeros((depth, bp, CONV_W - 1, GDN_CONV_CH), f32)
    zero_s = jnp.zeros((depth, bp, GDN_HEADS, GDN_DK, GDN_DV), f32)
    y_p, k_p, v_p, ik_p, s_p, conv_p = _trunk(
        x_prompt, 0, zero_conv, zero_s, None, None, None, CHUNK, wts, depth)
    ts = x_sample.shape[1]
    past = cache_k.shape[2]
    y_s, k_s, v_s, ik_s, s_s, conv_s = _trunk(
        x_sample, past, state_conv, state_gdn, cache_k, cache_v, cache_idx_k, ts, wts, depth)
    return (y_p, y_s, k_p, v_p, ik_p, s_p, conv_p, k_s, v_s, ik_s, s_s, conv_s)
```

```python
import functools

import jax
import jax.numpy as jnp
from jax import lax
from jax.experimental import pallas as pl
from jax.experimental.pallas import tpu as pltpu

CHUNK = 64
CONV_W = 4
GDN_HEADS = 4
GDN_DK = 128
GDN_DV = 128
DSA_HEADS = 8
DSA_DH = 64
IDX_HEADS = 4
IDX_DIM = 64
TOPK_MAX = 256
ROPE_THETA = 500000.0
TOP_K_EXPERTS = 2
LN_EPS = 1e-5
RMS_EPS = 1e-6
L2_EPS = 1e-6

GDN_QK = GDN_HEADS * GDN_DK
GDN_VW = GDN_HEADS * GDN_DV
GDN_CONV_CH = 2 * GDN_QK + GDN_VW
DSA_W = DSA_HEADS * DSA_DH
IDX_W = IDX_HEADS * IDX_DIM

LANES = 128
SUBLANES = 8
VMEM_LIMIT = 56 * 1024 * 1024

DSA_Q_BLOCK = 256
DSA_KEY_TILE = 2048
GDN_CHUNKS_PER_STEP = 4
MOE_TOKEN_TILE = 2048
MOE_SLOT_BLOCK = 576
MOE_PREFIX_BLOCK = 256
MOE_PLAN_ROWS = 24

MXU_DTYPE = jnp.bfloat16

C_QKV = 0
C_Z = C_QKV + GDN_CONV_CH
C_QD = C_Z + GDN_VW
C_KD = C_QD + DSA_W
C_VD = C_KD + DSA_W
C_QI = C_VD + DSA_W
C_KI = C_QI + IDX_W
C_SM = C_KI + LANES
C_END = C_SM + LANES
SM_BETA = 0
SM_DECAY = GDN_HEADS
SM_WI = 2 * GDN_HEADS

INT_MIN = -2 ** 31
HALF_BIAS = 2 ** 15
TIE_CHUNK = 512
NEG_BIG = -3e38
M_FLOOR = -1e38
LOG2E = 1.4426950408889634

f32 = jnp.float32


def _mm(a, b):
    return jnp.dot(a.astype(MXU_DTYPE), b.astype(MXU_DTYPE), preferred_element_type=f32)


def _mm_nt(a, b):
    return lax.dot_general(a.astype(MXU_DTYPE), b.astype(MXU_DTYPE), (((1,), (1,)), ((), ())),
                           preferred_element_type=f32)


def _mm_tn(a, b):
    return lax.dot_general(a.astype(MXU_DTYPE), b.astype(MXU_DTYPE), (((0,), (0,)), ((), ())),
                           preferred_element_type=f32)


def _mm_f32(a, b):
    return jnp.dot(a, b, precision=lax.Precision.HIGHEST, preferred_element_type=f32)


def _split(x):
    hi = x.astype(MXU_DTYPE)
    lo = (x - hi.astype(f32)).astype(MXU_DTYPE)
    return hi, lo


def _mm_x3(a, b):
    a_hi, a_lo = _split(a)
    b_hi, b_lo = _split(b)
    dot = functools.partial(jnp.dot, preferred_element_type=f32)
    return dot(a_hi, b_hi) + (dot(a_hi, b_lo) + dot(a_lo, b_hi))


def _sigmoid(x):
    return 1.0 / (1.0 + jnp.exp(-x))


def _layer_norm(r, g, b):
    mu = jnp.mean(r, axis=-1, keepdims=True)
    xc = r - mu
    var = jnp.mean(xc * xc, axis=-1, keepdims=True)
    return xc * lax.rsqrt(var + LN_EPS) * g + b


def _params(*sem):
    return pltpu.CompilerParams(dimension_semantics=sem, vmem_limit_bytes=VMEM_LIMIT)


def _row_tile(n, want):
    t = min(n, want)
    assert n % t == 0, (n, t)
    return t


def _ln_kernel(x_ref, g_ref, b_ref, o_ref):
    o_ref[...] = _layer_norm(x_ref[...], g_ref[...], b_ref[...])


def _ln_call(x, g, b):
    n, d = x.shape
    tm = _row_tile(n, 1024)
    return pl.pallas_call(
        _ln_kernel,
        out_shape=jax.ShapeDtypeStruct((n, d), f32),
        grid=(n // tm,),
        in_specs=[pl.BlockSpec((tm, d), lambda i: (i, 0)),
                  pl.BlockSpec((1, d), lambda i: (0, 0)),
                  pl.BlockSpec((1, d), lambda i: (0, 0))],
        out_specs=pl.BlockSpec((tm, d), lambda i: (i, 0)),
        compiler_params=_params("parallel"),
        name="ln_in",
    )(x, g.reshape(1, d), b.reshape(1, d))


def _rope(x, cos_t, sin_lo, sin_hi):
    w = x.shape[1]
    reps = w // LANES
    c = jnp.tile(cos_t, (1, reps))
    lo = jnp.tile(sin_lo, (1, reps))
    hi = jnp.tile(sin_hi, (1, reps))
    half = DSA_DH // 8
    from_above = pltpu.roll(x, w - half, 1)
    from_below = pltpu.roll(x, half, 1)
    return x * c + from_above * lo + from_below * hi


def _inproj_kernel(x_ref, w_ref, cos_ref, lo_ref, hi_ref,
                   qkv_o, z_o, k_o, v_o, ki_o, qb_o, kb_o, vb_o, qib_o, kib_o, sm_o):
    xb = x_ref[...].astype(MXU_DTYPE)
    cos_t, lo, hi = cos_ref[...], lo_ref[...], hi_ref[...]

    def proj(c0, c1):
        return jnp.dot(xb, w_ref[:, c0:c1], preferred_element_type=f32)

    qkv_o[...] = proj(C_QKV, C_Z)
    z_o[...] = proj(C_Z, C_QD)
    q = _rope(proj(C_QD, C_KD), cos_t, lo, hi) * (DSA_DH ** -0.5 * LOG2E)
    qb_o[...] = q.astype(qb_o.dtype)
    k = _rope(proj(C_KD, C_VD), cos_t, lo, hi)
    k_o[...] = k
    kb_o[...] = k.astype(kb_o.dtype)
    v = proj(C_VD, C_QI)
    v_o[...] = v
    vb_o[...] = v.astype(vb_o.dtype)
    qi = _rope(proj(C_QI, C_KI), cos_t, lo, hi) * (IDX_DIM ** -0.5)
    qib_o[...] = qi.astype(qib_o.dtype)
    ki = _rope(proj(C_KI, C_SM), cos_t, lo, hi)
    ki_o[...] = ki[:, :IDX_DIM]
    kib_o[...] = ki.astype(kib_o.dtype)
    sm_o[...] = proj(C_SM, C_END)


def _inproj_call(h, w_packed, tabs, seq_len):
    n, d = h.shape
    tm = _row_tile(n, 512)
    cos_t, lo, hi = tabs
    if tm > seq_len:
        assert tm % seq_len == 0
        cos_t, lo, hi = (jnp.tile(a, (tm // seq_len, 1)) for a in (cos_t, lo, hi))
        nblk = 1
    else:
        assert seq_len % tm == 0
        nblk = seq_len // tm
    tab_spec = pl.BlockSpec((tm, LANES), lambda i: (i % nblk, 0))

    def rows(w):
        return pl.BlockSpec((tm, w), lambda i: (i, 0))

    widths = [(GDN_CONV_CH, f32), (GDN_VW, f32), (DSA_W, f32), (DSA_W, f32), (IDX_DIM, f32),
              (DSA_W, MXU_DTYPE), (DSA_W, MXU_DTYPE), (DSA_W, MXU_DTYPE), (IDX_W, MXU_DTYPE),
              (LANES, MXU_DTYPE), (LANES, f32)]
    return pl.pallas_call(
        _inproj_kernel,
        out_shape=[jax.ShapeDtypeStruct((n, w), dt) for w, dt in widths],
        grid=(n // tm,),
        in_specs=[rows(d),
                  pl.BlockSpec((d, C_END), lambda i: (0, 0), pipeline_mode=pl.Buffered(1)),
                  tab_spec, tab_spec, tab_spec],
        out_specs=[rows(w) for w, _ in widths],
        compiler_params=_params("parallel"),
        name="inproj",
    )(h, w_packed, cos_t, lo, hi)


def _wide_to_blocks(w, chunk):
    size = w.shape[1]
    r = lax.broadcasted_iota(jnp.int32, (size, size), 0)
    c = lax.broadcasted_iota(jnp.int32, (size, size), 1)
    tiled = jnp.concatenate([w] * (size // chunk), axis=0)
    return jnp.where(r // chunk == c // chunk, tiled, jnp.zeros_like(tiled))


def _blocks_to_wide(sq, chunk):
    size = sq.shape[1]
    blk = lax.broadcasted_iota(jnp.int32, (chunk, size), 1) // chunk
    out = jnp.zeros((chunk, size), sq.dtype)
    for h in range(size // chunk):
        out = jnp.where(blk == h, sq[h * chunk:(h + 1) * chunk, :], out)
    return out


def _wide_matmul(x, p, chunk):
    x_hi, x_lo = _split(x)
    p_hi, p_lo = _split(p)
    dot = functools.partial(jnp.dot, preferred_element_type=f32)
    both = dot(jnp.concatenate([x_hi, x_lo], axis=0), _wide_to_blocks(p_hi, chunk))
    return both[:chunk] + both[chunk:] + dot(x_hi, _wide_to_blocks(p_lo, chunk))


def _unit_lower_inverse_wide(a, chunk):
    size = a.shape[1]
    r = lax.broadcasted_iota(jnp.int32, (chunk, size), 0)
    c = lax.broadcasted_iota(jnp.int32, (chunk, size), 1) % chunk

    def same_block(n):
        return (r // n) == (c // n)

    base = SUBLANES
    a0 = jnp.where(same_block(base), a, 0.0)
    p = _wide_matmul(a0, a0, chunk)
    x = jnp.where(r == c, 1.0, 0.0) - a0
    x = x + _wide_matmul(x, p, chunk)
    p = _wide_matmul(p, p, chunk)
    x = x + _wide_matmul(x, p, chunk)
    n = base
    while n < chunk:
        off = jnp.where(same_block(2 * n), jnp.where(same_block(n), 0.0, a), 0.0)
        x = x - _wide_matmul(_wide_matmul(x, off, chunk), x, chunk)
        n *= 2
    return x


def _chunk_cumsum(g, chunk):
    rows = g.shape[0]
    r = lax.broadcasted_iota(jnp.int32, (rows, rows), 0)
    c = lax.broadcasted_iota(jnp.int32, (rows, rows), 1)
    tri = jnp.where(jnp.logical_and(r // chunk == c // chunk, r >= c), 1.0, 0.0).astype(MXU_DTYPE)
    dot = functools.partial(jnp.dot, preferred_element_type=f32)
    g1 = g.astype(MXU_DTYPE)
    rem = g - g1.astype(f32)
    g2 = rem.astype(MXU_DTYPE)
    g3 = (rem - g2.astype(f32)).astype(MXU_DTYPE)
    return dot(tri, g1) + (dot(tri, g2) + dot(tri, g3))


def _gdn_kernel(qkv_ref, z_ref, sm_ref, cw_ref, cb_ref, s0_ref, par_ref, nw_ref,
                o_ref, s_ref, ubuf, *, chunk, group):
    step = pl.program_id(1)
    rows = chunk * group

    @pl.when(step == 0)
    def _():
        ubuf[0:SUBLANES, :] = cb_ref[0]
        s_ref[0] = s0_ref[0]

    ubuf[SUBLANES:SUBLANES + rows, :] = qkv_ref[0]
    y = ubuf[SUBLANES - 3:SUBLANES - 3 + rows, :] * cw_ref[0:1, :]
    for i in range(1, CONV_W):
        y = y + ubuf[SUBLANES - 3 + i:SUBLANES - 3 + i + rows, :] * cw_ref[i:i + 1, :]
    ubuf[0:SUBLANES, :] = ubuf[rows:rows + SUBLANES, :]
    y = y * _sigmoid(y)

    sm = sm_ref[0]
    beta_all = _sigmoid(sm)
    xg = sm + par_ref[1:2, :]
    softplus = jnp.maximum(xg, 0.0) + jnp.log1p(jnp.exp(-jnp.abs(xg)))
    g_all = -jnp.exp(par_ref[0:1, :]) * softplus
    gc_all = _chunk_cumsum(g_all, chunk)
    gc_rows = gc_all.T

    heads = GDN_HEADS
    size = heads * chunk
    w_row = lax.broadcasted_iota(jnp.int32, (chunk, size), 0)
    w_lane = lax.broadcasted_iota(jnp.int32, (chunk, size), 1)
    w_col = w_lane % chunk
    w_head = w_lane // chunk
    causal_w = w_row >= w_col
    strict_w = w_row > w_col
    z = z_ref[0]
    nw = nw_ref[...]
    dot = functools.partial(jnp.dot, preferred_element_type=f32)

    def stack(a, rs, col0, width):
        return jnp.concatenate([a[rs, col0 + h * width:col0 + (h + 1) * width] for h in range(heads)], axis=0)

    def column_stack(a, rs, lane0):
        return jnp.concatenate([a[rs, lane0 + h:lane0 + h + 1] for h in range(heads)], axis=0)

    def column_wide(a, rs, lane0):
        out = jnp.zeros((chunk, size), f32)
        for h in range(heads):
            out = jnp.where(w_head == h, a[rs, lane0 + h:lane0 + h + 1], out)
        return out

    local = []
    for g in range(group):
        rs = slice(g * chunk, (g + 1) * chunk)
        last = (g + 1) * chunk - 1
        q = stack(y, rs, 0, GDN_DK)
        k = stack(y, rs, GDN_QK, GDN_DK)
        v = stack(y, rs, 2 * GDN_QK, GDN_DV)
        q = q * lax.rsqrt(jnp.sum(q * q, -1, keepdims=True) + L2_EPS) * (GDN_DK ** -0.5)
        k = k * lax.rsqrt(jnp.sum(k * k, -1, keepdims=True) + L2_EPS)
        beta_s = column_stack(beta_all, rs, SM_BETA)
        gc_s = column_stack(gc_all, rs, SM_DECAY)
        g_last = [gc_all[last:last + 1, SM_DECAY + h:SM_DECAY + h + 1] for h in range(heads)]
        g_last_s = jnp.concatenate([jnp.broadcast_to(gl, (chunk, 1)) for gl in g_last], axis=0)
        gc_row_w = jnp.concatenate([gc_rows[SM_DECAY + h:SM_DECAY + h + 1, rs] for h in range(heads)], axis=1)
        decay_w = jnp.exp(jnp.where(causal_w, column_wide(gc_all, rs, SM_DECAY) - gc_row_w, -jnp.inf))
        kk_w = _blocks_to_wide(_mm_nt(k, k), chunk)
        a_w = jnp.where(strict_w, column_wide(beta_all, rs, SM_BETA) * kk_w * decay_w, 0.0)
        t_w = _unit_lower_inverse_wide(a_w, chunk)
        rhs = jnp.concatenate([v * beta_s, k * (beta_s * jnp.exp(gc_s))], axis=-1)
        t_hi, t_lo = _split(t_w)
        r_hi, r_lo = _split(rhs)
        tb_hi = _wide_to_blocks(t_hi, chunk)
        both = dot(jnp.concatenate([tb_hi, _wide_to_blocks(t_lo, chunk)], axis=0), r_hi)
        sol = both[:size] + both[size:] + dot(tb_hi, r_lo)
        local.append(dict(u0=sol[:, :GDN_DV], wk=sol[:, GDN_DV:],
                          qk=_mm_nt(q, k) * _wide_to_blocks(decay_w, chunk),
                          q_dec=q * jnp.exp(gc_s), k_dec=k * jnp.exp(g_last_s - gc_s), g_last=g_last))

    s = [s_ref[0, h] for h in range(heads)]
    for g in range(group):
        rs = slice(g * chunk, (g + 1) * chunk)
        lc = local[g]
        u_parts, o_parts = [], []
        for h in range(heads):
            hs = slice(h * chunk, (h + 1) * chunk)
            both = _mm(jnp.concatenate([lc["wk"][hs], lc["q_dec"][hs]], axis=0), s[h])
            u_parts.append(lc["u0"][hs] - both[:chunk])
            o_parts.append(both[chunk:])
        u = jnp.concatenate(u_parts, axis=0)
        o = jnp.concatenate(o_parts, axis=0) + _mm(lc["qk"], u)
        for h in range(heads):
            hs = slice(h * chunk, (h + 1) * chunk)
            s[h] = s[h] * jnp.exp(lc["g_last"][h]) + _mm_tn(lc["k_dec"][hs], u[hs])
        z_s = stack(z, rs, 0, GDN_DV)
        o = o * lax.rsqrt(jnp.mean(o * o, -1, keepdims=True) + RMS_EPS) * nw * (z_s * _sigmoid(z_s))
        for h in range(heads):
            o_ref[0, rs, h * GDN_DV:(h + 1) * GDN_DV] = o[h * chunk:(h + 1) * chunk].astype(o_ref.dtype)
    for h in range(heads):
        s_ref[0, h] = s[h]


def _gdn_call(qkv, z, sm, conv_w, conv_buf8, s0, par, norm_w, chunk):
    b, t, _ = qkv.shape
    group = min(GDN_CHUNKS_PER_STEP, t // chunk)
    rows = chunk * group
    assert t % rows == 0
    return pl.pallas_call(
        functools.partial(_gdn_kernel, chunk=chunk, group=group),
        out_shape=[jax.ShapeDtypeStruct((b, t, GDN_VW), MXU_DTYPE),
                   jax.ShapeDtypeStruct((b, GDN_HEADS, GDN_DK, GDN_DV), f32)],
        grid=(b, t // rows),
        in_specs=[pl.BlockSpec((1, rows, GDN_CONV_CH), lambda i, j: (i, j, 0)),
                  pl.BlockSpec((1, rows, GDN_VW), lambda i, j: (i, j, 0)),
                  pl.BlockSpec((1, rows, LANES), lambda i, j: (i, j, 0)),
                  pl.BlockSpec((CONV_W, GDN_CONV_CH), lambda i, j: (0, 0)),
                  pl.BlockSpec((1, SUBLANES, GDN_CONV_CH), lambda i, j: (i, 0, 0)),
                  pl.BlockSpec((1, GDN_HEADS, GDN_DK, GDN_DV), lambda i, j: (i, 0, 0, 0)),
                  pl.BlockSpec((2, LANES), lambda i, j: (0, 0)),
                  pl.BlockSpec((1, GDN_DV), lambda i, j: (0, 0))],
        out_specs=[pl.BlockSpec((1, rows, GDN_VW), lambda i, j: (i, j, 0)),
                   pl.BlockSpec((1, GDN_HEADS, GDN_DK, GDN_DV), lambda i, j: (i, 0, 0, 0))],
        scratch_shapes=[pltpu.VMEM((SUBLANES + rows, GDN_CONV_CH), f32)],
        compiler_params=_params("parallel", "arbitrary"),
        name="gdn",
    )(qkv, z, sm, conv_w, conv_buf8, s0, par, norm_w)


def _dsa_kernel(q_ref, qi_ref, sm_ref, k_ref, v_ref, ki_ref, o_ref,
                key_hi, key_lo, m_s, l_s, acc_s, *, qb, tk, pos0, n_keys, topk):
    blk = pl.program_id(1)
    row = lax.broadcasted_iota(jnp.int32, (qb, 1), 0)
    q_pos = pos0 + blk * qb + row
    limit = jnp.minimum((q_pos // CHUNK + 1) * CHUNK, n_keys)
    last_limit = jnp.minimum(((pos0 + (blk + 1) * qb - 1) // CHUNK + 1) * CHUNK, n_keys)
    n_tiles = (last_limit + tk - 1) // tk
    lane = lax.broadcasted_iota(jnp.int32, (qb, LANES), 1)
    lower_half = lane < DSA_DH
    col = lax.broadcasted_iota(jnp.int32, (qb, tk), 1)
    groups = tk // LANES

    sm = sm_ref[0]
    qi_heads = []
    for h in range(IDX_HEADS):
        pair = qi_ref[0, :, (h // 2) * LANES:(h // 2 + 1) * LANES]
        keep = lower_half if h % 2 == 0 else jnp.logical_not(lower_half)
        qi_heads.append(jnp.where(keep, pair, jnp.zeros_like(pair)))
    wi = [sm[:, SM_WI + h:SM_WI + h + 1] * (IDX_HEADS ** -0.5) for h in range(IDX_HEADS)]

    def score_tile(kt, carry):
        ki_t = ki_ref[0, pl.ds(pl.multiple_of(kt * tk, tk), tk), :]
        score = jnp.zeros((qb, tk), f32)
        for h in range(IDX_HEADS):
            score = score + jnp.maximum(_mm_nt(qi_heads[h], ki_t), 0.0) * wi[h]
        bits = pltpu.bitcast(score, jnp.int32)
        key = bits ^ ((bits >> 31) & 0x7FFFFFFF)
        key = jnp.where(kt * tk + col < limit, key, INT_MIN)
        key_hi[kt] = (key >> 16).astype(jnp.int16)
        key_lo[kt] = ((key & 0xFFFF) - HALF_BIAS).astype(jnp.int16)
        return carry

    lax.fori_loop(0, n_tiles, score_tile, 0)

    one16, zero16 = jnp.int16(1), jnp.int16(0)

    def count(hit):
        def body(kt, acc):
            h16 = hit(kt)
            part = h16[:, 0:LANES]
            for g in range(1, groups):
                part = part + h16[:, g * LANES:(g + 1) * LANES]
            return acc + part
        acc = lax.fori_loop(0, n_tiles, body, jnp.zeros((qb, LANES), jnp.int16))
        return jnp.sum(acc.astype(jnp.int32), axis=1, keepdims=True)

    def kth_largest_half(ref, want):
        def bit_step(i, u):
            cand = u | (jnp.int32(1) << (15 - i))
            cand16 = (cand - HALF_BIAS).astype(jnp.int16)
            n_ge_cand = count(lambda kt: jnp.where(ref[kt] >= cand16, one16, zero16))
            return jnp.where(n_ge_cand >= want, cand, u)
        return lax.fori_loop(0, 16, bit_step, jnp.zeros((qb, 1), jnp.int32))

    hi_u = kth_largest_half(key_hi, topk)
    thr_hi = (hi_u - HALF_BIAS).astype(jnp.int16)
    n_hi_gt = count(lambda kt: jnp.where(key_hi[kt] > thr_hi, one16, zero16))

    def mask_low(kt, carry):
        key_lo[kt] = jnp.where(key_hi[kt] == thr_hi, key_lo[kt], jnp.int16(-HALF_BIAS))
        return carry

    lax.fori_loop(0, n_tiles, mask_low, 0)
    lo_u = kth_largest_half(key_lo, topk - n_hi_gt)
    lo_u = jnp.where(jnp.logical_and(hi_u == 0, lo_u == 0), 1, lo_u)
    thr_lo = (lo_u - HALF_BIAS).astype(jnp.int16)

    def tie_band(kt, low_pred):
        return jnp.where(key_hi[kt] == thr_hi, jnp.where(low_pred(key_lo[kt]), one16, zero16), zero16)

    n_gt = n_hi_gt + count(lambda kt: tie_band(kt, lambda lo: lo > thr_lo))
    n_ge = n_hi_gt + count(lambda kt: tie_band(kt, lambda lo: lo >= thr_lo))
    need = (topk - n_gt).astype(f32)
    surplus = jnp.max(jnp.where(n_ge > topk, 1, 0))

    q_pairs = []
    for pi in range(DSA_HEADS // 2):
        pair = q_ref[0, :, pi * LANES:(pi + 1) * LANES]
        zero = jnp.zeros_like(pair)
        q_pairs.append(jnp.concatenate([jnp.where(lower_half, pair, zero), jnp.where(lower_half, zero, pair)], axis=0))

    def attend(exact_ties):
        m_s[...] = jnp.full(m_s.shape, M_FLOOR, f32)
        l_s[...] = jnp.zeros(l_s.shape, f32)
        acc_s[...] = jnp.zeros(acc_s.shape, f32)
        tc_w = min(tk, TIE_CHUNK)
        if exact_ties:
            tr = lax.broadcasted_iota(jnp.int32, (tc_w, tc_w), 0)
            tc = lax.broadcasted_iota(jnp.int32, (tc_w, tc_w), 1)
            prefix = jnp.where(tr <= tc, 1.0, 0.0).astype(MXU_DTYPE)

        def body(kt, taken):
            if exact_ties:
                hi = key_hi[kt].astype(jnp.int32)
                lo = key_lo[kt].astype(jnp.int32)
                t_hi = thr_hi.astype(jnp.int32)
                t_lo = thr_lo.astype(jnp.int32)
                in_band = hi == t_hi
                above = jnp.where(hi > t_hi, 1.0, jnp.where(in_band, jnp.where(lo > t_lo, 1.0, 0.0), 0.0))
                tie = jnp.where(in_band, jnp.where(lo == t_lo, 1.0, 0.0), 0.0)
                parts = []
                for c0 in range(0, tk, tc_w):
                    tie_c = tie[:, c0:c0 + tc_w]
                    rank = taken + _mm(tie_c, prefix)
                    parts.append(above[:, c0:c0 + tc_w] + jnp.where(rank <= need, tie_c, 0.0))
                    taken = rank[:, tc_w - 1:tc_w]
                sel = parts[0] if len(parts) == 1 else jnp.concatenate(parts, axis=1)
            else:
                sel = tie_band(kt, lambda lo: lo >= thr_lo)
                sel = jnp.where(key_hi[kt] > thr_hi, one16, sel).astype(f32)
            bias = (1.0 - sel) * NEG_BIG
            bias2 = jnp.concatenate([bias, bias], axis=0)
            start = pl.multiple_of(kt * tk, tk)
            for pi in range(DSA_HEADS // 2):
                pr = slice(pi * LANES, (pi + 1) * LANES)
                logits = _mm_nt(q_pairs[pi], k_ref[0, pl.ds(start, tk), pr]) + bias2
                m_old = m_s[pi]
                m_new = jnp.maximum(m_old, jnp.max(logits, axis=1, keepdims=True))
                alpha = jnp.exp2(m_old - m_new)
                p = jnp.exp2(logits - m_new)
                l_s[pi] = alpha * l_s[pi] + jnp.sum(p, axis=1, keepdims=True)
                acc_s[pi] = alpha * acc_s[pi] + _mm(p, v_ref[0, pl.ds(start, tk), pr])
                m_s[pi] = m_new
            return taken

        lax.fori_loop(0, n_tiles, body, jnp.zeros((qb, 1), f32))
        for pi in range(DSA_HEADS // 2):
            out = acc_s[pi] / l_s[pi]
            o_ref[0, :, pi * LANES:(pi + 1) * LANES] = jnp.where(lower_half, out[:qb], out[qb:]).astype(o_ref.dtype)

    @pl.when(surplus == 0)
    def _():
        attend(False)

    @pl.when(surplus != 0)
    def _():
        attend(True)


def _dsa_call(qb_arr, qib_arr, sm, k_all, v_all, ki_all, pos0, n_keys, qb, tk):
    b, t, _ = qb_arr.shape
    lp = k_all.shape[1]
    assert lp % tk == 0 and t % qb == 0
    topk = min(TOPK_MAX, n_keys // 4)
    kern = functools.partial(_dsa_kernel, qb=qb, tk=tk, pos0=pos0, n_keys=n_keys, topk=topk)
    resident = dict(pipeline_mode=pl.Buffered(1))
    return pl.pallas_call(
        kern,
        out_shape=jax.ShapeDtypeStruct((b, t, DSA_W), MXU_DTYPE),
        grid=(b, t // qb),
        in_specs=[pl.BlockSpec((1, qb, DSA_W), lambda i, j: (i, j, 0)),
                  pl.BlockSpec((1, qb, IDX_W), lambda i, j: (i, j, 0)),
                  pl.BlockSpec((1, qb, LANES), lambda i, j: (i, j, 0)),
                  pl.BlockSpec((1, lp, DSA_W), lambda i, j: (i, 0, 0), **resident),
                  pl.BlockSpec((1, lp, DSA_W), lambda i, j: (i, 0, 0), **resident),
                  pl.BlockSpec((1, lp, LANES), lambda i, j: (i, 0, 0), **resident)],
        out_specs=pl.BlockSpec((1, qb, DSA_W), lambda i, j: (i, j, 0)),
        scratch_shapes=[pltpu.VMEM((lp // tk, qb, tk), jnp.int16),
                        pltpu.VMEM((lp // tk, qb, tk), jnp.int16),
                        pltpu.VMEM((DSA_HEADS // 2, 2 * qb, 1), f32),
                        pltpu.VMEM((DSA_HEADS // 2, 2 * qb, 1), f32),
                        pltpu.VMEM((DSA_HEADS // 2, 2 * qb, LANES), f32)],
        compiler_params=_params("parallel", "arbitrary"),
        name="dsa",
    )(qb_arr, qib_arr, sm, k_all, v_all, ki_all)


def _outproj_kernel(h_ref, og_ref, od_ref, w_ref, g_ref, b_ref, o_ref, *, alpha):
    mix = (jnp.dot(og_ref[...], w_ref[0:GDN_VW, :], preferred_element_type=f32)
           + jnp.dot(od_ref[...], w_ref[GDN_VW:, :], preferred_element_type=f32))
    o_ref[...] = _layer_norm(alpha * h_ref[...] + mix, g_ref[...], b_ref[...])


def _outproj_call(h, og, od, w, g, b, alpha):
    n, d = h.shape
    tm = _row_tile(n, 512)
    return pl.pallas_call(
        functools.partial(_outproj_kernel, alpha=alpha),
        out_shape=jax.ShapeDtypeStruct((n, d), f32),
        grid=(n // tm,),
        in_specs=[pl.BlockSpec((tm, d), lambda i: (i, 0)),
                  pl.BlockSpec((tm, GDN_VW), lambda i: (i, 0)),
                  pl.BlockSpec((tm, DSA_W), lambda i: (i, 0)),
                  pl.BlockSpec((GDN_VW + DSA_W, d), lambda i: (0, 0)),
                  pl.BlockSpec((1, d), lambda i: (0, 0)),
                  pl.BlockSpec((1, d), lambda i: (0, 0))],
        out_specs=pl.BlockSpec((tm, d), lambda i: (i, 0)),
        compiler_params=_params("parallel"),
        name="outproj_ln",
    )(h, og, od, w, g.reshape(1, d), b.reshape(1, d))


def _ffn_kernel(x_ref, wg_ref, wu_ref, wd_ref, g_ref, b_ref, o_ref, acc, *, alpha):
    j = pl.program_id(1)

    @pl.when(j == 0)
    def _():
        acc[...] = jnp.zeros(acc.shape, f32)

    xb = x_ref[...].astype(MXU_DTYPE)
    gate = jnp.dot(xb, wg_ref[...], preferred_element_type=f32)
    up = jnp.dot(xb, wu_ref[...], preferred_element_type=f32)
    act = gate * _sigmoid(gate) * up
    acc[...] += jnp.dot(act.astype(MXU_DTYPE), wd_ref[...], preferred_element_type=f32)

    @pl.when(j == pl.num_programs(1) - 1)
    def _():
        o_ref[...] = _layer_norm(alpha * x_ref[...] + acc[...], g_ref[...], b_ref[...])


def _ffn_call(x, wg, wu, wd, g, b, alpha):
    n, d = x.shape
    f = wg.shape[1]
    tm = _row_tile(n, 1024)
    tf = _row_tile(f, 512)
    return pl.pallas_call(
        functools.partial(_ffn_kernel, alpha=alpha),
        out_shape=jax.ShapeDtypeStruct((n, d), f32),
        grid=(n // tm, f // tf),
        in_specs=[pl.BlockSpec((tm, d), lambda i, j: (i, 0)),
                  pl.BlockSpec((d, tf), lambda i, j: (0, j)),
                  pl.BlockSpec((d, tf), lambda i, j: (0, j)),
                  pl.BlockSpec((tf, d), lambda i, j: (j, 0)),
                  pl.BlockSpec((1, d), lambda i, j: (0, 0)),
                  pl.BlockSpec((1, d), lambda i, j: (0, 0))],
        out_specs=pl.BlockSpec((tm, d), lambda i, j: (i, 0)),
        scratch_shapes=[pltpu.VMEM((tm, d), f32)],
        compiler_params=_params("parallel", "arbitrary"),
        name="ffn_ln",
    )(x, wg, wu, wd, g.reshape(1, d), b.reshape(1, d))


def _moe_plan_kernel(x_ref, wr_ref, plan_o, plan_t_o, xb_o, *, n_experts):
    tm = x_ref.shape[0]
    lane = lax.broadcasted_iota(jnp.int32, (tm, LANES), 1)
    x = x_ref[...]
    xb_o[...] = x.astype(xb_o.dtype)
    logits = _mm_f32(x, wr_ref[...])
    logits = jnp.where(lane < n_experts, logits, -jnp.inf)
    m1 = jnp.max(logits, axis=1, keepdims=True)
    i1 = jnp.min(jnp.where(logits == m1, lane, LANES), axis=1, keepdims=True)
    rest = jnp.where(lane == i1, -jnp.inf, logits)
    m2 = jnp.max(rest, axis=1, keepdims=True)
    i2 = jnp.min(jnp.where(rest == m2, lane, LANES), axis=1, keepdims=True)
    e2 = jnp.exp(m2 - m1)
    w1 = 1.0 / (1.0 + e2)
    gates = jnp.where(lane == i1, w1, 0.0) + jnp.where(lane == i2, e2 * w1, 0.0)
    chosen = jnp.where(lane == i1, 1.0, 0.0) + jnp.where(lane == i2, 1.0, 0.0)
    blk = min(tm, MOE_PREFIX_BLOCK)
    r = lax.broadcasted_iota(jnp.int32, (blk, blk), 0)
    c = lax.broadcasted_iota(jnp.int32, (blk, blk), 1)
    tri = jnp.where(r >= c, 1.0, 0.0).astype(MXU_DTYPE)
    carry = jnp.zeros((1, LANES), f32)
    slots = []
    for b0 in range(0, tm, blk):
        m = chosen[b0:b0 + blk]
        inclusive = jnp.dot(tri, m.astype(MXU_DTYPE), preferred_element_type=f32) + carry
        slots.append(inclusive - m)
        carry = inclusive[blk - 1:blk, :]
    slot = jnp.concatenate(slots, axis=0) if len(slots) > 1 else slots[0]
    plan = jnp.where(lane < n_experts, gates,
                     jnp.where(lane < 2 * n_experts, pltpu.roll(slot, n_experts, 1),
                               jnp.where(lane < 3 * n_experts, pltpu.roll(chosen, 2 * n_experts, 1), 0.0)))
    plan_o[...] = plan
    plan_t_o[...] = plan.T[0:MOE_PLAN_ROWS, :]


def _moe_expert_kernel(xb_ref, plan_ref, plan_t_ref, wg_ref, wu_ref, wd_ref, y_ref,
                       xc, acc_c, *, n_experts, slot_block):
    e = pl.program_id(1)
    j = pl.program_id(2)
    tm = xb_ref.shape[0]
    cb = slot_block
    slot_row = plan_t_ref[pl.ds(e + n_experts, 1), :]
    chosen_row = plan_t_ref[pl.ds(e + 2 * n_experts, 1), :]
    count = jnp.max(jnp.where(chosen_row > 0.0, slot_row + 1.0, 0.0)).astype(jnp.int32)
    n_blocks = (count + cb - 1) // cb
    dot = functools.partial(jnp.dot, preferred_element_type=f32)

    def one_hot(sb):
        slot_id = (lax.broadcasted_iota(jnp.int32, (cb, 1), 0) + sb * cb).astype(f32)
        hit = jnp.where(chosen_row > 0.0, jnp.where(slot_row == slot_id, 1.0, 0.0), 0.0)
        return hit.astype(MXU_DTYPE)

    def rows_of(sb):
        return pl.ds(pl.multiple_of(sb * cb, cb), cb)

    @pl.when(jnp.logical_and(e == 0, j == 0))
    def _():
        y_ref[...] = jnp.zeros(y_ref.shape, f32)

    @pl.when(j == 0)
    def _():
        def gather(sb, carry):
            xc[rows_of(sb), :] = dot(one_hot(sb), xb_ref[...]).astype(xc.dtype)
            acc_c[rows_of(sb), :] = jnp.zeros((cb, acc_c.shape[1]), f32)
            return carry
        lax.fori_loop(0, n_blocks, gather, 0)

    def expert(sb, carry):
        xs = xc[rows_of(sb), :]
        gate = dot(xs, wg_ref[0])
        up = dot(xs, wu_ref[0])
        act = gate * _sigmoid(gate) * up
        acc_c[rows_of(sb), :] += dot(act.astype(MXU_DTYPE), wd_ref[0])
        return carry

    lax.fori_loop(0, n_blocks, expert, 0)

    @pl.when(j == pl.num_programs(2) - 1)
    def _():
        lane = lax.broadcasted_iota(jnp.int32, (tm, LANES), 1)
        gate_col = jnp.sum(jnp.where(lane == e, plan_ref[...], 0.0), axis=1, keepdims=True)

        def scatter(sb, carry):
            back = lax.dot_general(one_hot(sb), acc_c[rows_of(sb), :].astype(MXU_DTYPE),
                                   (((0,), (0,)), ((), ())), preferred_element_type=f32)
            y_ref[...] += gate_col * back
            return carry
        lax.fori_loop(0, n_blocks, scatter, 0)


def _residual_ln_kernel(x_ref, y_ref, g_ref, b_ref, o_ref, *, alpha):
    o_ref[...] = _layer_norm(alpha * x_ref[...] + y_ref[...], g_ref[...], b_ref[...])


def _moe_call(x, wr, wg, wu, wd, g, b, alpha):
    n, d = x.shape
    n_experts, _, f = wg.shape
    assert 3 * n_experts <= MOE_PLAN_ROWS
    tm = _row_tile(n, MOE_TOKEN_TILE)
    tf = _row_tile(f, 512)
    cb = min(MOE_SLOT_BLOCK, tm)
    cap = -(-tm // cb) * cb
    plan, plan_t, xb = pl.pallas_call(
        functools.partial(_moe_plan_kernel, n_experts=n_experts),
        out_shape=[jax.ShapeDtypeStruct((n, LANES), f32),
                   jax.ShapeDtypeStruct((MOE_PLAN_ROWS, n), f32),
                   jax.ShapeDtypeStruct((n, d), MXU_DTYPE)],
        grid=(n // tm,),
        in_specs=[pl.BlockSpec((tm, d), lambda i: (i, 0)),
                  pl.BlockSpec((d, LANES), lambda i: (0, 0))],
        out_specs=[pl.BlockSpec((tm, LANES), lambda i: (i, 0)),
                   pl.BlockSpec((MOE_PLAN_ROWS, tm), lambda i: (0, i)),
                   pl.BlockSpec((tm, d), lambda i: (i, 0))],
        compiler_params=_params("parallel"),
        name="moe_plan",
    )(x, wr)
    y = pl.pallas_call(
        functools.partial(_moe_expert_kernel, n_experts=n_experts, slot_block=cb),
        out_shape=jax.ShapeDtypeStruct((n, d), f32),
        grid=(n // tm, n_experts, f // tf),
        in_specs=[pl.BlockSpec((tm, d), lambda i, e, j: (i, 0)),
                  pl.BlockSpec((tm, LANES), lambda i, e, j: (i, 0)),
                  pl.BlockSpec((MOE_PLAN_ROWS, tm), lambda i, e, j: (0, i)),
                  pl.BlockSpec((1, d, tf), lambda i, e, j: (e, 0, j)),
                  pl.BlockSpec((1, d, tf), lambda i, e, j: (e, 0, j)),
                  pl.BlockSpec((1, tf, d), lambda i, e, j: (e, j, 0))],
        out_specs=pl.BlockSpec((tm, d), lambda i, e, j: (i, 0)),
        scratch_shapes=[pltpu.VMEM((cap, d), MXU_DTYPE),
                        pltpu.VMEM((cap, d), f32)],
        compiler_params=_params("parallel", "arbitrary", "arbitrary"),
        name="moe_experts",
    )(xb, plan, plan_t, wg, wu, wd)
    tr = _row_tile(n, 1024)
    return pl.pallas_call(
        functools.partial(_residual_ln_kernel, alpha=alpha),
        out_shape=jax.ShapeDtypeStruct((n, d), f32),
        grid=(n // tr,),
        in_specs=[pl.BlockSpec((tr, d), lambda i: (i, 0)),
                  pl.BlockSpec((tr, d), lambda i: (i, 0)),
                  pl.BlockSpec((1, d), lambda i: (0, 0)),
                  pl.BlockSpec((1, d), lambda i: (0, 0))],
        out_specs=pl.BlockSpec((tr, d), lambda i: (i, 0)),
        compiler_params=_params("parallel"),
        name="moe_residual_ln",
    )(x, y, g.reshape(1, d), b.reshape(1, d))


def _pack_w_in(w_in):
    o_z = GDN_CONV_CH
    o_b = o_z + GDN_VW
    o_a = o_b + GDN_HEADS
    o_qd = o_a + GDN_HEADS
    o_kd = o_qd + DSA_W
    o_vd = o_kd + DSA_W
    o_qi = o_vd + DSA_W
    o_ki = o_qi + IDX_W
    o_wi = o_ki + IDX_DIM
    assert w_in.shape[-1] == o_wi + IDX_HEADS
    sl = lambda a, n: w_in[..., a:a + n]
    pad = jnp.zeros(w_in.shape[:-1] + (LANES - 2 * GDN_HEADS - IDX_HEADS,), w_in.dtype)
    packed = jnp.concatenate([
        sl(0, GDN_CONV_CH), sl(o_z, GDN_VW), sl(o_qd, DSA_W), sl(o_kd, DSA_W), sl(o_vd, DSA_W),
        sl(o_qi, IDX_W), sl(o_ki, IDX_DIM), sl(o_ki, IDX_DIM),
        sl(o_b, GDN_HEADS), sl(o_a, GDN_HEADS), sl(o_wi, IDX_HEADS), pad], axis=-1)
    assert packed.shape[-1] == C_END
    return packed.astype(MXU_DTYPE)


def _rope_tables(pos):
    half = DSA_DH // 8
    inv_freq = ROPE_THETA ** (-jnp.arange(half, dtype=f32) / half)
    ang = pos.astype(f32)[:, None] * inv_freq[None, :]
    cos, sin = jnp.cos(ang), jnp.sin(ang)
    t = pos.shape[0]
    ones = jnp.ones((t, DSA_DH - 2 * half), f32)
    zeros = jnp.zeros((t, DSA_DH - 2 * half), f32)
    z8 = jnp.zeros((t, half), f32)
    cos_t = jnp.concatenate([cos, cos, ones], -1)
    lo = jnp.concatenate([-sin, z8, zeros], -1)
    hi = jnp.concatenate([z8, sin, zeros], -1)
    return tuple(jnp.tile(a, (1, LANES // DSA_DH)) for a in (cos_t, lo, hi))


def _gdn_params(a_log, dt_bias):
    row = lambda v: jnp.zeros((LANES,), f32).at[SM_DECAY:SM_DECAY + GDN_HEADS].set(v.astype(f32))
    return jnp.stack([row(a_log), row(dt_bias)])


def _trunk(x, pos0, conv_bufs, s0s, past_k, past_v, past_ik, chunk, wts, depth):
    b, t, d = x.shape
    n = b * t
    alpha = (2.0 * depth) ** 0.25
    tabs = _rope_tables(pos0 + jnp.arange(t))
    past = 0 if past_k is None else past_k.shape[2]
    n_keys = past + t
    assert t >= CONV_W - 1
    qb = min(t, DSA_Q_BLOCK)
    tk = DSA_KEY_TILE
    lp = -(-n_keys // tk) * tk

    def with_past(new, old, l):
        parts = [] if old is None else [old[l].reshape(b, past, -1).astype(MXU_DTYPE)]
        parts.append(new.reshape(b, t, -1))
        if lp > n_keys:
            parts.append(jnp.zeros((b, lp - n_keys, new.shape[-1]), MXU_DTYPE))
        return parts[0] if len(parts) == 1 else jnp.concatenate(parts, axis=1)

    h = _ln_call(x.reshape(n, d), wts["ln_in_g"], wts["ln_in_b"])
    new_k, new_v, new_ik, new_conv, new_s = [], [], [], [], []
    for l in range(depth):
        (qkv, z, k_f, v_f, ki_f, q_b, k_b, v_b, qi_b, ki_b, sm) = _inproj_call(h, wts["w_in"][l], tabs, t)
        qkv3 = qkv.reshape(b, t, GDN_CONV_CH)
        buf8 = jnp.concatenate([jnp.zeros((b, SUBLANES - (CONV_W - 1), GDN_CONV_CH), f32), conv_bufs[l]], axis=1)
        o_g, s_new = _gdn_call(qkv3, z.reshape(b, t, GDN_VW), sm.reshape(b, t, LANES), wts["conv_w"][l],
                               buf8, s0s[l], _gdn_params(wts["gdn_a_log"][l], wts["gdn_dt_bias"][l]),
                               wts["gdn_norm_w"][l].reshape(1, GDN_DV), chunk)
        ki_dup = None if past_ik is None else jnp.concatenate([past_ik, past_ik], axis=-1)
        o_d = _dsa_call(q_b.reshape(b, t, DSA_W), qi_b.reshape(b, t, IDX_W), sm.reshape(b, t, LANES),
                        with_past(k_b, past_k, l), with_past(v_b, past_v, l), with_past(ki_b, ki_dup, l),
                        past, n_keys, qb, tk)
        h = _outproj_call(h, o_g.reshape(n, GDN_VW), o_d.reshape(n, DSA_W), wts["w_out"][l],
                          wts["ln1_g"][l], wts["ln1_b"][l], alpha)
        if l % 2 == 0:
            i = l // 2
            h = _ffn_call(h, wts["ffn_wg"][i], wts["ffn_wu"][i], wts["ffn_wd"][i],
                          wts["ln2_g"][l], wts["ln2_b"][l], alpha)
        else:
            i = l // 2
            h = _moe_call(h, wts["moe_router"][i], wts["moe_wg"][i], wts["moe_wu"][i], wts["moe_wd"][i],
                          wts["ln2_g"][l], wts["ln2_b"][l], alpha)
        new_k.append(k_f.reshape(b, t, DSA_HEADS, DSA_DH))
        new_v.append(v_f.reshape(b, t, DSA_HEADS, DSA_DH))
        new_ik.append(ki_f.reshape(b, t, IDX_DIM))
        new_conv.append(qkv3[:, t - (CONV_W - 1):])
        new_s.append(s_new)
    return (h.reshape(b, t, d), jnp.stack(new_k), jnp.stack(new_v), jnp.stack(new_ik),
            jnp.stack(new_s), jnp.stack(new_conv))


def kernel(x_prompt, x_sample, cache_k, cache_v, cache_idx_k, state_gdn, state_conv, ln_in_g, ln_in_b, w_in, conv_w, gdn_a_log, gdn_dt_bias, gdn_norm_w, w_out, ln1_g, ln1_b, ln2_g, ln2_b, ffn_wg, ffn_wu, ffn_wd, moe_router, moe_wg, moe_wu, moe_wd):
    depth = w_in.shape[0]
    n_experts = moe_router.shape[-1]
    bp, tp, d = x_prompt.shape
    cast = lambda a: a.astype(MXU_DTYPE)
    router = jnp.concatenate(
        [moe_router.astype(f32), jnp.zeros(moe_router.shape[:-1] + (LANES - n_experts,), f32)], axis=-1)
    wts = dict(ln_in_g=ln_in_g, ln_in_b=ln_in_b, w_in=_pack_w_in(w_in), conv_w=conv_w,
               gdn_a_log=gdn_a_log, gdn_dt_bias=gdn_dt_bias, gdn_norm_w=gdn_norm_w, w_out=cast(w_out),
               ln1_g=ln1_g, ln1_b=ln1_b, ln2_g=ln2_g, ln2_b=ln2_b,
               ffn_wg=cast(ffn_wg), ffn_wu=cast(ffn_wu), ffn_wd=cast(ffn_wd),
               moe_router=router, moe_wg=cast(moe_wg), moe_wu=cast(moe_wu), moe_wd=cast(moe_wd))
    zero_conv = jnp.zeros((depth, bp, CONV_W - 1, GDN_CONV_CH), f32)
    zero_s = jnp.zeros((depth, bp, GDN_HEADS, GDN_DK, GDN_DV), f32)
    y_p, k_p, v_p, ik_p, s_p, conv_p = _trunk(
        x_prompt, 0, zero_conv, zero_s, None, None, None, CHUNK, wts, depth)
    ts = x_sample.shape[1]
    past = cache_k.shape[2]
    y_s, k_s, v_s, ik_s, s_s, conv_s = _trunk(
        x_sample, past, state_conv, state_gdn, cache_k, cache_v, cache_idx_k, ts, wts, depth)
    return (y_p, y_s, k_p, v_p, ik_p, s_p, conv_p, k_s, v_s, ik_s, s_s, conv_s)
```

```python
import functools

import jax
import jax.numpy as jnp
from jax import lax
from jax.experimental import pallas as pl
from jax.experimental.pallas import tpu as pltpu

CHUNK = 64
CONV_W = 4
GDN_HEADS = 4
GDN_DK = 128
GDN_DV = 128
DSA_HEADS = 8
DSA_DH = 64
IDX_HEADS = 4
IDX_DIM = 64
TOPK_MAX = 256
ROPE_THETA = 500000.0
TOP_K_EXPERTS = 2
LN_EPS = 1e-5
RMS_EPS = 1e-6
L2_EPS = 1e-6

GDN_QK = GDN_HEADS * GDN_DK
GDN_VW = GDN_HEADS * GDN_DV
GDN_CONV_CH = 2 * GDN_QK + GDN_VW
DSA_W = DSA_HEADS * DSA_DH
IDX_W = IDX_HEADS * IDX_DIM

LANES = 128
SUBLANES = 8
VMEM_LIMIT = 56 * 1024 * 1024

DSA_Q_BLOCK = 256
DSA_KEY_TILE = 2048
DSA_TAIL_STEP = 512
GDN_CHUNKS_PER_STEP = 4
MOE_TOKEN_TILE = 2048
MOE_SLOT_BLOCK = 576
MOE_PREFIX_BLOCK = 256
MOE_PLAN_ROWS = 24

MXU_DTYPE = jnp.bfloat16

C_QKV = 0
C_Z = C_QKV + GDN_CONV_CH
C_QD = C_Z + GDN_VW
C_KD = C_QD + DSA_W
C_VD = C_KD + DSA_W
C_QI = C_VD + DSA_W
C_KI = C_QI + IDX_W
C_SM = C_KI + LANES
C_END = C_SM + LANES
SM_BETA = 0
SM_DECAY = GDN_HEADS
SM_WI = 2 * GDN_HEADS

INT_MIN = -2 ** 31
HALF_BIAS = 2 ** 15
TIE_CHUNK = 512
NEG_BIG = -3e38
M_FLOOR = -1e38
LOG2E = 1.4426950408889634

f32 = jnp.float32


def _mm(a, b):
    return jnp.dot(a.astype(MXU_DTYPE), b.astype(MXU_DTYPE), preferred_element_type=f32)


def _mm_nt(a, b):
    return lax.dot_general(a.astype(MXU_DTYPE), b.astype(MXU_DTYPE), (((1,), (1,)), ((), ())),
                           preferred_element_type=f32)


def _mm_tn(a, b):
    return lax.dot_general(a.astype(MXU_DTYPE), b.astype(MXU_DTYPE), (((0,), (0,)), ((), ())),
                           preferred_element_type=f32)


def _mm_f32(a, b):
    return jnp.dot(a, b, precision=lax.Precision.HIGHEST, preferred_element_type=f32)


def _split(x):
    hi = x.astype(MXU_DTYPE)
    lo = (x - hi.astype(f32)).astype(MXU_DTYPE)
    return hi, lo


def _mm_x3(a, b):
    a_hi, a_lo = _split(a)
    b_hi, b_lo = _split(b)
    dot = functools.partial(jnp.dot, preferred_element_type=f32)
    return dot(a_hi, b_hi) + (dot(a_hi, b_lo) + dot(a_lo, b_hi))


def _sigmoid(x):
    return 1.0 / (1.0 + jnp.exp(-x))


def _layer_norm(r, g, b):
    mu = jnp.mean(r, axis=-1, keepdims=True)
    xc = r - mu
    var = jnp.mean(xc * xc, axis=-1, keepdims=True)
    return xc * lax.rsqrt(var + LN_EPS) * g + b


def _params(*sem):
    return pltpu.CompilerParams(dimension_semantics=sem, vmem_limit_bytes=VMEM_LIMIT)


def _row_tile(n, want):
    t = min(n, want)
    assert n % t == 0, (n, t)
    return t


def _ln_kernel(x_ref, g_ref, b_ref, o_ref):
    o_ref[...] = _layer_norm(x_ref[...], g_ref[...], b_ref[...])


def _ln_call(x, g, b):
    n, d = x.shape
    tm = _row_tile(n, 1024)
    return pl.pallas_call(
        _ln_kernel,
        out_shape=jax.ShapeDtypeStruct((n, d), f32),
        grid=(n // tm,),
        in_specs=[pl.BlockSpec((tm, d), lambda i: (i, 0)),
                  pl.BlockSpec((1, d), lambda i: (0, 0)),
                  pl.BlockSpec((1, d), lambda i: (0, 0))],
        out_specs=pl.BlockSpec((tm, d), lambda i: (i, 0)),
        compiler_params=_params("parallel"),
        name="ln_in",
    )(x, g.reshape(1, d), b.reshape(1, d))


def _rope(x, cos_t, sin_lo, sin_hi):
    w = x.shape[1]
    reps = w // LANES
    c = jnp.tile(cos_t, (1, reps))
    lo = jnp.tile(sin_lo, (1, reps))
    hi = jnp.tile(sin_hi, (1, reps))
    half = DSA_DH // 8
    from_above = pltpu.roll(x, w - half, 1)
    from_below = pltpu.roll(x, half, 1)
    return x * c + from_above * lo + from_below * hi


def _inproj_kernel(x_ref, w_ref, cos_ref, lo_ref, hi_ref,
                   qkv_o, z_o, k_o, v_o, ki_o, qb_o, kb_o, vb_o, qib_o, kib_o, sm_o):
    xb = x_ref[...].astype(MXU_DTYPE)
    cos_t, lo, hi = cos_ref[...], lo_ref[...], hi_ref[...]

    def proj(c0, c1):
        return jnp.dot(xb, w_ref[:, c0:c1], preferred_element_type=f32)

    qkv_o[...] = proj(C_QKV, C_Z)
    z_o[...] = proj(C_Z, C_QD)
    q = _rope(proj(C_QD, C_KD), cos_t, lo, hi) * (DSA_DH ** -0.5 * LOG2E)
    qb_o[...] = q.astype(qb_o.dtype)
    k = _rope(proj(C_KD, C_VD), cos_t, lo, hi)
    k_o[...] = k
    kb_o[...] = k.astype(kb_o.dtype)
    v = proj(C_VD, C_QI)
    v_o[...] = v
    vb_o[...] = v.astype(vb_o.dtype)
    qi = _rope(proj(C_QI, C_KI), cos_t, lo, hi) * (IDX_DIM ** -0.5)
    qib_o[...] = qi.astype(qib_o.dtype)
    ki = _rope(proj(C_KI, C_SM), cos_t, lo, hi)
    ki_o[...] = ki[:, :IDX_DIM]
    kib_o[...] = ki.astype(kib_o.dtype)
    sm_o[...] = proj(C_SM, C_END)


def _inproj_call(h, w_packed, tabs, seq_len):
    n, d = h.shape
    tm = _row_tile(n, 512)
    cos_t, lo, hi = tabs
    if tm > seq_len:
        assert tm % seq_len == 0
        cos_t, lo, hi = (jnp.tile(a, (tm // seq_len, 1)) for a in (cos_t, lo, hi))
        nblk = 1
    else:
        assert seq_len % tm == 0
        nblk = seq_len // tm
    tab_spec = pl.BlockSpec((tm, LANES), lambda i: (i % nblk, 0))

    def rows(w):
        return pl.BlockSpec((tm, w), lambda i: (i, 0))

    widths = [(GDN_CONV_CH, f32), (GDN_VW, f32), (DSA_W, f32), (DSA_W, f32), (IDX_DIM, f32),
              (DSA_W, MXU_DTYPE), (DSA_W, MXU_DTYPE), (DSA_W, MXU_DTYPE), (IDX_W, MXU_DTYPE),
              (LANES, MXU_DTYPE), (LANES, f32)]
    return pl.pallas_call(
        _inproj_kernel,
        out_shape=[jax.ShapeDtypeStruct((n, w), dt) for w, dt in widths],
        grid=(n // tm,),
        in_specs=[rows(d),
                  pl.BlockSpec((d, C_END), lambda i: (0, 0), pipeline_mode=pl.Buffered(1)),
                  tab_spec, tab_spec, tab_spec],
        out_specs=[rows(w) for w, _ in widths],
        compiler_params=_params("parallel"),
        name="inproj",
    )(h, w_packed, cos_t, lo, hi)


def _wide_to_blocks(w, chunk):
    size = w.shape[1]
    r = lax.broadcasted_iota(jnp.int32, (size, size), 0)
    c = lax.broadcasted_iota(jnp.int32, (size, size), 1)
    tiled = jnp.concatenate([w] * (size // chunk), axis=0)
    return jnp.where(r // chunk == c // chunk, tiled, jnp.zeros_like(tiled))


def _blocks_to_wide(sq, chunk):
    size = sq.shape[1]
    blk = lax.broadcasted_iota(jnp.int32, (chunk, size), 1) // chunk
    out = jnp.zeros((chunk, size), sq.dtype)
    for h in range(size // chunk):
        out = jnp.where(blk == h, sq[h * chunk:(h + 1) * chunk, :], out)
    return out


def _wide_matmul(x, p, chunk):
    x_hi, x_lo = _split(x)
    p_hi, p_lo = _split(p)
    dot = functools.partial(jnp.dot, preferred_element_type=f32)
    both = dot(jnp.concatenate([x_hi, x_lo], axis=0), _wide_to_blocks(p_hi, chunk))
    return both[:chunk] + both[chunk:] + dot(x_hi, _wide_to_blocks(p_lo, chunk))


def _unit_lower_inverse_wide(a, chunk):
    size = a.shape[1]
    r = lax.broadcasted_iota(jnp.int32, (chunk, size), 0)
    c = lax.broadcasted_iota(jnp.int32, (chunk, size), 1) % chunk

    def same_block(n):
        return (r // n) == (c // n)

    base = SUBLANES
    a0 = jnp.where(same_block(base), a, 0.0)
    p = _wide_matmul(a0, a0, chunk)
    x = jnp.where(r == c, 1.0, 0.0) - a0
    x = x + _wide_matmul(x, p, chunk)
    p = _wide_matmul(p, p, chunk)
    x = x + _wide_matmul(x, p, chunk)
    n = base
    while n < chunk:
        off = jnp.where(same_block(2 * n), jnp.where(same_block(n), 0.0, a), 0.0)
        x = x - _wide_matmul(_wide_matmul(x, off, chunk), x, chunk)
        n *= 2
    return x


def _chunk_cumsum(g, chunk):
    rows = g.shape[0]
    r = lax.broadcasted_iota(jnp.int32, (rows, rows), 0)
    c = lax.broadcasted_iota(jnp.int32, (rows, rows), 1)
    tri = jnp.where(jnp.logical_and(r // chunk == c // chunk, r >= c), 1.0, 0.0).astype(MXU_DTYPE)
    dot = functools.partial(jnp.dot, preferred_element_type=f32)
    g1 = g.astype(MXU_DTYPE)
    rem = g - g1.astype(f32)
    g2 = rem.astype(MXU_DTYPE)
    g3 = (rem - g2.astype(f32)).astype(MXU_DTYPE)
    return dot(tri, g1) + (dot(tri, g2) + dot(tri, g3))


def _gdn_kernel(qkv_ref, z_ref, sm_ref, cw_ref, cb_ref, s0_ref, par_ref, nw_ref,
                o_ref, s_ref, ubuf, *, chunk, group):
    step = pl.program_id(1)
    rows = chunk * group

    @pl.when(step == 0)
    def _():
        ubuf[0:SUBLANES, :] = cb_ref[0]
        s_ref[0] = s0_ref[0]

    ubuf[SUBLANES:SUBLANES + rows, :] = qkv_ref[0]
    y = ubuf[SUBLANES - 3:SUBLANES - 3 + rows, :] * cw_ref[0:1, :]
    for i in range(1, CONV_W):
        y = y + ubuf[SUBLANES - 3 + i:SUBLANES - 3 + i + rows, :] * cw_ref[i:i + 1, :]
    ubuf[0:SUBLANES, :] = ubuf[rows:rows + SUBLANES, :]
    y = y * _sigmoid(y)

    sm = sm_ref[0]
    beta_all = _sigmoid(sm)
    xg = sm + par_ref[1:2, :]
    softplus = jnp.maximum(xg, 0.0) + jnp.log1p(jnp.exp(-jnp.abs(xg)))
    g_all = -jnp.exp(par_ref[0:1, :]) * softplus
    gc_all = _chunk_cumsum(g_all, chunk)
    gc_rows = gc_all.T

    heads = GDN_HEADS
    size = heads * chunk
    w_row = lax.broadcasted_iota(jnp.int32, (chunk, size), 0)
    w_lane = lax.broadcasted_iota(jnp.int32, (chunk, size), 1)
    w_col = w_lane % chunk
    w_head = w_lane // chunk
    causal_w = w_row >= w_col
    strict_w = w_row > w_col
    z = z_ref[0]
    nw = nw_ref[...]
    dot = functools.partial(jnp.dot, preferred_element_type=f32)

    def stack(a, rs, col0, width):
        return jnp.concatenate([a[rs, col0 + h * width:col0 + (h + 1) * width] for h in range(heads)], axis=0)

    def column_stack(a, rs, lane0):
        return jnp.concatenate([a[rs, lane0 + h:lane0 + h + 1] for h in range(heads)], axis=0)

    def column_wide(a, rs, lane0):
        out = jnp.zeros((chunk, size), f32)
        for h in range(heads):
            out = jnp.where(w_head == h, a[rs, lane0 + h:lane0 + h + 1], out)
        return out

    local = []
    for g in range(group):
        rs = slice(g * chunk, (g + 1) * chunk)
        last = (g + 1) * chunk - 1
        q = stack(y, rs, 0, GDN_DK)
        k = stack(y, rs, GDN_QK, GDN_DK)
        v = stack(y, rs, 2 * GDN_QK, GDN_DV)
        q = q * lax.rsqrt(jnp.sum(q * q, -1, keepdims=True) + L2_EPS) * (GDN_DK ** -0.5)
        k = k * lax.rsqrt(jnp.sum(k * k, -1, keepdims=True) + L2_EPS)
        beta_s = column_stack(beta_all, rs, SM_BETA)
        gc_s = column_stack(gc_all, rs, SM_DECAY)
        g_last = [gc_all[last:last + 1, SM_DECAY + h:SM_DECAY + h + 1] for h in range(heads)]
        g_last_s = jnp.concatenate([jnp.broadcast_to(gl, (chunk, 1)) for gl in g_last], axis=0)
        gc_row_w = jnp.concatenate([gc_rows[SM_DECAY + h:SM_DECAY + h + 1, rs] for h in range(heads)], axis=1)
        decay_w = jnp.exp(jnp.where(causal_w, column_wide(gc_all, rs, SM_DECAY) - gc_row_w, -jnp.inf))
        kk_w = _blocks_to_wide(_mm_nt(k, k), chunk)
        a_w = jnp.where(strict_w, column_wide(beta_all, rs, SM_BETA) * kk_w * decay_w, 0.0)
        t_w = _unit_lower_inverse_wide(a_w, chunk)
        rhs = jnp.concatenate([v * beta_s, k * (beta_s * jnp.exp(gc_s))], axis=-1)
        t_hi, t_lo = _split(t_w)
        r_hi, r_lo = _split(rhs)
        tb_hi = _wide_to_blocks(t_hi, chunk)
        both = dot(jnp.concatenate([tb_hi, _wide_to_blocks(t_lo, chunk)], axis=0), r_hi)
        sol = both[:size] + both[size:] + dot(tb_hi, r_lo)
        local.append(dict(u0=sol[:, :GDN_DV], wk=sol[:, GDN_DV:],
                          qk=_mm_nt(q, k) * _wide_to_blocks(decay_w, chunk),
                          q_dec=q * jnp.exp(gc_s), k_dec=k * jnp.exp(g_last_s - gc_s), g_last=g_last))

    s = [s_ref[0, h] for h in range(heads)]
    for g in range(group):
        rs = slice(g * chunk, (g + 1) * chunk)
        lc = local[g]
        u_parts, o_parts = [], []
        for h in range(heads):
            hs = slice(h * chunk, (h + 1) * chunk)
            both = _mm(jnp.concatenate([lc["wk"][hs], lc["q_dec"][hs]], axis=0), s[h])
            u_parts.append(lc["u0"][hs] - both[:chunk])
            o_parts.append(both[chunk:])
        u = jnp.concatenate(u_parts, axis=0)
        o = jnp.concatenate(o_parts, axis=0) + _mm(lc["qk"], u)
        for h in range(heads):
            hs = slice(h * chunk, (h + 1) * chunk)
            s[h] = s[h] * jnp.exp(lc["g_last"][h]) + _mm_tn(lc["k_dec"][hs], u[hs])
        z_s = stack(z, rs, 0, GDN_DV)
        o = o * lax.rsqrt(jnp.mean(o * o, -1, keepdims=True) + RMS_EPS) * nw * (z_s * _sigmoid(z_s))
        for h in range(heads):
            o_ref[0, rs, h * GDN_DV:(h + 1) * GDN_DV] = o[h * chunk:(h + 1) * chunk].astype(o_ref.dtype)
    for h in range(heads):
        s_ref[0, h] = s[h]


def _gdn_call(qkv, z, sm, conv_w, conv_buf8, s0, par, norm_w, chunk):
    b, t, _ = qkv.shape
    group = min(GDN_CHUNKS_PER_STEP, t // chunk)
    rows = chunk * group
    assert t % rows == 0
    return pl.pallas_call(
        functools.partial(_gdn_kernel, chunk=chunk, group=group),
        out_shape=[jax.ShapeDtypeStruct((b, t, GDN_VW), MXU_DTYPE),
                   jax.ShapeDtypeStruct((b, GDN_HEADS, GDN_DK, GDN_DV), f32)],
        grid=(b, t // rows),
        in_specs=[pl.BlockSpec((1, rows, GDN_CONV_CH), lambda i, j: (i, j, 0)),
                  pl.BlockSpec((1, rows, GDN_VW), lambda i, j: (i, j, 0)),
                  pl.BlockSpec((1, rows, LANES), lambda i, j: (i, j, 0)),
                  pl.BlockSpec((CONV_W, GDN_CONV_CH), lambda i, j: (0, 0)),
                  pl.BlockSpec((1, SUBLANES, GDN_CONV_CH), lambda i, j: (i, 0, 0)),
                  pl.BlockSpec((1, GDN_HEADS, GDN_DK, GDN_DV), lambda i, j: (i, 0, 0, 0)),
                  pl.BlockSpec((2, LANES), lambda i, j: (0, 0)),
                  pl.BlockSpec((1, GDN_DV), lambda i, j: (0, 0))],
        out_specs=[pl.BlockSpec((1, rows, GDN_VW), lambda i, j: (i, j, 0)),
                   pl.BlockSpec((1, GDN_HEADS, GDN_DK, GDN_DV), lambda i, j: (i, 0, 0, 0))],
        scratch_shapes=[pltpu.VMEM((SUBLANES + rows, GDN_CONV_CH), f32)],
        compiler_params=_params("parallel", "arbitrary"),
        name="gdn",
    )(qkv, z, sm, conv_w, conv_buf8, s0, par, norm_w)


def _dsa_kernel(q_ref, qi_ref, sm_ref, k_ref, v_ref, ki_ref, o_ref,
                key_hi, key_lo, m_s, l_s, acc_s, *, qb, tk, pos0, n_keys, topk):
    blk = pl.program_id(1)
    row = lax.broadcasted_iota(jnp.int32, (qb, 1), 0)
    q_pos = pos0 + blk * qb + row
    limit = jnp.minimum((q_pos // CHUNK + 1) * CHUNK, n_keys)
    last_limit = jnp.minimum(((pos0 + (blk + 1) * qb - 1) // CHUNK + 1) * CHUNK, n_keys)
    step_w = min(tk, DSA_TAIL_STEP)
    n_full = last_limit // tk
    tail_w = (last_limit - n_full * tk + step_w - 1) // step_w * step_w
    tail_widths = tuple(range(step_w, tk + 1, step_w))
    n_tiles = n_full + jnp.where(tail_w > 0, 1, 0)
    lane = lax.broadcasted_iota(jnp.int32, (qb, LANES), 1)
    lower_half = lane < DSA_DH

    def for_each_tile(fn):
        def whole(kt, carry):
            fn(kt, tk)
            return carry
        lax.fori_loop(0, n_full, whole, 0)
        for w in tail_widths:
            @pl.when(tail_w == w)
            def _():
                fn(n_full, w)

    def keys_of(ref, kt, width):
        return ref[kt] if width == tk else ref[kt, :, 0:width]

    sm = sm_ref[0]
    qi_heads = []
    for h in range(IDX_HEADS):
        pair = qi_ref[0, :, (h // 2) * LANES:(h // 2 + 1) * LANES]
        keep = lower_half if h % 2 == 0 else jnp.logical_not(lower_half)
        qi_heads.append(jnp.where(keep, pair, jnp.zeros_like(pair)))
    wi = [sm[:, SM_WI + h:SM_WI + h + 1] * (IDX_HEADS ** -0.5) for h in range(IDX_HEADS)]

    def score_tile(kt, width):
        start = pl.multiple_of(kt * tk, tk)
        ki_t = ki_ref[0, pl.ds(start, width), :]
        score = jnp.zeros((qb, width), f32)
        for h in range(IDX_HEADS):
            score = score + jnp.maximum(_mm_nt(qi_heads[h], ki_t), 0.0) * wi[h]
        bits = pltpu.bitcast(score, jnp.int32)
        key = bits ^ ((bits >> 31) & 0x7FFFFFFF)
        col = lax.broadcasted_iota(jnp.int32, (qb, width), 1)
        key = jnp.where(kt * tk + col < limit, key, INT_MIN)
        hi = (key >> 16).astype(jnp.int16)
        lo = ((key & 0xFFFF) - HALF_BIAS).astype(jnp.int16)
        if width == tk:
            key_hi[kt] = hi
            key_lo[kt] = lo
        else:
            lowest = jnp.full((qb, tk - width), -HALF_BIAS, jnp.int16)
            key_hi[kt, :, 0:width] = hi
            key_hi[kt, :, width:tk] = lowest
            key_lo[kt, :, 0:width] = lo
            key_lo[kt, :, width:tk] = lowest

    for_each_tile(score_tile)

    one16, zero16 = jnp.int16(1), jnp.int16(0)

    def lane_fold(h16):
        part = h16[:, 0:LANES]
        for g in range(1, h16.shape[1] // LANES):
            part = part + h16[:, g * LANES:(g + 1) * LANES]
        return part

    def count(hit):
        acc = lax.fori_loop(0, n_full, lambda kt, a: a + lane_fold(hit(kt, tk)),
                            jnp.zeros((qb, LANES), jnp.int16))
        for w in tail_widths:
            acc = lax.cond(tail_w == w, lambda a, w=w: a + lane_fold(hit(n_full, w)), lambda a: a, acc)
        return jnp.sum(acc.astype(jnp.int32), axis=1, keepdims=True)

    def kth_largest_half(ref, want):
        def bit_step(i, u):
            cand = u | (jnp.int32(1) << (15 - i))
            cand16 = (cand - HALF_BIAS).astype(jnp.int16)
            n_ge_cand = count(lambda kt, w: jnp.where(keys_of(ref, kt, w) >= cand16, one16, zero16))
            return jnp.where(n_ge_cand >= want, cand, u)
        return lax.fori_loop(0, 16, bit_step, jnp.zeros((qb, 1), jnp.int32))

    hi_u = kth_largest_half(key_hi, topk)
    thr_hi = (hi_u - HALF_BIAS).astype(jnp.int16)
    n_hi_gt = count(lambda kt, w: jnp.where(keys_of(key_hi, kt, w) > thr_hi, one16, zero16))

    def mask_low(kt, width):
        masked = jnp.where(keys_of(key_hi, kt, width) == thr_hi, keys_of(key_lo, kt, width), jnp.int16(-HALF_BIAS))
        if width == tk:
            key_lo[kt] = masked
        else:
            key_lo[kt, :, 0:width] = masked

    for_each_tile(mask_low)
    lo_u = kth_largest_half(key_lo, topk - n_hi_gt)
    lo_u = jnp.where(jnp.logical_and(hi_u == 0, lo_u == 0), 1, lo_u)
    thr_lo = (lo_u - HALF_BIAS).astype(jnp.int16)

    def tie_band(kt, width, low_pred):
        in_band = keys_of(key_hi, kt, width) == thr_hi
        return jnp.where(in_band, jnp.where(low_pred(keys_of(key_lo, kt, width)), one16, zero16), zero16)

    n_gt = n_hi_gt + count(lambda kt, w: tie_band(kt, w, lambda lo: lo > thr_lo))
    n_ge = n_hi_gt + count(lambda kt, w: tie_band(kt, w, lambda lo: lo >= thr_lo))
    need = (topk - n_gt).astype(f32)
    surplus = jnp.max(jnp.where(n_ge > topk, 1, 0))

    q_pairs = []
    for pi in range(DSA_HEADS // 2):
        pair = q_ref[0, :, pi * LANES:(pi + 1) * LANES]
        zero = jnp.zeros_like(pair)
        q_pairs.append(jnp.concatenate([jnp.where(lower_half, pair, zero), jnp.where(lower_half, zero, pair)], axis=0))

    def attend_tile(kt, width, sel):
        bias = (1.0 - sel) * NEG_BIG
        bias2 = jnp.concatenate([bias, bias], axis=0)
        start = pl.multiple_of(kt * tk, tk)
        for pi in range(DSA_HEADS // 2):
            pr = slice(pi * LANES, (pi + 1) * LANES)
            logits = _mm_nt(q_pairs[pi], k_ref[0, pl.ds(start, width), pr]) + bias2
            m_old = m_s[pi]
            m_new = jnp.maximum(m_old, jnp.max(logits, axis=1, keepdims=True))
            alpha = jnp.exp2(m_old - m_new)
            p = jnp.exp2(logits - m_new)
            l_s[pi] = alpha * l_s[pi] + jnp.sum(p, axis=1, keepdims=True)
            acc_s[pi] = alpha * acc_s[pi] + _mm(p, v_ref[0, pl.ds(start, width), pr])
            m_s[pi] = m_new

    def start_softmax():
        m_s[...] = jnp.full(m_s.shape, M_FLOOR, f32)
        l_s[...] = jnp.zeros(l_s.shape, f32)
        acc_s[...] = jnp.zeros(acc_s.shape, f32)

    def finish_softmax():
        for pi in range(DSA_HEADS // 2):
            out = acc_s[pi] / l_s[pi]
            o_ref[0, :, pi * LANES:(pi + 1) * LANES] = jnp.where(lower_half, out[:qb], out[qb:]).astype(o_ref.dtype)

    @pl.when(surplus == 0)
    def _():
        start_softmax()

        def tile(kt, width):
            sel = tie_band(kt, width, lambda lo: lo >= thr_lo)
            sel = jnp.where(keys_of(key_hi, kt, width) > thr_hi, one16, sel).astype(f32)
            attend_tile(kt, width, sel)

        for_each_tile(tile)
        finish_softmax()

    @pl.when(surplus != 0)
    def _():
        start_softmax()
        tc_w = min(tk, TIE_CHUNK)
        tr = lax.broadcasted_iota(jnp.int32, (tc_w, tc_w), 0)
        tc = lax.broadcasted_iota(jnp.int32, (tc_w, tc_w), 1)
        prefix = jnp.where(tr <= tc, 1.0, 0.0).astype(MXU_DTYPE)
        t_hi = thr_hi.astype(jnp.int32)
        t_lo = thr_lo.astype(jnp.int32)

        def tile(kt, taken):
            hi = key_hi[kt].astype(jnp.int32)
            lo = key_lo[kt].astype(jnp.int32)
            in_band = hi == t_hi
            above = jnp.where(hi > t_hi, 1.0, jnp.where(in_band, jnp.where(lo > t_lo, 1.0, 0.0), 0.0))
            tie = jnp.where(in_band, jnp.where(lo == t_lo, 1.0, 0.0), 0.0)
            parts = []
            for c0 in range(0, tk, tc_w):
                tie_c = tie[:, c0:c0 + tc_w]
                rank = taken + _mm(tie_c, prefix)
                parts.append(above[:, c0:c0 + tc_w] + jnp.where(rank <= need, tie_c, 0.0))
                taken = rank[:, tc_w - 1:tc_w]
            sel = parts[0] if len(parts) == 1 else jnp.concatenate(parts, axis=1)
            attend_tile(kt, tk, sel)
            return taken

        lax.fori_loop(0, n_tiles, tile, jnp.zeros((qb, 1), f32))
        finish_softmax()


def _dsa_call(qb_arr, qib_arr, sm, k_all, v_all, ki_all, pos0, n_keys, qb, tk):
    b, t, _ = qb_arr.shape
    lp = k_all.shape[1]
    assert lp % tk == 0 and t % qb == 0
    topk = min(TOPK_MAX, n_keys // 4)
    kern = functools.partial(_dsa_kernel, qb=qb, tk=tk, pos0=pos0, n_keys=n_keys, topk=topk)
    resident = dict(pipeline_mode=pl.Buffered(1))
    return pl.pallas_call(
        kern,
        out_shape=jax.ShapeDtypeStruct((b, t, DSA_W), MXU_DTYPE),
        grid=(b, t // qb),
        in_specs=[pl.BlockSpec((1, qb, DSA_W), lambda i, j: (i, j, 0)),
                  pl.BlockSpec((1, qb, IDX_W), lambda i, j: (i, j, 0)),
                  pl.BlockSpec((1, qb, LANES), lambda i, j: (i, j, 0)),
                  pl.BlockSpec((1, lp, DSA_W), lambda i, j: (i, 0, 0), **resident),
                  pl.BlockSpec((1, lp, DSA_W), lambda i, j: (i, 0, 0), **resident),
                  pl.BlockSpec((1, lp, LANES), lambda i, j: (i, 0, 0), **resident)],
        out_specs=pl.BlockSpec((1, qb, DSA_W), lambda i, j: (i, j, 0)),
        scratch_shapes=[pltpu.VMEM((lp // tk, qb, tk), jnp.int16),
                        pltpu.VMEM((lp // tk, qb, tk), jnp.int16),
                        pltpu.VMEM((DSA_HEADS // 2, 2 * qb, 1), f32),
                        pltpu.VMEM((DSA_HEADS // 2, 2 * qb, 1), f32),
                        pltpu.VMEM((DSA_HEADS // 2, 2 * qb, LANES), f32)],
        compiler_params=_params("parallel", "arbitrary"),
        name="dsa",
    )(qb_arr, qib_arr, sm, k_all, v_all, ki_all)


def _outproj_kernel(h_ref, og_ref, od_ref, w_ref, g_ref, b_ref, o_ref, *, alpha):
    mix = (jnp.dot(og_ref[...], w_ref[0:GDN_VW, :], preferred_element_type=f32)
           + jnp.dot(od_ref[...], w_ref[GDN_VW:, :], preferred_element_type=f32))
    o_ref[...] = _layer_norm(alpha * h_ref[...] + mix, g_ref[...], b_ref[...])


def _outproj_call(h, og, od, w, g, b, alpha):
    n, d = h.shape
    tm = _row_tile(n, 512)
    return pl.pallas_call(
        functools.partial(_outproj_kernel, alpha=alpha),
        out_shape=jax.ShapeDtypeStruct((n, d), f32),
        grid=(n // tm,),
        in_specs=[pl.BlockSpec((tm, d), lambda i: (i, 0)),
                  pl.BlockSpec((tm, GDN_VW), lambda i: (i, 0)),
                  pl.BlockSpec((tm, DSA_W), lambda i: (i, 0)),
                  pl.BlockSpec((GDN_VW + DSA_W, d), lambda i: (0, 0)),
                  pl.BlockSpec((1, d), lambda i: (0, 0)),
                  pl.BlockSpec((1, d), lambda i: (0, 0))],
        out_specs=pl.BlockSpec((tm, d), lambda i: (i, 0)),
        compiler_params=_params("parallel"),
        name="outproj_ln",
    )(h, og, od, w, g.reshape(1, d), b.reshape(1, d))


def _ffn_kernel(x_ref, wg_ref, wu_ref, wd_ref, g_ref, b_ref, o_ref, acc, *, alpha):
    j = pl.program_id(1)

    @pl.when(j == 0)
    def _():
        acc[...] = jnp.zeros(acc.shape, f32)

    xb = x_ref[...].astype(MXU_DTYPE)
    gate = jnp.dot(xb, wg_ref[...], preferred_element_type=f32)
    up = jnp.dot(xb, wu_ref[...], preferred_element_type=f32)
    act = gate * _sigmoid(gate) * up
    acc[...] += jnp.dot(act.astype(MXU_DTYPE), wd_ref[...], preferred_element_type=f32)

    @pl.when(j == pl.num_programs(1) - 1)
    def _():
        o_ref[...] = _layer_norm(alpha * x_ref[...] + acc[...], g_ref[...], b_ref[...])


def _ffn_call(x, wg, wu, wd, g, b, alpha):
    n, d = x.shape
    f = wg.shape[1]
    tm = _row_tile(n, 1024)
    tf = _row_tile(f, 512)
    return pl.pallas_call(
        functools.partial(_ffn_kernel, alpha=alpha),
        out_shape=jax.ShapeDtypeStruct((n, d), f32),
        grid=(n // tm, f // tf),
        in_specs=[pl.BlockSpec((tm, d), lambda i, j: (i, 0)),
                  pl.BlockSpec((d, tf), lambda i, j: (0, j)),
                  pl.BlockSpec((d, tf), lambda i, j: (0, j)),
                  pl.BlockSpec((tf, d), lambda i, j: (j, 0)),
                  pl.BlockSpec((1, d), lambda i, j: (0, 0)),
                  pl.BlockSpec((1, d), lambda i, j: (0, 0))],
        out_specs=pl.BlockSpec((tm, d), lambda i, j: (i, 0)),
        scratch_shapes=[pltpu.VMEM((tm, d), f32)],
        compiler_params=_params("parallel", "arbitrary"),
        name="ffn_ln",
    )(x, wg, wu, wd, g.reshape(1, d), b.reshape(1, d))


def _moe_plan_kernel(x_ref, wr_ref, plan_o, plan_t_o, xb_o, *, n_experts):
    tm = x_ref.shape[0]
    lane = lax.broadcasted_iota(jnp.int32, (tm, LANES), 1)
    x = x_ref[...]
    xb_o[...] = x.astype(xb_o.dtype)
    logits = _mm_f32(x, wr_ref[...])
    logits = jnp.where(lane < n_experts, logits, -jnp.inf)
    m1 = jnp.max(logits, axis=1, keepdims=True)
    i1 = jnp.min(jnp.where(logits == m1, lane, LANES), axis=1, keepdims=True)
    rest = jnp.where(lane == i1, -jnp.inf, logits)
    m2 = jnp.max(rest, axis=1, keepdims=True)
    i2 = jnp.min(jnp.where(rest == m2, lane, LANES), axis=1, keepdims=True)
    e2 = jnp.exp(m2 - m1)
    w1 = 1.0 / (1.0 + e2)
    gates = jnp.where(lane == i1, w1, 0.0) + jnp.where(lane == i2, e2 * w1, 0.0)
    chosen = jnp.where(lane == i1, 1.0, 0.0) + jnp.where(lane == i2, 1.0, 0.0)
    blk = min(tm, MOE_PREFIX_BLOCK)
    r = lax.broadcasted_iota(jnp.int32, (blk, blk), 0)
    c = lax.broadcasted_iota(jnp.int32, (blk, blk), 1)
    tri = jnp.where(r >= c, 1.0, 0.0).astype(MXU_DTYPE)
    carry = jnp.zeros((1, LANES), f32)
    slots = []
    for b0 in range(0, tm, blk):
        m = chosen[b0:b0 + blk]
        inclusive = jnp.dot(tri, m.astype(MXU_DTYPE), preferred_element_type=f32) + carry
        slots.append(inclusive - m)
        carry = inclusive[blk - 1:blk, :]
    slot = jnp.concatenate(slots, axis=0) if len(slots) > 1 else slots[0]
    plan = jnp.where(lane < n_experts, gates,
                     jnp.where(lane < 2 * n_experts, pltpu.roll(slot, n_experts, 1),
                               jnp.where(lane < 3 * n_experts, pltpu.roll(chosen, 2 * n_experts, 1), 0.0)))
    plan_o[...] = plan
    plan_t_o[...] = plan.T[0:MOE_PLAN_ROWS, :]


def _moe_expert_kernel(xb_ref, plan_ref, plan_t_ref, wg_ref, wu_ref, wd_ref, y_ref,
                       xc, acc_c, *, n_experts, slot_block):
    e = pl.program_id(1)
    j = pl.program_id(2)
    tm = xb_ref.shape[0]
    cb = slot_block
    slot_row = plan_t_ref[pl.ds(e + n_experts, 1), :]
    chosen_row = plan_t_ref[pl.ds(e + 2 * n_experts, 1), :]
    count = jnp.max(jnp.where(chosen_row > 0.0, slot_row + 1.0, 0.0)).astype(jnp.int32)
    n_blocks = (count + cb - 1) // cb
    dot = functools.partial(jnp.dot, preferred_element_type=f32)

    def one_hot(sb):
        slot_id = (lax.broadcasted_iota(jnp.int32, (cb, 1), 0) + sb * cb).astype(f32)
        hit = jnp.where(chosen_row > 0.0, jnp.where(slot_row == slot_id, 1.0, 0.0), 0.0)
        return hit.astype(MXU_DTYPE)

    def rows_of(sb):
        return pl.ds(pl.multiple_of(sb * cb, cb), cb)

    @pl.when(jnp.logical_and(e == 0, j == 0))
    def _():
        y_ref[...] = jnp.zeros(y_ref.shape, f32)

    @pl.when(j == 0)
    def _():
        def gather(sb, carry):
            xc[rows_of(sb), :] = dot(one_hot(sb), xb_ref[...]).astype(xc.dtype)
            acc_c[rows_of(sb), :] = jnp.zeros((cb, acc_c.shape[1]), f32)
            return carry
        lax.fori_loop(0, n_blocks, gather, 0)

    def expert(sb, carry):
        xs = xc[rows_of(sb), :]
        gate = dot(xs, wg_ref[0])
        up = dot(xs, wu_ref[0])
        act = gate * _sigmoid(gate) * up
        acc_c[rows_of(sb), :] += dot(act.astype(MXU_DTYPE), wd_ref[0])
        return carry

    lax.fori_loop(0, n_blocks, expert, 0)

    @pl.when(j == pl.num_programs(2) - 1)
    def _():
        lane = lax.broadcasted_iota(jnp.int32, (tm, LANES), 1)
        gate_col = jnp.sum(jnp.where(lane == e, plan_ref[...], 0.0), axis=1, keepdims=True)

        def scatter(sb, carry):
            back = lax.dot_general(one_hot(sb), acc_c[rows_of(sb), :].astype(MXU_DTYPE),
                                   (((0,), (0,)), ((), ())), preferred_element_type=f32)
            y_ref[...] += gate_col * back
            return carry
        lax.fori_loop(0, n_blocks, scatter, 0)


def _residual_ln_kernel(x_ref, y_ref, g_ref, b_ref, o_ref, *, alpha):
    o_ref[...] = _layer_norm(alpha * x_ref[...] + y_ref[...], g_ref[...], b_ref[...])


def _moe_call(x, wr, wg, wu, wd, g, b, alpha):
    n, d = x.shape
    n_experts, _, f = wg.shape
    assert 3 * n_experts <= MOE_PLAN_ROWS
    tm = _row_tile(n, MOE_TOKEN_TILE)
    tf = _row_tile(f, 512)
    cb = min(MOE_SLOT_BLOCK, tm)
    cap = -(-tm // cb) * cb
    plan, plan_t, xb = pl.pallas_call(
        functools.partial(_moe_plan_kernel, n_experts=n_experts),
        out_shape=[jax.ShapeDtypeStruct((n, LANES), f32),
                   jax.ShapeDtypeStruct((MOE_PLAN_ROWS, n), f32),
                   jax.ShapeDtypeStruct((n, d), MXU_DTYPE)],
        grid=(n // tm,),
        in_specs=[pl.BlockSpec((tm, d), lambda i: (i, 0)),
                  pl.BlockSpec((d, LANES), lambda i: (0, 0))],
        out_specs=[pl.BlockSpec((tm, LANES), lambda i: (i, 0)),
                   pl.BlockSpec((MOE_PLAN_ROWS, tm), lambda i: (0, i)),
                   pl.BlockSpec((tm, d), lambda i: (i, 0))],
        compiler_params=_params("parallel"),
        name="moe_plan",
    )(x, wr)
    y = pl.pallas_call(
        functools.partial(_moe_expert_kernel, n_experts=n_experts, slot_block=cb),
        out_shape=jax.ShapeDtypeStruct((n, d), f32),
        grid=(n // tm, n_experts, f // tf),
        in_specs=[pl.BlockSpec((tm, d), lambda i, e, j: (i, 0)),
                  pl.BlockSpec((tm, LANES), lambda i, e, j: (i, 0)),
                  pl.BlockSpec((MOE_PLAN_ROWS, tm), lambda i, e, j: (0, i)),
                  pl.BlockSpec((1, d, tf), lambda i, e, j: (e, 0, j)),
                  pl.BlockSpec((1, d, tf), lambda i, e, j: (e, 0, j)),
                  pl.BlockSpec((1, tf, d), lambda i, e, j: (e, j, 0))],
        out_specs=pl.BlockSpec((tm, d), lambda i, e, j: (i, 0)),
        scratch_shapes=[pltpu.VMEM((cap, d), MXU_DTYPE),
                        pltpu.VMEM((cap, d), f32)],
        compiler_params=_params("parallel", "arbitrary", "arbitrary"),
        name="moe_experts",
    )(xb, plan, plan_t, wg, wu, wd)
    tr = _row_tile(n, 1024)
    return pl.pallas_call(
        functools.partial(_residual_ln_kernel, alpha=alpha),
        out_shape=jax.ShapeDtypeStruct((n, d), f32),
        grid=(n // tr,),
        in_specs=[pl.BlockSpec((tr, d), lambda i: (i, 0)),
                  pl.BlockSpec((tr, d), lambda i: (i, 0)),
                  pl.BlockSpec((1, d), lambda i: (0, 0)),
                  pl.BlockSpec((1, d), lambda i: (0, 0))],
        out_specs=pl.BlockSpec((tr, d), lambda i: (i, 0)),
        compiler_params=_params("parallel"),
        name="moe_residual_ln",
    )(x, y, g.reshape(1, d), b.reshape(1, d))


def _pack_w_in(w_in):
    o_z = GDN_CONV_CH
    o_b = o_z + GDN_VW
    o_a = o_b + GDN_HEADS
    o_qd = o_a + GDN_HEADS
    o_kd = o_qd + DSA_W
    o_vd = o_kd + DSA_W
    o_qi = o_vd + DSA_W
    o_ki = o_qi + IDX_W
    o_wi = o_ki + IDX_DIM
    assert w_in.shape[-1] == o_wi + IDX_HEADS
    sl = lambda a, n: w_in[..., a:a + n]
    pad = jnp.zeros(w_in.shape[:-1] + (LANES - 2 * GDN_HEADS - IDX_HEADS,), w_in.dtype)
    packed = jnp.concatenate([
        sl(0, GDN_CONV_CH), sl(o_z, GDN_VW), sl(o_qd, DSA_W), sl(o_kd, DSA_W), sl(o_vd, DSA_W),
        sl(o_qi, IDX_W), sl(o_ki, IDX_DIM), sl(o_ki, IDX_DIM),
        sl(o_b, GDN_HEADS), sl(o_a, GDN_HEADS), sl(o_wi, IDX_HEADS), pad], axis=-1)
    assert packed.shape[-1] == C_END
    return packed.astype(MXU_DTYPE)


def _rope_tables(pos):
    half = DSA_DH // 8
    inv_freq = ROPE_THETA ** (-jnp.arange(half, dtype=f32) / half)
    ang = pos.astype(f32)[:, None] * inv_freq[None, :]
    cos, sin = jnp.cos(ang), jnp.sin(ang)
    t = pos.shape[0]
    ones = jnp.ones((t, DSA_DH - 2 * half), f32)
    zeros = jnp.zeros((t, DSA_DH - 2 * half), f32)
    z8 = jnp.zeros((t, half), f32)
    cos_t = jnp.concatenate([cos, cos, ones], -1)
    lo = jnp.concatenate([-sin, z8, zeros], -1)
    hi = jnp.concatenate([z8, sin, zeros], -1)
    return tuple(jnp.tile(a, (1, LANES // DSA_DH)) for a in (cos_t, lo, hi))


def _gdn_params(a_log, dt_bias):
    row = lambda v: jnp.zeros((LANES,), f32).at[SM_DECAY:SM_DECAY + GDN_HEADS].set(v.astype(f32))
    return jnp.stack([row(a_log), row(dt_bias)])


def _trunk(x, pos0, conv_bufs, s0s, past_k, past_v, past_ik, chunk, wts, depth):
    b, t, d = x.shape
    n = b * t
    alpha = (2.0 * depth) ** 0.25
    tabs = _rope_tables(pos0 + jnp.arange(t))
    past = 0 if past_k is None else past_k.shape[2]
    n_keys = past + t
    assert t >= CONV_W - 1
    qb = min(t, DSA_Q_BLOCK)
    tk = DSA_KEY_TILE
    lp = -(-n_keys // tk) * tk

    def with_past(new, old, l):
        parts = [] if old is None else [old[l].reshape(b, past, -1).astype(MXU_DTYPE)]
        parts.append(new.reshape(b, t, -1))
        if lp > n_keys:
            parts.append(jnp.zeros((b, lp - n_keys, new.shape[-1]), MXU_DTYPE))
        return parts[0] if len(parts) == 1 else jnp.concatenate(parts, axis=1)

    h = _ln_call(x.reshape(n, d), wts["ln_in_g"], wts["ln_in_b"])
    new_k, new_v, new_ik, new_conv, new_s = [], [], [], [], []
    for l in range(depth):
        (qkv, z, k_f, v_f, ki_f, q_b, k_b, v_b, qi_b, ki_b, sm) = _inproj_call(h, wts["w_in"][l], tabs, t)
        qkv3 = qkv.reshape(b, t, GDN_CONV_CH)
        buf8 = jnp.concatenate([jnp.zeros((b, SUBLANES - (CONV_W - 1), GDN_CONV_CH), f32), conv_bufs[l]], axis=1)
        o_g, s_new = _gdn_call(qkv3, z.reshape(b, t, GDN_VW), sm.reshape(b, t, LANES), wts["conv_w"][l],
                               buf8, s0s[l], _gdn_params(wts["gdn_a_log"][l], wts["gdn_dt_bias"][l]),
                               wts["gdn_norm_w"][l].reshape(1, GDN_DV), chunk)
        ki_dup = None if past_ik is None else jnp.concatenate([past_ik, past_ik], axis=-1)
        o_d = _dsa_call(q_b.reshape(b, t, DSA_W), qi_b.reshape(b, t, IDX_W), sm.reshape(b, t, LANES),
                        with_past(k_b, past_k, l), with_past(v_b, past_v, l), with_past(ki_b, ki_dup, l),
                        past, n_keys, qb, tk)
        h = _outproj_call(h, o_g.reshape(n, GDN_VW), o_d.reshape(n, DSA_W), wts["w_out"][l],
                          wts["ln1_g"][l], wts["ln1_b"][l], alpha)
        if l % 2 == 0:
            i = l // 2
            h = _ffn_call(h, wts["ffn_wg"][i], wts["ffn_wu"][i], wts["ffn_wd"][i],
                          wts["ln2_g"][l], wts["ln2_b"][l], alpha)
        else:
            i = l // 2
            h = _moe_call(h, wts["moe_router"][i], wts["moe_wg"][i], wts["moe_wu"][i], wts["moe_wd"][i],
                          wts["ln2_g"][l], wts["ln2_b"][l], alpha)
        new_k.append(k_f.reshape(b, t, DSA_HEADS, DSA_DH))
        new_v.append(v_f.reshape(b, t, DSA_HEADS, DSA_DH))
        new_ik.append(ki_f.reshape(b, t, IDX_DIM))
        new_conv.append(qkv3[:, t - (CONV_W - 1):])
        new_s.append(s_new)
    return (h.reshape(b, t, d), jnp.stack(new_k), jnp.stack(new_v), jnp.stack(new_ik),
            jnp.stack(new_s), jnp.stack(new_conv))


def kernel(x_prompt, x_sample, cache_k, cache_v, cache_idx_k, state_gdn, state_conv, ln_in_g, ln_in_b, w_in, conv_w, gdn_a_log, gdn_dt_bias, gdn_norm_w, w_out, ln1_g, ln1_b, ln2_g, ln2_b, ffn_wg, ffn_wu, ffn_wd, moe_router, moe_wg, moe_wu, moe_wd):
    depth = w_in.shape[0]
    n_experts = moe_router.shape[-1]
    bp, tp, d = x_prompt.shape
    cast = lambda a: a.astype(MXU_DTYPE)
    router = jnp.concatenate(
        [moe_router.astype(f32), jnp.zeros(moe_router.shape[:-1] + (LANES - n_experts,), f32)], axis=-1)
    wts = dict(ln_in_g=ln_in_g, ln_in_b=ln_in_b, w_in=_pack_w_in(w_in), conv_w=conv_w,
               gdn_a_log=gdn_a_log, gdn_dt_bias=gdn_dt_bias, gdn_norm_w=gdn_norm_w, w_out=cast(w_out),
               ln1_g=ln1_g, ln1_b=ln1_b, ln2_g=ln2_g, ln2_b=ln2_b,
               ffn_wg=cast(ffn_wg), ffn_wu=cast(ffn_wu), ffn_wd=cast(ffn_wd),
               moe_router=router, moe_wg=cast(moe_wg), moe_wu=cast(moe_wu), moe_wd=cast(moe_wd))
    zero_conv = jnp.zeros((depth, bp, CONV_W - 1, GDN_CONV_CH), f32)
    zero_s = jnp.zeros((depth, bp, GDN_HEADS, GDN_DK, GDN_DV), f32)
    y_p, k_p, v_p, ik_p, s_p, conv_p = _trunk(
        x_prompt, 0, zero_conv, zero_s, None, None, None, CHUNK, wts, depth)
    ts = x_sample.shape[1]
    past = cache_k.shape[2]
    y_s, k_s, v_s, ik_s, s_s, conv_s = _trunk(
        x_sample, past, state_conv, state_gdn, cache_k, cache_v, cache_idx_k, ts, wts, depth)
    return (y_p, y_s, k_p, v_p, ik_p, s_p, conv_p, k_s, v_s, ik_s, s_s, conv_s)
```

```python
import functools

import jax
import jax.numpy as jnp
from jax import lax
from jax.experimental import pallas as pl
from jax.experimental.pallas import tpu as pltpu

CHUNK = 64
CONV_W = 4
GDN_HEADS = 4
GDN_DK = 128
GDN_DV = 128
DSA_HEADS = 8
DSA_DH = 64
IDX_HEADS = 4
IDX_DIM = 64
TOPK_MAX = 256
ROPE_THETA = 500000.0
TOP_K_EXPERTS = 2
LN_EPS = 1e-5
RMS_EPS = 1e-6
L2_EPS = 1e-6

GDN_QK = GDN_HEADS * GDN_DK
GDN_VW = GDN_HEADS * GDN_DV
GDN_CONV_CH = 2 * GDN_QK + GDN_VW
DSA_W = DSA_HEADS * DSA_DH
IDX_W = IDX_HEADS * IDX_DIM

LANES = 128
SUBLANES = 8
VMEM_LIMIT = 56 * 1024 * 1024

DSA_Q_BLOCK = 256
DSA_KEY_TILE = 2048
GDN_CHUNKS_PER_STEP = 4
MOE_TOKEN_TILE = 2048
MOE_SLOT_BLOCK = 576
MOE_PREFIX_BLOCK = 256
MOE_PLAN_ROWS = 24

MXU_DTYPE = jnp.bfloat16

C_QKV = 0
C_Z = C_QKV + GDN_CONV_CH
C_QD = C_Z + GDN_VW
C_KD = C_QD + DSA_W
C_VD = C_KD + DSA_W
C_QI = C_VD + DSA_W
C_KI = C_QI + IDX_W
C_SM = C_KI + LANES
C_END = C_SM + LANES
SM_BETA = 0
SM_DECAY = GDN_HEADS
SM_WI = 2 * GDN_HEADS

INT_MIN = -2 ** 31
HALF_BIAS = 2 ** 15
TIE_CHUNK = 512
NEG_BIG = -3e38
M_FLOOR = -1e38
LOG2E = 1.4426950408889634

f32 = jnp.float32


def _mm(a, b):
    return jnp.dot(a.astype(MXU_DTYPE), b.astype(MXU_DTYPE), preferred_element_type=f32)


def _mm_nt(a, b):
    return lax.dot_general(a.astype(MXU_DTYPE), b.astype(MXU_DTYPE), (((1,), (1,)), ((), ())),
                           preferred_element_type=f32)


def _mm_tn(a, b):
    return lax.dot_general(a.astype(MXU_DTYPE), b.astype(MXU_DTYPE), (((0,), (0,)), ((), ())),
                           preferred_element_type=f32)


def _split(x):
    hi = x.astype(MXU_DTYPE)
    lo = (x - hi.astype(f32)).astype(MXU_DTYPE)
    return hi, lo


def _sigmoid(x):
    return 1.0 / (1.0 + jnp.exp(-x))


def _layer_norm(r, g, b):
    mu = jnp.mean(r, axis=-1, keepdims=True)
    xc = r - mu
    var = jnp.mean(xc * xc, axis=-1, keepdims=True)
    return xc * lax.rsqrt(var + LN_EPS) * g + b


def _params(*sem):
    return pltpu.CompilerParams(dimension_semantics=sem, vmem_limit_bytes=VMEM_LIMIT)


def _row_tile(n, want):
    t = min(n, want)
    assert n % t == 0, (n, t)
    return t


def _ln_kernel(x_ref, g_ref, b_ref, o_ref):
    o_ref[...] = _layer_norm(x_ref[...], g_ref[...], b_ref[...])


def _ln_call(x, g, b):
    n, d = x.shape
    tm = _row_tile(n, 1024)
    return pl.pallas_call(
        _ln_kernel,
        out_shape=jax.ShapeDtypeStruct((n, d), f32),
        grid=(n // tm,),
        in_specs=[pl.BlockSpec((tm, d), lambda i: (i, 0)),
                  pl.BlockSpec((1, d), lambda i: (0, 0)),
                  pl.BlockSpec((1, d), lambda i: (0, 0))],
        out_specs=pl.BlockSpec((tm, d), lambda i: (i, 0)),
        compiler_params=_params("parallel"),
        name="ln_in",
    )(x, g.reshape(1, d), b.reshape(1, d))


def _rope(x, cos_t, sin_lo, sin_hi):
    w = x.shape[1]
    reps = w // LANES
    c = jnp.tile(cos_t, (1, reps))
    lo = jnp.tile(sin_lo, (1, reps))
    hi = jnp.tile(sin_hi, (1, reps))
    half = DSA_DH // 8
    from_above = pltpu.roll(x, w - half, 1)
    from_below = pltpu.roll(x, half, 1)
    return x * c + from_above * lo + from_below * hi


def _inproj_kernel(x_ref, w_ref, cos_ref, lo_ref, hi_ref,
                   qkv_o, z_o, k_o, v_o, ki_o, qb_o, kb_o, vb_o, qib_o, kib_o, sm_o):
    xb = x_ref[...].astype(MXU_DTYPE)
    cos_t, lo, hi = cos_ref[...], lo_ref[...], hi_ref[...]

    def proj(c0, c1):
        return jnp.dot(xb, w_ref[:, c0:c1], preferred_element_type=f32)

    qkv_o[...] = proj(C_QKV, C_Z)
    z_o[...] = proj(C_Z, C_QD)
    q = _rope(proj(C_QD, C_KD), cos_t, lo, hi) * (DSA_DH ** -0.5 * LOG2E)
    qb_o[...] = q.astype(qb_o.dtype)
    k = _rope(proj(C_KD, C_VD), cos_t, lo, hi)
    k_o[...] = k
    kb_o[...] = k.astype(kb_o.dtype)
    v = proj(C_VD, C_QI)
    v_o[...] = v
    vb_o[...] = v.astype(vb_o.dtype)
    qi = _rope(proj(C_QI, C_KI), cos_t, lo, hi) * (IDX_DIM ** -0.5)
    qib_o[...] = qi.astype(qib_o.dtype)
    ki = _rope(proj(C_KI, C_SM), cos_t, lo, hi)
    ki_o[...] = ki[:, :IDX_DIM]
    kib_o[...] = ki.astype(kib_o.dtype)
    sm_o[...] = proj(C_SM, C_END)


def _inproj_call(h, w_packed, tabs, seq_len):
    n, d = h.shape
    tm = _row_tile(n, 512)
    cos_t, lo, hi = tabs
    if tm > seq_len:
        assert tm % seq_len == 0
        cos_t, lo, hi = (jnp.tile(a, (tm // seq_len, 1)) for a in (cos_t, lo, hi))
        nblk = 1
    else:
        assert seq_len % tm == 0
        nblk = seq_len // tm
    tab_spec = pl.BlockSpec((tm, LANES), lambda i: (i % nblk, 0))

    def rows(w):
        return pl.BlockSpec((tm, w), lambda i: (i, 0))

    widths = [(GDN_CONV_CH, f32), (GDN_VW, f32), (DSA_W, f32), (DSA_W, f32), (IDX_DIM, f32),
              (DSA_W, MXU_DTYPE), (DSA_W, MXU_DTYPE), (DSA_W, MXU_DTYPE), (IDX_W, MXU_DTYPE),
              (LANES, MXU_DTYPE), (LANES, f32)]
    return pl.pallas_call(
        _inproj_kernel,
        out_shape=[jax.ShapeDtypeStruct((n, w), dt) for w, dt in widths],
        grid=(n // tm,),
        in_specs=[rows(d),
                  pl.BlockSpec((d, C_END), lambda i: (0, 0), pipeline_mode=pl.Buffered(1)),
                  tab_spec, tab_spec, tab_spec],
        out_specs=[rows(w) for w, _ in widths],
        compiler_params=_params("parallel"),
        name="inproj",
    )(h, w_packed, cos_t, lo, hi)


def _wide_to_blocks(w, chunk):
    size = w.shape[1]
    r = lax.broadcasted_iota(jnp.int32, (size, size), 0)
    c = lax.broadcasted_iota(jnp.int32, (size, size), 1)
    tiled = jnp.concatenate([w] * (size // chunk), axis=0)
    return jnp.where(r // chunk == c // chunk, tiled, jnp.zeros_like(tiled))


def _blocks_to_wide(sq, chunk):
    size = sq.shape[1]
    blk = lax.broadcasted_iota(jnp.int32, (chunk, size), 1) // chunk
    out = jnp.zeros((chunk, size), sq.dtype)
    for h in range(size // chunk):
        out = jnp.where(blk == h, sq[h * chunk:(h + 1) * chunk, :], out)
    return out


def _wide_matmul(x, p, chunk):
    x_hi, x_lo = _split(x)
    p_hi, p_lo = _split(p)
    dot = functools.partial(jnp.dot, preferred_element_type=f32)
    both = dot(jnp.concatenate([x_hi, x_lo], axis=0), _wide_to_blocks(p_hi, chunk))
    return both[:chunk] + both[chunk:] + dot(x_hi, _wide_to_blocks(p_lo, chunk))


def _unit_lower_inverse_wide(a_list, chunk):
    size = a_list[0].shape[1]
    r = lax.broadcasted_iota(jnp.int32, (chunk, size), 0)
    c = lax.broadcasted_iota(jnp.int32, (chunk, size), 1) % chunk

    def same_block(n):
        return (r // n) == (c // n)

    base = SUBLANES
    eye = jnp.where(r == c, 1.0, 0.0)
    a0 = [jnp.where(same_block(base), a, 0.0) for a in a_list]
    p = [_wide_matmul(m, m, chunk) for m in a0]
    x = [eye - m for m in a0]
    x = [xi + _wide_matmul(xi, pi, chunk) for xi, pi in zip(x, p)]
    p = [_wide_matmul(pi, pi, chunk) for pi in p]
    x = [xi + _wide_matmul(xi, pi, chunk) for xi, pi in zip(x, p)]
    n = base
    while n < chunk:
        off = [jnp.where(same_block(2 * n), jnp.where(same_block(n), 0.0, a), 0.0) for a in a_list]
        t = [_wide_matmul(xi, oi, chunk) for xi, oi in zip(x, off)]
        x = [xi - _wide_matmul(ti, xi, chunk) for xi, ti in zip(x, t)]
        n *= 2
    return x


def _chunk_cumsum(g, chunk):
    rows = g.shape[0]
    r = lax.broadcasted_iota(jnp.int32, (rows, rows), 0)
    c = lax.broadcasted_iota(jnp.int32, (rows, rows), 1)
    tri = jnp.where(jnp.logical_and(r // chunk == c // chunk, r >= c), 1.0, 0.0).astype(MXU_DTYPE)
    dot = functools.partial(jnp.dot, preferred_element_type=f32)
    g1 = g.astype(MXU_DTYPE)
    rem = g - g1.astype(f32)
    g2 = rem.astype(MXU_DTYPE)
    g3 = (rem - g2.astype(f32)).astype(MXU_DTYPE)
    return dot(tri, g1) + (dot(tri, g2) + dot(tri, g3))


def _gdn_kernel(qkv_ref, z_ref, sm_ref, cw_ref, cb_ref, s0_ref, par_ref, nw_ref,
                o_ref, s_ref, ubuf, *, chunk, group):
    step = pl.program_id(1)
    rows = chunk * group

    @pl.when(step == 0)
    def _():
        ubuf[0:SUBLANES, :] = cb_ref[0]
        s_ref[0] = s0_ref[0]

    ubuf[SUBLANES:SUBLANES + rows, :] = qkv_ref[0]
    y = ubuf[SUBLANES - 3:SUBLANES - 3 + rows, :] * cw_ref[0:1, :]
    for i in range(1, CONV_W):
        y = y + ubuf[SUBLANES - 3 + i:SUBLANES - 3 + i + rows, :] * cw_ref[i:i + 1, :]
    ubuf[0:SUBLANES, :] = ubuf[rows:rows + SUBLANES, :]
    y = y * _sigmoid(y)

    sm = sm_ref[0]
    beta_all = _sigmoid(sm)
    xg = sm + par_ref[1:2, :]
    softplus = jnp.maximum(xg, 0.0) + jnp.log1p(jnp.exp(-jnp.abs(xg)))
    g_all = -jnp.exp(par_ref[0:1, :]) * softplus
    gc_all = _chunk_cumsum(g_all, chunk)
    gc_rows = gc_all.T

    heads = GDN_HEADS
    size = heads * chunk
    w_row = lax.broadcasted_iota(jnp.int32, (chunk, size), 0)
    w_lane = lax.broadcasted_iota(jnp.int32, (chunk, size), 1)
    w_col = w_lane % chunk
    w_head = w_lane // chunk
    causal_w = w_row >= w_col
    strict_w = w_row > w_col
    z = z_ref[0]
    nw = nw_ref[...]
    dot = functools.partial(jnp.dot, preferred_element_type=f32)

    def stack(a, rs, col0, width):
        return jnp.concatenate([a[rs, col0 + h * width:col0 + (h + 1) * width] for h in range(heads)], axis=0)

    def column_stack(a, rs, lane0):
        return jnp.concatenate([a[rs, lane0 + h:lane0 + h + 1] for h in range(heads)], axis=0)

    def column_wide(a, rs, lane0):
        out = jnp.zeros((chunk, size), f32)
        for h in range(heads):
            out = jnp.where(w_head == h, a[rs, lane0 + h:lane0 + h + 1], out)
        return out

    pre = []
    for g in range(group):
        rs = slice(g * chunk, (g + 1) * chunk)
        last = (g + 1) * chunk - 1
        q = stack(y, rs, 0, GDN_DK)
        k = stack(y, rs, GDN_QK, GDN_DK)
        v = stack(y, rs, 2 * GDN_QK, GDN_DV)
        q = q * lax.rsqrt(jnp.sum(q * q, -1, keepdims=True) + L2_EPS) * (GDN_DK ** -0.5)
        k = k * lax.rsqrt(jnp.sum(k * k, -1, keepdims=True) + L2_EPS)
        beta_s = column_stack(beta_all, rs, SM_BETA)
        gc_s = column_stack(gc_all, rs, SM_DECAY)
        g_last = [gc_all[last:last + 1, SM_DECAY + h:SM_DECAY + h + 1] for h in range(heads)]
        g_last_s = jnp.concatenate([jnp.broadcast_to(gl, (chunk, 1)) for gl in g_last], axis=0)
        gc_row_w = jnp.concatenate([gc_rows[SM_DECAY + h:SM_DECAY + h + 1, rs] for h in range(heads)], axis=1)
        decay_w = jnp.exp(jnp.where(causal_w, column_wide(gc_all, rs, SM_DECAY) - gc_row_w, -jnp.inf))
        kk_w = _blocks_to_wide(_mm_nt(k, k), chunk)
        a_w = jnp.where(strict_w, column_wide(beta_all, rs, SM_BETA) * kk_w * decay_w, 0.0)
        pre.append(dict(q=q, k=k, v=v, beta_s=beta_s, gc_s=gc_s, g_last=g_last, g_last_s=g_last_s,
                        decay_w=decay_w, a_w=a_w))
    t_ws = _unit_lower_inverse_wide([pc["a_w"] for pc in pre], chunk)
    local = []
    for pc, t_w in zip(pre, t_ws):
        q, k, v, beta_s, gc_s = pc["q"], pc["k"], pc["v"], pc["beta_s"], pc["gc_s"]
        rhs = jnp.concatenate([v * beta_s, k * (beta_s * jnp.exp(gc_s))], axis=-1)
        t_hi, t_lo = _split(t_w)
        r_hi, r_lo = _split(rhs)
        tb_hi = _wide_to_blocks(t_hi, chunk)
        both = dot(jnp.concatenate([tb_hi, _wide_to_blocks(t_lo, chunk)], axis=0), r_hi)
        sol = both[:size] + both[size:] + dot(tb_hi, r_lo)
        local.append(dict(u0=sol[:, :GDN_DV], wk=sol[:, GDN_DV:],
                          qk=_mm_nt(q, k) * _wide_to_blocks(pc["decay_w"], chunk),
                          q_dec=q * jnp.exp(gc_s), k_dec=k * jnp.exp(pc["g_last_s"] - gc_s), g_last=pc["g_last"]))

    s = [s_ref[0, h] for h in range(heads)]
    for g in range(group):
        rs = slice(g * chunk, (g + 1) * chunk)
        lc = local[g]
        u_parts, o_parts = [], []
        for h in range(heads):
            hs = slice(h * chunk, (h + 1) * chunk)
            both = _mm(jnp.concatenate([lc["wk"][hs], lc["q_dec"][hs]], axis=0), s[h])
            u_parts.append(lc["u0"][hs] - both[:chunk])
            o_parts.append(both[chunk:])
        u = jnp.concatenate(u_parts, axis=0)
        o = jnp.concatenate(o_parts, axis=0) + _mm(lc["qk"], u)
        for h in range(heads):
            hs = slice(h * chunk, (h + 1) * chunk)
            s[h] = s[h] * jnp.exp(lc["g_last"][h]) + _mm_tn(lc["k_dec"][hs], u[hs])
        z_s = stack(z, rs, 0, GDN_DV)
        o = o * lax.rsqrt(jnp.mean(o * o, -1, keepdims=True) + RMS_EPS) * nw * (z_s * _sigmoid(z_s))
        for h in range(heads):
            o_ref[0, rs, h * GDN_DV:(h + 1) * GDN_DV] = o[h * chunk:(h + 1) * chunk].astype(o_ref.dtype)
    for h in range(heads):
        s_ref[0, h] = s[h]


def _gdn_call(qkv, z, sm, conv_w, conv_buf8, s0, par, norm_w, chunk):
    b, t, _ = qkv.shape
    group = min(GDN_CHUNKS_PER_STEP, t // chunk)
    rows = chunk * group
    assert t % rows == 0
    return pl.pallas_call(
        functools.partial(_gdn_kernel, chunk=chunk, group=group),
        out_shape=[jax.ShapeDtypeStruct((b, t, GDN_VW), MXU_DTYPE),
                   jax.ShapeDtypeStruct((b, GDN_HEADS, GDN_DK, GDN_DV), f32)],
        grid=(b, t // rows),
        in_specs=[pl.BlockSpec((1, rows, GDN_CONV_CH), lambda i, j: (i, j, 0)),
                  pl.BlockSpec((1, rows, GDN_VW), lambda i, j: (i, j, 0)),
                  pl.BlockSpec((1, rows, LANES), lambda i, j: (i, j, 0)),
                  pl.BlockSpec((CONV_W, GDN_CONV_CH), lambda i, j: (0, 0)),
                  pl.BlockSpec((1, SUBLANES, GDN_CONV_CH), lambda i, j: (i, 0, 0)),
                  pl.BlockSpec((1, GDN_HEADS, GDN_DK, GDN_DV), lambda i, j: (i, 0, 0, 0)),
                  pl.BlockSpec((2, LANES), lambda i, j: (0, 0)),
                  pl.BlockSpec((1, GDN_DV), lambda i, j: (0, 0))],
        out_specs=[pl.BlockSpec((1, rows, GDN_VW), lambda i, j: (i, j, 0)),
                   pl.BlockSpec((1, GDN_HEADS, GDN_DK, GDN_DV), lambda i, j: (i, 0, 0, 0))],
        scratch_shapes=[pltpu.VMEM((SUBLANES + rows, GDN_CONV_CH), f32)],
        compiler_params=_params("parallel", "arbitrary"),
        name="gdn",
    )(qkv, z, sm, conv_w, conv_buf8, s0, par, norm_w)


def _dsa_kernel(q_ref, qi_ref, sm_ref, k_ref, v_ref, ki_ref, o_ref,
                key_hi, key_lo, m_s, l_s, acc_s, *, qb, tk, pos0, n_keys, topk):
    blk = pl.program_id(1)
    row = lax.broadcasted_iota(jnp.int32, (qb, 1), 0)
    q_pos = pos0 + blk * qb + row
    limit = jnp.minimum((q_pos // CHUNK + 1) * CHUNK, n_keys)
    last_limit = jnp.minimum(((pos0 + (blk + 1) * qb - 1) // CHUNK + 1) * CHUNK, n_keys)
    n_tiles = (last_limit + tk - 1) // tk
    lane = lax.broadcasted_iota(jnp.int32, (qb, LANES), 1)
    lower_half = lane < DSA_DH
    col = lax.broadcasted_iota(jnp.int32, (qb, tk), 1)
    groups = tk // LANES

    sm = sm_ref[0]
    qi_heads = []
    for h in range(IDX_HEADS):
        pair = qi_ref[0, :, (h // 2) * LANES:(h // 2 + 1) * LANES]
        keep = lower_half if h % 2 == 0 else jnp.logical_not(lower_half)
        qi_heads.append(jnp.where(keep, pair, jnp.zeros_like(pair)))
    wi = [sm[:, SM_WI + h:SM_WI + h + 1] * (IDX_HEADS ** -0.5) for h in range(IDX_HEADS)]

    def score_tile(kt, carry):
        ki_t = ki_ref[0, pl.ds(pl.multiple_of(kt * tk, tk), tk), :]
        score = jnp.zeros((qb, tk), f32)
        for h in range(IDX_HEADS):
            score = score + jnp.maximum(_mm_nt(qi_heads[h], ki_t), 0.0) * wi[h]
        bits = pltpu.bitcast(score, jnp.int32)
        key = bits ^ ((bits >> 31) & 0x7FFFFFFF)
        key = jnp.where(kt * tk + col < limit, key, INT_MIN)
        key_hi[kt] = (key >> 16).astype(jnp.int16)
        key_lo[kt] = ((key & 0xFFFF) - HALF_BIAS).astype(jnp.int16)
        return carry

    lax.fori_loop(0, n_tiles, score_tile, 0)

    one16, zero16 = jnp.int16(1), jnp.int16(0)

    def count(hit):
        def body(kt, acc):
            h16 = hit(kt)
            part = h16[:, 0:LANES]
            for g in range(1, groups):
                part = part + h16[:, g * LANES:(g + 1) * LANES]
            return acc + part
        acc = lax.fori_loop(0, n_tiles, body, jnp.zeros((qb, LANES), jnp.int16))
        return jnp.sum(acc.astype(jnp.int32), axis=1, keepdims=True)

    def kth_largest_half(ref, want):
        def bit_step(i, u):
            cand = u | (jnp.int32(1) << (15 - i))
            cand16 = (cand - HALF_BIAS).astype(jnp.int16)
            n_ge_cand = count(lambda kt: jnp.where(ref[kt] >= cand16, one16, zero16))
            return jnp.where(n_ge_cand >= want, cand, u)
        return lax.fori_loop(0, 16, bit_step, jnp.zeros((qb, 1), jnp.int32))

    hi_u = kth_largest_half(key_hi, topk)
    thr_hi = (hi_u - HALF_BIAS).astype(jnp.int16)
    n_hi_gt = count(lambda kt: jnp.where(key_hi[kt] > thr_hi, one16, zero16))

    def mask_low(kt, carry):
        key_lo[kt] = jnp.where(key_hi[kt] == thr_hi, key_lo[kt], jnp.int16(-HALF_BIAS))
        return carry

    lax.fori_loop(0, n_tiles, mask_low, 0)
    lo_u = kth_largest_half(key_lo, topk - n_hi_gt)
    lo_u = jnp.where(jnp.logical_and(hi_u == 0, lo_u == 0), 1, lo_u)
    thr_lo = (lo_u - HALF_BIAS).astype(jnp.int16)

    def tie_band(kt, low_pred):
        return jnp.where(key_hi[kt] == thr_hi, jnp.where(low_pred(key_lo[kt]), one16, zero16), zero16)

    n_gt = n_hi_gt + count(lambda kt: tie_band(kt, lambda lo: lo > thr_lo))
    n_ge = n_hi_gt + count(lambda kt: tie_band(kt, lambda lo: lo >= thr_lo))
    need = (topk - n_gt).astype(f32)
    surplus = jnp.max(jnp.where(n_ge > topk, 1, 0))

    q_pairs = []
    for pi in range(DSA_HEADS // 2):
        pair = q_ref[0, :, pi * LANES:(pi + 1) * LANES]
        zero = jnp.zeros_like(pair)
        q_pairs.append(jnp.concatenate([jnp.where(lower_half, pair, zero), jnp.where(lower_half, zero, pair)], axis=0))

    def attend(exact_ties):
        m_s[...] = jnp.full(m_s.shape, M_FLOOR, f32)
        l_s[...] = jnp.zeros(l_s.shape, f32)
        acc_s[...] = jnp.zeros(acc_s.shape, f32)
        tc_w = min(tk, TIE_CHUNK)
        if exact_ties:
            tr = lax.broadcasted_iota(jnp.int32, (tc_w, tc_w), 0)
            tc = lax.broadcasted_iota(jnp.int32, (tc_w, tc_w), 1)
            prefix = jnp.where(tr <= tc, 1.0, 0.0).astype(MXU_DTYPE)

        def body(kt, taken):
            if exact_ties:
                hi = key_hi[kt].astype(jnp.int32)
                lo = key_lo[kt].astype(jnp.int32)
                t_hi = thr_hi.astype(jnp.int32)
                t_lo = thr_lo.astype(jnp.int32)
                in_band = hi == t_hi
                above = jnp.where(hi > t_hi, 1.0, jnp.where(in_band, jnp.where(lo > t_lo, 1.0, 0.0), 0.0))
                tie = jnp.where(in_band, jnp.where(lo == t_lo, 1.0, 0.0), 0.0)
                parts = []
                for c0 in range(0, tk, tc_w):
                    tie_c = tie[:, c0:c0 + tc_w]
                    rank = taken + _mm(tie_c, prefix)
                    parts.append(above[:, c0:c0 + tc_w] + jnp.where(rank <= need, tie_c, 0.0))
                    taken = rank[:, tc_w - 1:tc_w]
                sel = parts[0] if len(parts) == 1 else jnp.concatenate(parts, axis=1)
            else:
                sel = tie_band(kt, lambda lo: lo >= thr_lo)
                sel = jnp.where(key_hi[kt] > thr_hi, one16, sel).astype(f32)
            bias = (1.0 - sel) * NEG_BIG
            bias2 = jnp.concatenate([bias, bias], axis=0)
            start = pl.multiple_of(kt * tk, tk)
            pairs = range(DSA_HEADS // 2)
            slabs = [slice(pi * LANES, (pi + 1) * LANES) for pi in pairs]
            logits = [_mm_nt(q_pairs[pi], k_ref[0, pl.ds(start, tk), slabs[pi]]) + bias2 for pi in pairs]
            m_old = [m_s[pi] for pi in pairs]
            m_new = [jnp.maximum(m_old[pi], jnp.max(logits[pi], axis=1, keepdims=True)) for pi in pairs]
            alpha = [jnp.exp2(m_old[pi] - m_new[pi]) for pi in pairs]
            p = [jnp.exp2(logits[pi] - m_new[pi]) for pi in pairs]
            for pi in pairs:
                l_s[pi] = alpha[pi] * l_s[pi] + jnp.sum(p[pi], axis=1, keepdims=True)
                acc_s[pi] = alpha[pi] * acc_s[pi] + _mm(p[pi], v_ref[0, pl.ds(start, tk), slabs[pi]])
                m_s[pi] = m_new[pi]
            return taken

        lax.fori_loop(0, n_tiles, body, jnp.zeros((qb, 1), f32))
        for pi in range(DSA_HEADS // 2):
            out = acc_s[pi] / l_s[pi]
            o_ref[0, :, pi * LANES:(pi + 1) * LANES] = jnp.where(lower_half, out[:qb], out[qb:]).astype(o_ref.dtype)

    @pl.when(surplus == 0)
    def _():
        attend(False)

    @pl.when(surplus != 0)
    def _():
        attend(True)


def _dsa_call(qb_arr, qib_arr, sm, k_all, v_all, ki_all, pos0, n_keys, qb, tk):
    b, t, _ = qb_arr.shape
    lp = k_all.shape[1]
    assert lp % tk == 0 and t % qb == 0
    topk = min(TOPK_MAX, n_keys // 4)
    kern = functools.partial(_dsa_kernel, qb=qb, tk=tk, pos0=pos0, n_keys=n_keys, topk=topk)
    resident = dict(pipeline_mode=pl.Buffered(1))
    return pl.pallas_call(
        kern,
        out_shape=jax.ShapeDtypeStruct((b, t, DSA_W), MXU_DTYPE),
        grid=(b, t // qb),
        in_specs=[pl.BlockSpec((1, qb, DSA_W), lambda i, j: (i, j, 0)),
                  pl.BlockSpec((1, qb, IDX_W), lambda i, j: (i, j, 0)),
                  pl.BlockSpec((1, qb, LANES), lambda i, j: (i, j, 0)),
                  pl.BlockSpec((1, lp, DSA_W), lambda i, j: (i, 0, 0), **resident),
                  pl.BlockSpec((1, lp, DSA_W), lambda i, j: (i, 0, 0), **resident),
                  pl.BlockSpec((1, lp, LANES), lambda i, j: (i, 0, 0), **resident)],
        out_specs=pl.BlockSpec((1, qb, DSA_W), lambda i, j: (i, j, 0)),
        scratch_shapes=[pltpu.VMEM((lp // tk, qb, tk), jnp.int16),
                        pltpu.VMEM((lp // tk, qb, tk), jnp.int16),
                        pltpu.VMEM((DSA_HEADS // 2, 2 * qb, 1), f32),
                        pltpu.VMEM((DSA_HEADS // 2, 2 * qb, 1), f32),
                        pltpu.VMEM((DSA_HEADS // 2, 2 * qb, LANES), f32)],
        compiler_params=_params("parallel", "arbitrary"),
        name="dsa",
    )(qb_arr, qib_arr, sm, k_all, v_all, ki_all)


def _outproj_kernel(h_ref, og_ref, od_ref, w_ref, g_ref, b_ref, o_ref, *, alpha):
    mix = (jnp.dot(og_ref[...], w_ref[0:GDN_VW, :], preferred_element_type=f32)
           + jnp.dot(od_ref[...], w_ref[GDN_VW:, :], preferred_element_type=f32))
    o_ref[...] = _layer_norm(alpha * h_ref[...] + mix, g_ref[...], b_ref[...])


def _outproj_call(h, og, od, w, g, b, alpha):
    n, d = h.shape
    tm = _row_tile(n, 512)
    return pl.pallas_call(
        functools.partial(_outproj_kernel, alpha=alpha),
        out_shape=jax.ShapeDtypeStruct((n, d), f32),
        grid=(n // tm,),
        in_specs=[pl.BlockSpec((tm, d), lambda i: (i, 0)),
                  pl.BlockSpec((tm, GDN_VW), lambda i: (i, 0)),
                  pl.BlockSpec((tm, DSA_W), lambda i: (i, 0)),
                  pl.BlockSpec((GDN_VW + DSA_W, d), lambda i: (0, 0)),
                  pl.BlockSpec((1, d), lambda i: (0, 0)),
                  pl.BlockSpec((1, d), lambda i: (0, 0))],
        out_specs=pl.BlockSpec((tm, d), lambda i: (i, 0)),
        compiler_params=_params("parallel"),
        name="outproj_ln",
    )(h, og, od, w, g.reshape(1, d), b.reshape(1, d))


def _ffn_kernel(x_ref, wg_ref, wu_ref, wd_ref, g_ref, b_ref, o_ref, acc, *, alpha):
    j = pl.program_id(1)

    @pl.when(j == 0)
    def _():
        acc[...] = jnp.zeros(acc.shape, f32)

    xb = x_ref[...].astype(MXU_DTYPE)
    gate = jnp.dot(xb, wg_ref[...], preferred_element_type=f32)
    up = jnp.dot(xb, wu_ref[...], preferred_element_type=f32)
    act = gate * _sigmoid(gate) * up
    acc[...] += jnp.dot(act.astype(MXU_DTYPE), wd_ref[...], preferred_element_type=f32)

    @pl.when(j == pl.num_programs(1) - 1)
    def _():
        o_ref[...] = _layer_norm(alpha * x_ref[...] + acc[...], g_ref[...], b_ref[...])


def _ffn_call(x, wg, wu, wd, g, b, alpha):
    n, d = x.shape
    f = wg.shape[1]
    tm = _row_tile(n, 1024)
    tf = _row_tile(f, 512)
    return pl.pallas_call(
        functools.partial(_ffn_kernel, alpha=alpha),
        out_shape=jax.ShapeDtypeStruct((n, d), f32),
        grid=(n // tm, f // tf),
        in_specs=[pl.BlockSpec((tm, d), lambda i, j: (i, 0)),
                  pl.BlockSpec((d, tf), lambda i, j: (0, j)),
                  pl.BlockSpec((d, tf), lambda i, j: (0, j)),
                  pl.BlockSpec((tf, d), lambda i, j: (j, 0)),
                  pl.BlockSpec((1, d), lambda i, j: (0, 0)),
                  pl.BlockSpec((1, d), lambda i, j: (0, 0))],
        out_specs=pl.BlockSpec((tm, d), lambda i, j: (i, 0)),
        scratch_shapes=[pltpu.VMEM((tm, d), f32)],
        compiler_params=_params("parallel", "arbitrary"),
        name="ffn_ln",
    )(x, wg, wu, wd, g.reshape(1, d), b.reshape(1, d))


def _moe_plan_kernel(x_ref, wr_ref, plan_o, plan_t_o, xb_o, *, n_experts):
    tm = x_ref.shape[0]
    lane = lax.broadcasted_iota(jnp.int32, (tm, LANES), 1)
    x = x_ref[...]
    xb_o[...] = x.astype(xb_o.dtype)
    logits = _mm(x, wr_ref[...])
    logits = jnp.where(lane < n_experts, logits, -jnp.inf)
    m1 = jnp.max(logits, axis=1, keepdims=True)
    i1 = jnp.min(jnp.where(logits == m1, lane, LANES), axis=1, keepdims=True)
    rest = jnp.where(lane == i1, -jnp.inf, logits)
    m2 = jnp.max(rest, axis=1, keepdims=True)
    i2 = jnp.min(jnp.where(rest == m2, lane, LANES), axis=1, keepdims=True)
    e2 = jnp.exp(m2 - m1)
    w1 = 1.0 / (1.0 + e2)
    gates = jnp.where(lane == i1, w1, 0.0) + jnp.where(lane == i2, e2 * w1, 0.0)
    chosen = jnp.where(lane == i1, 1.0, 0.0) + jnp.where(lane == i2, 1.0, 0.0)
    blk = min(tm, MOE_PREFIX_BLOCK)
    r = lax.broadcasted_iota(jnp.int32, (blk, blk), 0)
    c = lax.broadcasted_iota(jnp.int32, (blk, blk), 1)
    tri = jnp.where(r >= c, 1.0, 0.0).astype(MXU_DTYPE)
    carry = jnp.zeros((1, LANES), f32)
    slots = []
    for b0 in range(0, tm, blk):
        m = chosen[b0:b0 + blk]
        inclusive = jnp.dot(tri, m.astype(MXU_DTYPE), preferred_element_type=f32) + carry
        slots.append(inclusive - m)
        carry = inclusive[blk - 1:blk, :]
    slot = jnp.concatenate(slots, axis=0) if len(slots) > 1 else slots[0]
    plan = jnp.where(lane < n_experts, gates,
                     jnp.where(lane < 2 * n_experts, pltpu.roll(slot, n_experts, 1),
                               jnp.where(lane < 3 * n_experts, pltpu.roll(chosen, 2 * n_experts, 1), 0.0)))
    plan_o[...] = plan
    plan_t_o[...] = plan.T[0:MOE_PLAN_ROWS, :]


def _moe_expert_kernel(xb_ref, plan_ref, plan_t_ref, wg_ref, wu_ref, wd_ref, y_ref,
                       xc, acc_c, *, n_experts, slot_block):
    e = pl.program_id(1)
    j = pl.program_id(2)
    tm = xb_ref.shape[0]
    cb = slot_block
    slot_row = plan_t_ref[pl.ds(e + n_experts, 1), :]
    chosen_row = plan_t_ref[pl.ds(e + 2 * n_experts, 1), :]
    count = jnp.max(jnp.where(chosen_row > 0.0, slot_row + 1.0, 0.0)).astype(jnp.int32)
    n_blocks = (count + cb - 1) // cb
    dot = functools.partial(jnp.dot, preferred_element_type=f32)

    def one_hot(sb):
        slot_id = (lax.broadcasted_iota(jnp.int32, (cb, 1), 0) + sb * cb).astype(f32)
        hit = jnp.where(chosen_row > 0.0, jnp.where(slot_row == slot_id, 1.0, 0.0), 0.0)
        return hit.astype(MXU_DTYPE)

    def rows_of(sb):
        return pl.ds(pl.multiple_of(sb * cb, cb), cb)

    @pl.when(jnp.logical_and(e == 0, j == 0))
    def _():
        y_ref[...] = jnp.zeros(y_ref.shape, f32)

    @pl.when(j == 0)
    def _():
        def gather(sb, carry):
            xc[rows_of(sb), :] = dot(one_hot(sb), xb_ref[...]).astype(xc.dtype)
            acc_c[rows_of(sb), :] = jnp.zeros((cb, acc_c.shape[1]), f32)
            return carry
        lax.fori_loop(0, n_blocks, gather, 0)

    def expert(sb, carry):
        xs = xc[rows_of(sb), :]
        gate = dot(xs, wg_ref[0])
        up = dot(xs, wu_ref[0])
        act = gate * _sigmoid(gate) * up
        acc_c[rows_of(sb), :] += dot(act.astype(MXU_DTYPE), wd_ref[0])
        return carry

    lax.fori_loop(0, n_blocks, expert, 0)

    @pl.when(j == pl.num_programs(2) - 1)
    def _():
        lane = lax.broadcasted_iota(jnp.int32, (tm, LANES), 1)
        gate_col = jnp.sum(jnp.where(lane == e, plan_ref[...], 0.0), axis=1, keepdims=True)

        def scatter(sb, carry):
            back = lax.dot_general(one_hot(sb), acc_c[rows_of(sb), :].astype(MXU_DTYPE),
                                   (((0,), (0,)), ((), ())), preferred_element_type=f32)
            y_ref[...] += gate_col * back
            return carry
        lax.fori_loop(0, n_blocks, scatter, 0)


def _residual_ln_kernel(x_ref, y_ref, g_ref, b_ref, o_ref, *, alpha):
    o_ref[...] = _layer_norm(alpha * x_ref[...] + y_ref[...], g_ref[...], b_ref[...])


def _moe_call(x, wr, wg, wu, wd, g, b, alpha):
    n, d = x.shape
    n_experts, _, f = wg.shape
    assert 3 * n_experts <= MOE_PLAN_ROWS
    tm = _row_tile(n, MOE_TOKEN_TILE)
    tf = _row_tile(f, 512)
    cb = min(MOE_SLOT_BLOCK, tm)
    cap = -(-tm // cb) * cb
    plan, plan_t, xb = pl.pallas_call(
        functools.partial(_moe_plan_kernel, n_experts=n_experts),
        out_shape=[jax.ShapeDtypeStruct((n, LANES), f32),
                   jax.ShapeDtypeStruct((MOE_PLAN_ROWS, n), f32),
                   jax.ShapeDtypeStruct((n, d), MXU_DTYPE)],
        grid=(n // tm,),
        in_specs=[pl.BlockSpec((tm, d), lambda i: (i, 0)),
                  pl.BlockSpec((d, LANES), lambda i: (0, 0))],
        out_specs=[pl.BlockSpec((tm, LANES), lambda i: (i, 0)),
                   pl.BlockSpec((MOE_PLAN_ROWS, tm), lambda i: (0, i)),
                   pl.BlockSpec((tm, d), lambda i: (i, 0))],
        compiler_params=_params("parallel"),
        name="moe_plan",
    )(x, wr)
    y = pl.pallas_call(
        functools.partial(_moe_expert_kernel, n_experts=n_experts, slot_block=cb),
        out_shape=jax.ShapeDtypeStruct((n, d), f32),
        grid=(n // tm, n_experts, f // tf),
        in_specs=[pl.BlockSpec((tm, d), lambda i, e, j: (i, 0)),
                  pl.BlockSpec((tm, LANES), lambda i, e, j: (i, 0)),
                  pl.BlockSpec((MOE_PLAN_ROWS, tm), lambda i, e, j: (0, i)),
                  pl.BlockSpec((1, d, tf), lambda i, e, j: (e, 0, j)),
                  pl.BlockSpec((1, d, tf), lambda i, e, j: (e, 0, j)),
                  pl.BlockSpec((1, tf, d), lambda i, e, j: (e, j, 0))],
        out_specs=pl.BlockSpec((tm, d), lambda i, e, j: (i, 0)),
        scratch_shapes=[pltpu.VMEM((cap, d), MXU_DTYPE),
                        pltpu.VMEM((cap, d), f32)],
        compiler_params=_params("parallel", "arbitrary", "arbitrary"),
        name="moe_experts",
    )(xb, plan, plan_t, wg, wu, wd)
    tr = _row_tile(n, 1024)
    return pl.pallas_call(
        functools.partial(_residual_ln_kernel, alpha=alpha),
        out_shape=jax.ShapeDtypeStruct((n, d), f32),
        grid=(n // tr,),
        in_specs=[pl.BlockSpec((tr, d), lambda i: (i, 0)),
                  pl.BlockSpec((tr, d), lambda i: (i, 0)),
                  pl.BlockSpec((1, d), lambda i: (0, 0)),
                  pl.BlockSpec((1, d), lambda i: (0, 0))],
        out_specs=pl.BlockSpec((tr, d), lambda i: (i, 0)),
        compiler_params=_params("parallel"),
        name="moe_residual_ln",
    )(x, y, g.reshape(1, d), b.reshape(1, d))


def _pack_w_in(w_in):
    o_z = GDN_CONV_CH
    o_b = o_z + GDN_VW
    o_a = o_b + GDN_HEADS
    o_qd = o_a + GDN_HEADS
    o_kd = o_qd + DSA_W
    o_vd = o_kd + DSA_W
    o_qi = o_vd + DSA_W
    o_ki = o_qi + IDX_W
    o_wi = o_ki + IDX_DIM
    assert w_in.shape[-1] == o_wi + IDX_HEADS
    sl = lambda a, n: w_in[..., a:a + n]
    pad = jnp.zeros(w_in.shape[:-1] + (LANES - 2 * GDN_HEADS - IDX_HEADS,), w_in.dtype)
    packed = jnp.concatenate([
        sl(0, GDN_CONV_CH), sl(o_z, GDN_VW), sl(o_qd, DSA_W), sl(o_kd, DSA_W), sl(o_vd, DSA_W),
        sl(o_qi, IDX_W), sl(o_ki, IDX_DIM), sl(o_ki, IDX_DIM),
        sl(o_b, GDN_HEADS), sl(o_a, GDN_HEADS), sl(o_wi, IDX_HEADS), pad], axis=-1)
    assert packed.shape[-1] == C_END
    return packed.astype(MXU_DTYPE)


def _rope_tables(pos):
    half = DSA_DH // 8
    inv_freq = ROPE_THETA ** (-jnp.arange(half, dtype=f32) / half)
    ang = pos.astype(f32)[:, None] * inv_freq[None, :]
    cos, sin = jnp.cos(ang), jnp.sin(ang)
    t = pos.shape[0]
    ones = jnp.ones((t, DSA_DH - 2 * half), f32)
    zeros = jnp.zeros((t, DSA_DH - 2 * half), f32)
    z8 = jnp.zeros((t, half), f32)
    cos_t = jnp.concatenate([cos, cos, ones], -1)
    lo = jnp.concatenate([-sin, z8, zeros], -1)
    hi = jnp.concatenate([z8, sin, zeros], -1)
    return tuple(jnp.tile(a, (1, LANES // DSA_DH)) for a in (cos_t, lo, hi))


def _gdn_params(a_log, dt_bias):
    row = lambda v: jnp.zeros((LANES,), f32).at[SM_DECAY:SM_DECAY + GDN_HEADS].set(v.astype(f32))
    return jnp.stack([row(a_log), row(dt_bias)])


def _trunk(x, pos0, conv_bufs, s0s, past_k, past_v, past_ik, chunk, wts, depth):
    b, t, d = x.shape
    n = b * t
    alpha = (2.0 * depth) ** 0.25
    tabs = _rope_tables(pos0 + jnp.arange(t))
    past = 0 if past_k is None else past_k.shape[2]
    n_keys = past + t
    assert t >= CONV_W - 1
    qb = min(t, DSA_Q_BLOCK)
    tk = DSA_KEY_TILE
    lp = -(-n_keys // tk) * tk

    def with_past(new, old, l):
        parts = [] if old is None else [old[l].reshape(b, past, -1).astype(MXU_DTYPE)]
        parts.append(new.reshape(b, t, -1))
        if lp > n_keys:
            parts.append(jnp.zeros((b, lp - n_keys, new.shape[-1]), MXU_DTYPE))
        return parts[0] if len(parts) == 1 else jnp.concatenate(parts, axis=1)

    h = _ln_call(x.reshape(n, d), wts["ln_in_g"], wts["ln_in_b"])
    new_k, new_v, new_ik, new_conv, new_s = [], [], [], [], []
    for l in range(depth):
        (qkv, z, k_f, v_f, ki_f, q_b, k_b, v_b, qi_b, ki_b, sm) = _inproj_call(h, wts["w_in"][l], tabs, t)
        qkv3 = qkv.reshape(b, t, GDN_CONV_CH)
        buf8 = jnp.concatenate([jnp.zeros((b, SUBLANES - (CONV_W - 1), GDN_CONV_CH), f32), conv_bufs[l]], axis=1)
        o_g, s_new = _gdn_call(qkv3, z.reshape(b, t, GDN_VW), sm.reshape(b, t, LANES), wts["conv_w"][l],
                               buf8, s0s[l], _gdn_params(wts["gdn_a_log"][l], wts["gdn_dt_bias"][l]),
                               wts["gdn_norm_w"][l].reshape(1, GDN_DV), chunk)
        ki_dup = None if past_ik is None else jnp.concatenate([past_ik, past_ik], axis=-1)
        o_d = _dsa_call(q_b.reshape(b, t, DSA_W), qi_b.reshape(b, t, IDX_W), sm.reshape(b, t, LANES),
                        with_past(k_b, past_k, l), with_past(v_b, past_v, l), with_past(ki_b, ki_dup, l),
                        past, n_keys, qb, tk)
        h = _outproj_call(h, o_g.reshape(n, GDN_VW), o_d.reshape(n, DSA_W), wts["w_out"][l],
                          wts["ln1_g"][l], wts["ln1_b"][l], alpha)
        if l % 2 == 0:
            i = l // 2
            h = _ffn_call(h, wts["ffn_wg"][i], wts["ffn_wu"][i], wts["ffn_wd"][i],
                          wts["ln2_g"][l], wts["ln2_b"][l], alpha)
        else:
            i = l // 2
            h = _moe_call(h, wts["moe_router"][i], wts["moe_wg"][i], wts["moe_wu"][i], wts["moe_wd"][i],
                          wts["ln2_g"][l], wts["ln2_b"][l], alpha)
        new_k.append(k_f.reshape(b, t, DSA_HEADS, DSA_DH))
        new_v.append(v_f.reshape(b, t, DSA_HEADS, DSA_DH))
        new_ik.append(ki_f.reshape(b, t, IDX_DIM))
        new_conv.append(qkv3[:, t - (CONV_W - 1):])
        new_s.append(s_new)
    return (h.reshape(b, t, d), jnp.stack(new_k), jnp.stack(new_v), jnp.stack(new_ik),
            jnp.stack(new_s), jnp.stack(new_conv))


def kernel(x_prompt, x_sample, cache_k, cache_v, cache_idx_k, state_gdn, state_conv, ln_in_g, ln_in_b, w_in, conv_w, gdn_a_log, gdn_dt_bias, gdn_norm_w, w_out, ln1_g, ln1_b, ln2_g, ln2_b, ffn_wg, ffn_wu, ffn_wd, moe_router, moe_wg, moe_wu, moe_wd):
    depth = w_in.shape[0]
    n_experts = moe_router.shape[-1]
    bp, tp, d = x_prompt.shape
    cast = lambda a: a.astype(MXU_DTYPE)
    router = jnp.concatenate(
        [moe_router.astype(f32), jnp.zeros(moe_router.shape[:-1] + (LANES - n_experts,), f32)], axis=-1)
    wts = dict(ln_in_g=ln_in_g, ln_in_b=ln_in_b, w_in=_pack_w_in(w_in), conv_w=conv_w,
               gdn_a_log=gdn_a_log, gdn_dt_bias=gdn_dt_bias, gdn_norm_w=gdn_norm_w, w_out=cast(w_out),
               ln1_g=ln1_g, ln1_b=ln1_b, ln2_g=ln2_g, ln2_b=ln2_b,
               ffn_wg=cast(ffn_wg), ffn_wu=cast(ffn_wu), ffn_wd=cast(ffn_wd),
               moe_router=router, moe_wg=cast(moe_wg), moe_wu=cast(moe_wu), moe_wd=cast(moe_wd))
    zero_conv = jnp.zeros((depth, bp, CONV_W - 1, GDN_CONV_CH), f32)
    zero_s = jnp.zeros((depth, bp, GDN_HEADS, GDN_DK, GDN_DV), f32)
    y_p, k_p, v_p, ik_p, s_p, conv_p = _trunk(
        x_prompt, 0, zero_conv, zero_s, None, None, None, CHUNK, wts, depth)
    ts = x_sample.shape[1]
    past = cache_k.shape[2]
    y_s, k_s, v_s, ik_s, s_s, conv_s = _trunk(
        x_sample, past, state_conv, state_gdn, cache_k, cache_v, cache_idx_k, ts, wts, depth)
    return (y_p, y_s, k_p, v_p, ik_p, s_p, conv_p, k_s, v_s, ik_s, s_s, conv_s)
```

```python
import functools

import jax
import jax.numpy as jnp
from jax import lax
from jax.experimental import pallas as pl
from jax.experimental.pallas import tpu as pltpu

CHUNK = 64
CONV_W = 4
GDN_HEADS = 4
GDN_DK = 128
GDN_DV = 128
DSA_HEADS = 8
DSA_DH = 64
IDX_HEADS = 4
IDX_DIM = 64
TOPK_MAX = 256
ROPE_THETA = 500000.0
TOP_K_EXPERTS = 2
LN_EPS = 1e-5
RMS_EPS = 1e-6
L2_EPS = 1e-6

GDN_QK = GDN_HEADS * GDN_DK
GDN_VW = GDN_HEADS * GDN_DV
GDN_CONV_CH = 2 * GDN_QK + GDN_VW
DSA_W = DSA_HEADS * DSA_DH
IDX_W = IDX_HEADS * IDX_DIM

LANES = 128
SUBLANES = 8
VMEM_LIMIT = 56 * 1024 * 1024

DSA_Q_BLOCK = 256
DSA_KEY_TILE = 2048
DSA_SEARCH_TILE = 1024
GDN_CHUNKS_PER_STEP = 4
MOE_TOKEN_TILE = 2048
MOE_SLOT_BLOCK = 576
MOE_PREFIX_BLOCK = 256
MOE_PLAN_ROWS = 24

MXU_DTYPE = jnp.bfloat16

C_QKV = 0
C_Z = C_QKV + GDN_CONV_CH
C_QD = C_Z + GDN_VW
C_KD = C_QD + DSA_W
C_VD = C_KD + DSA_W
C_QI = C_VD + DSA_W
C_KI = C_QI + IDX_W
C_SM = C_KI + LANES
C_END = C_SM + LANES
SM_BETA = 0
SM_DECAY = GDN_HEADS
SM_WI = 2 * GDN_HEADS

INT_MIN = -2 ** 31
HALF_BIAS = 2 ** 15
TIE_CHUNK = 512
NEG_BIG = -3e38
M_FLOOR = -1e38
LOG2E = 1.4426950408889634

f32 = jnp.float32


def _mm(a, b):
    return jnp.dot(a.astype(MXU_DTYPE), b.astype(MXU_DTYPE), preferred_element_type=f32)


def _mm_nt(a, b):
    return lax.dot_general(a.astype(MXU_DTYPE), b.astype(MXU_DTYPE), (((1,), (1,)), ((), ())),
                           preferred_element_type=f32)


def _mm_tn(a, b):
    return lax.dot_general(a.astype(MXU_DTYPE), b.astype(MXU_DTYPE), (((0,), (0,)), ((), ())),
                           preferred_element_type=f32)


def _split(x):
    hi = x.astype(MXU_DTYPE)
    lo = (x - hi.astype(f32)).astype(MXU_DTYPE)
    return hi, lo


def _sigmoid(x):
    return 1.0 / (1.0 + jnp.exp(-x))


def _layer_norm(r, g, b):
    mu = jnp.mean(r, axis=-1, keepdims=True)
    xc = r - mu
    var = jnp.mean(xc * xc, axis=-1, keepdims=True)
    return xc * lax.rsqrt(var + LN_EPS) * g + b


def _params(*sem):
    return pltpu.CompilerParams(dimension_semantics=sem, vmem_limit_bytes=VMEM_LIMIT)


def _row_tile(n, want):
    t = min(n, want)
    assert n % t == 0, (n, t)
    return t


def _ln_kernel(x_ref, g_ref, b_ref, o_ref):
    o_ref[...] = _layer_norm(x_ref[...], g_ref[...], b_ref[...])


def _ln_call(x, g, b):
    n, d = x.shape
    tm = _row_tile(n, 1024)
    return pl.pallas_call(
        _ln_kernel,
        out_shape=jax.ShapeDtypeStruct((n, d), f32),
        grid=(n // tm,),
        in_specs=[pl.BlockSpec((tm, d), lambda i: (i, 0)),
                  pl.BlockSpec((1, d), lambda i: (0, 0)),
                  pl.BlockSpec((1, d), lambda i: (0, 0))],
        out_specs=pl.BlockSpec((tm, d), lambda i: (i, 0)),
        compiler_params=_params("parallel"),
        name="ln_in",
    )(x, g.reshape(1, d), b.reshape(1, d))


def _rope(x, cos_t, sin_lo, sin_hi):
    w = x.shape[1]
    reps = w // LANES
    c = jnp.tile(cos_t, (1, reps))
    lo = jnp.tile(sin_lo, (1, reps))
    hi = jnp.tile(sin_hi, (1, reps))
    half = DSA_DH // 8
    from_above = pltpu.roll(x, w - half, 1)
    from_below = pltpu.roll(x, half, 1)
    return x * c + from_above * lo + from_below * hi


def _inproj_kernel(x_ref, w_ref, cos_ref, lo_ref, hi_ref,
                   qkv_o, z_o, k_o, v_o, ki_o, qb_o, kb_o, vb_o, qib_o, kib_o, sm_o):
    xb = x_ref[...].astype(MXU_DTYPE)
    cos_t, lo, hi = cos_ref[...], lo_ref[...], hi_ref[...]

    def proj(c0, c1):
        return jnp.dot(xb, w_ref[:, c0:c1], preferred_element_type=f32)

    qkv_o[...] = proj(C_QKV, C_Z)
    z_o[...] = proj(C_Z, C_QD)
    q = _rope(proj(C_QD, C_KD), cos_t, lo, hi) * (DSA_DH ** -0.5 * LOG2E)
    qb_o[...] = q.astype(qb_o.dtype)
    k = _rope(proj(C_KD, C_VD), cos_t, lo, hi)
    k_o[...] = k
    kb_o[...] = k.astype(kb_o.dtype)
    v = proj(C_VD, C_QI)
    v_o[...] = v
    vb_o[...] = v.astype(vb_o.dtype)
    qi = _rope(proj(C_QI, C_KI), cos_t, lo, hi) * (IDX_DIM ** -0.5)
    qib_o[...] = qi.astype(qib_o.dtype)
    ki = _rope(proj(C_KI, C_SM), cos_t, lo, hi)
    ki_o[...] = ki[:, :IDX_DIM]
    kib_o[...] = ki.astype(kib_o.dtype)
    sm_o[...] = proj(C_SM, C_END)


def _inproj_call(h, w_packed, tabs, seq_len):
    n, d = h.shape
    tm = _row_tile(n, 512)
    cos_t, lo, hi = tabs
    if tm > seq_len:
        assert tm % seq_len == 0
        cos_t, lo, hi = (jnp.tile(a, (tm // seq_len, 1)) for a in (cos_t, lo, hi))
        nblk = 1
    else:
        assert seq_len % tm == 0
        nblk = seq_len // tm
    tab_spec = pl.BlockSpec((tm, LANES), lambda i: (i % nblk, 0))

    def rows(w):
        return pl.BlockSpec((tm, w), lambda i: (i, 0))

    widths = [(GDN_CONV_CH, f32), (GDN_VW, f32), (DSA_W, f32), (DSA_W, f32), (IDX_DIM, f32),
              (DSA_W, MXU_DTYPE), (DSA_W, MXU_DTYPE), (DSA_W, MXU_DTYPE), (IDX_W, MXU_DTYPE),
              (LANES, MXU_DTYPE), (LANES, f32)]
    return pl.pallas_call(
        _inproj_kernel,
        out_shape=[jax.ShapeDtypeStruct((n, w), dt) for w, dt in widths],
        grid=(n // tm,),
        in_specs=[rows(d),
                  pl.BlockSpec((d, C_END), lambda i: (0, 0), pipeline_mode=pl.Buffered(1)),
                  tab_spec, tab_spec, tab_spec],
        out_specs=[rows(w) for w, _ in widths],
        compiler_params=_params("parallel"),
        name="inproj",
    )(h, w_packed, cos_t, lo, hi)


def _wide_to_blocks(w, chunk):
    size = w.shape[1]
    r = lax.broadcasted_iota(jnp.int32, (size, size), 0)
    c = lax.broadcasted_iota(jnp.int32, (size, size), 1)
    tiled = jnp.concatenate([w] * (size // chunk), axis=0)
    return jnp.where(r // chunk == c // chunk, tiled, jnp.zeros_like(tiled))


def _blocks_to_wide(sq, chunk):
    size = sq.shape[1]
    blk = lax.broadcasted_iota(jnp.int32, (chunk, size), 1) // chunk
    out = jnp.zeros((chunk, size), sq.dtype)
    for h in range(size // chunk):
        out = jnp.where(blk == h, sq[h * chunk:(h + 1) * chunk, :], out)
    return out


def _wide_matmul(x, p, chunk):
    x_hi, x_lo = _split(x)
    p_hi, p_lo = _split(p)
    dot = functools.partial(jnp.dot, preferred_element_type=f32)
    both = dot(jnp.concatenate([x_hi, x_lo], axis=0), _wide_to_blocks(p_hi, chunk))
    return both[:chunk] + both[chunk:] + dot(x_hi, _wide_to_blocks(p_lo, chunk))


def _unit_lower_inverse_wide(a_list, chunk):
    size = a_list[0].shape[1]
    r = lax.broadcasted_iota(jnp.int32, (chunk, size), 0)
    c = lax.broadcasted_iota(jnp.int32, (chunk, size), 1) % chunk

    def same_block(n):
        return (r // n) == (c // n)

    base = SUBLANES
    eye = jnp.where(r == c, 1.0, 0.0)
    a0 = [jnp.where(same_block(base), a, 0.0) for a in a_list]
    p = [_wide_matmul(m, m, chunk) for m in a0]
    x = [eye - m for m in a0]
    x = [xi + _wide_matmul(xi, pi, chunk) for xi, pi in zip(x, p)]
    p = [_wide_matmul(pi, pi, chunk) for pi in p]
    x = [xi + _wide_matmul(xi, pi, chunk) for xi, pi in zip(x, p)]
    n = base
    while n < chunk:
        off = [jnp.where(same_block(2 * n), jnp.where(same_block(n), 0.0, a), 0.0) for a in a_list]
        t = [_wide_matmul(xi, oi, chunk) for xi, oi in zip(x, off)]
        x = [xi - _wide_matmul(ti, xi, chunk) for xi, ti in zip(x, t)]
        n *= 2
    return x


def _chunk_cumsum(g, chunk):
    rows = g.shape[0]
    r = lax.broadcasted_iota(jnp.int32, (rows, rows), 0)
    c = lax.broadcasted_iota(jnp.int32, (rows, rows), 1)
    tri = jnp.where(jnp.logical_and(r // chunk == c // chunk, r >= c), 1.0, 0.0).astype(MXU_DTYPE)
    dot = functools.partial(jnp.dot, preferred_element_type=f32)
    g1 = g.astype(MXU_DTYPE)
    rem = g - g1.astype(f32)
    g2 = rem.astype(MXU_DTYPE)
    g3 = (rem - g2.astype(f32)).astype(MXU_DTYPE)
    return dot(tri, g1) + (dot(tri, g2) + dot(tri, g3))


def _gdn_kernel(qkv_ref, z_ref, sm_ref, cw_ref, cb_ref, s0_ref, par_ref, nw_ref,
                o_ref, s_ref, ubuf, *, chunk, group):
    step = pl.program_id(1)
    rows = chunk * group

    @pl.when(step == 0)
    def _():
        ubuf[0:SUBLANES, :] = cb_ref[0]
        s_ref[0] = s0_ref[0]

    ubuf[SUBLANES:SUBLANES + rows, :] = qkv_ref[0]
    y = ubuf[SUBLANES - 3:SUBLANES - 3 + rows, :] * cw_ref[0:1, :]
    for i in range(1, CONV_W):
        y = y + ubuf[SUBLANES - 3 + i:SUBLANES - 3 + i + rows, :] * cw_ref[i:i + 1, :]
    ubuf[0:SUBLANES, :] = ubuf[rows:rows + SUBLANES, :]
    y = y * _sigmoid(y)

    sm = sm_ref[0]
    beta_all = _sigmoid(sm)
    xg = sm + par_ref[1:2, :]
    softplus = jnp.maximum(xg, 0.0) + jnp.log1p(jnp.exp(-jnp.abs(xg)))
    g_all = -jnp.exp(par_ref[0:1, :]) * softplus
    gc_all = _chunk_cumsum(g_all, chunk)
    gc_rows = gc_all.T

    heads = GDN_HEADS
    size = heads * chunk
    w_row = lax.broadcasted_iota(jnp.int32, (chunk, size), 0)
    w_lane = lax.broadcasted_iota(jnp.int32, (chunk, size), 1)
    w_col = w_lane % chunk
    w_head = w_lane // chunk
    causal_w = w_row >= w_col
    strict_w = w_row > w_col
    z = z_ref[0]
    nw = nw_ref[...]
    dot = functools.partial(jnp.dot, preferred_element_type=f32)

    def stack(a, rs, col0, width):
        return jnp.concatenate([a[rs, col0 + h * width:col0 + (h + 1) * width] for h in range(heads)], axis=0)

    def column_stack(a, rs, lane0):
        return jnp.concatenate([a[rs, lane0 + h:lane0 + h + 1] for h in range(heads)], axis=0)

    def column_wide(a, rs, lane0):
        out = jnp.zeros((chunk, size), f32)
        for h in range(heads):
            out = jnp.where(w_head == h, a[rs, lane0 + h:lane0 + h + 1], out)
        return out

    pre = []
    for g in range(group):
        rs = slice(g * chunk, (g + 1) * chunk)
        last = (g + 1) * chunk - 1
        q = stack(y, rs, 0, GDN_DK)
        k = stack(y, rs, GDN_QK, GDN_DK)
        v = stack(y, rs, 2 * GDN_QK, GDN_DV)
        q = q * lax.rsqrt(jnp.sum(q * q, -1, keepdims=True) + L2_EPS) * (GDN_DK ** -0.5)
        k = k * lax.rsqrt(jnp.sum(k * k, -1, keepdims=True) + L2_EPS)
        beta_s = column_stack(beta_all, rs, SM_BETA)
        gc_s = column_stack(gc_all, rs, SM_DECAY)
        g_last = [gc_all[last:last + 1, SM_DECAY + h:SM_DECAY + h + 1] for h in range(heads)]
        g_last_s = jnp.concatenate([jnp.broadcast_to(gl, (chunk, 1)) for gl in g_last], axis=0)
        gc_row_w = jnp.concatenate([gc_rows[SM_DECAY + h:SM_DECAY + h + 1, rs] for h in range(heads)], axis=1)
        decay_w = jnp.exp(jnp.where(causal_w, column_wide(gc_all, rs, SM_DECAY) - gc_row_w, -jnp.inf))
        kk_w = _blocks_to_wide(_mm_nt(k, k), chunk)
        a_w = jnp.where(strict_w, column_wide(beta_all, rs, SM_BETA) * kk_w * decay_w, 0.0)
        pre.append(dict(q=q, k=k, v=v, beta_s=beta_s, gc_s=gc_s, g_last=g_last, g_last_s=g_last_s,
                        decay_w=decay_w, a_w=a_w))
    t_ws = _unit_lower_inverse_wide([pc["a_w"] for pc in pre], chunk)
    local = []
    for pc, t_w in zip(pre, t_ws):
        q, k, v, beta_s, gc_s = pc["q"], pc["k"], pc["v"], pc["beta_s"], pc["gc_s"]
        rhs = jnp.concatenate([v * beta_s, k * (beta_s * jnp.exp(gc_s))], axis=-1)
        t_hi, t_lo = _split(t_w)
        r_hi, r_lo = _split(rhs)
        tb_hi = _wide_to_blocks(t_hi, chunk)
        both = dot(jnp.concatenate([tb_hi, _wide_to_blocks(t_lo, chunk)], axis=0), r_hi)
        sol = both[:size] + both[size:] + dot(tb_hi, r_lo)
        local.append(dict(u0=sol[:, :GDN_DV], wk=sol[:, GDN_DV:],
                          qk=_mm_nt(q, k) * _wide_to_blocks(pc["decay_w"], chunk),
                          q_dec=q * jnp.exp(gc_s), k_dec=k * jnp.exp(pc["g_last_s"] - gc_s), g_last=pc["g_last"]))

    s = [s_ref[0, h] for h in range(heads)]
    for g in range(group):
        rs = slice(g * chunk, (g + 1) * chunk)
        lc = local[g]
        u_parts, o_parts = [], []
        for h in range(heads):
            hs = slice(h * chunk, (h + 1) * chunk)
            both = _mm(jnp.concatenate([lc["wk"][hs], lc["q_dec"][hs]], axis=0), s[h])
            u_parts.append(lc["u0"][hs] - both[:chunk])
            o_parts.append(both[chunk:])
        u = jnp.concatenate(u_parts, axis=0)
        o = jnp.concatenate(o_parts, axis=0) + _mm(lc["qk"], u)
        for h in range(heads):
            hs = slice(h * chunk, (h + 1) * chunk)
            s[h] = s[h] * jnp.exp(lc["g_last"][h]) + _mm_tn(lc["k_dec"][hs], u[hs])
        z_s = stack(z, rs, 0, GDN_DV)
        o = o * lax.rsqrt(jnp.mean(o * o, -1, keepdims=True) + RMS_EPS) * nw * (z_s * _sigmoid(z_s))
        for h in range(heads):
            o_ref[0, rs, h * GDN_DV:(h + 1) * GDN_DV] = o[h * chunk:(h + 1) * chunk].astype(o_ref.dtype)
    for h in range(heads):
        s_ref[0, h] = s[h]


def _gdn_call(qkv, z, sm, conv_w, conv_buf8, s0, par, norm_w, chunk):
    b, t, _ = qkv.shape
    group = min(GDN_CHUNKS_PER_STEP, t // chunk)
    rows = chunk * group
    assert t % rows == 0
    return pl.pallas_call(
        functools.partial(_gdn_kernel, chunk=chunk, group=group),
        out_shape=[jax.ShapeDtypeStruct((b, t, GDN_VW), MXU_DTYPE),
                   jax.ShapeDtypeStruct((b, GDN_HEADS, GDN_DK, GDN_DV), f32)],
        grid=(b, t // rows),
        in_specs=[pl.BlockSpec((1, rows, GDN_CONV_CH), lambda i, j: (i, j, 0)),
                  pl.BlockSpec((1, rows, GDN_VW), lambda i, j: (i, j, 0)),
                  pl.BlockSpec((1, rows, LANES), lambda i, j: (i, j, 0)),
                  pl.BlockSpec((CONV_W, GDN_CONV_CH), lambda i, j: (0, 0)),
                  pl.BlockSpec((1, SUBLANES, GDN_CONV_CH), lambda i, j: (i, 0, 0)),
                  pl.BlockSpec((1, GDN_HEADS, GDN_DK, GDN_DV), lambda i, j: (i, 0, 0, 0)),
                  pl.BlockSpec((2, LANES), lambda i, j: (0, 0)),
                  pl.BlockSpec((1, GDN_DV), lambda i, j: (0, 0))],
        out_specs=[pl.BlockSpec((1, rows, GDN_VW), lambda i, j: (i, j, 0)),
                   pl.BlockSpec((1, GDN_HEADS, GDN_DK, GDN_DV), lambda i, j: (i, 0, 0, 0))],
        scratch_shapes=[pltpu.VMEM((SUBLANES + rows, GDN_CONV_CH), f32)],
        compiler_params=_params("parallel", "arbitrary"),
        name="gdn",
    )(qkv, z, sm, conv_w, conv_buf8, s0, par, norm_w)


def _dsa_kernel(q_ref, qi_ref, sm_ref, k_ref, v_ref, ki_ref, o_ref,
                key_hi, key_lo, m_s, l_s, acc_s, *, qb, tk, ts, pos0, n_keys, topk):
    blk = pl.program_id(1)
    row = lax.broadcasted_iota(jnp.int32, (qb, 1), 0)
    q_pos = pos0 + blk * qb + row
    limit = jnp.minimum((q_pos // CHUNK + 1) * CHUNK, n_keys)
    last_limit = jnp.minimum(((pos0 + (blk + 1) * qb - 1) // CHUNK + 1) * CHUNK, n_keys)
    n_tiles = (last_limit + tk - 1) // tk
    n_stiles = (last_limit + ts - 1) // ts
    sub = tk // ts
    lane = lax.broadcasted_iota(jnp.int32, (qb, LANES), 1)
    lower_half = lane < DSA_DH
    col = lax.broadcasted_iota(jnp.int32, (qb, ts), 1)
    groups = ts // LANES

    sm = sm_ref[0]
    qi_heads = []
    for h in range(IDX_HEADS):
        pair = qi_ref[0, :, (h // 2) * LANES:(h // 2 + 1) * LANES]
        keep = lower_half if h % 2 == 0 else jnp.logical_not(lower_half)
        qi_heads.append(jnp.where(keep, pair, jnp.zeros_like(pair)))
    wi = [sm[:, SM_WI + h:SM_WI + h + 1] * (IDX_HEADS ** -0.5) for h in range(IDX_HEADS)]

    def score_tile(kt, carry):
        ki_t = ki_ref[0, pl.ds(pl.multiple_of(kt * ts, ts), ts), :]
        score = jnp.zeros((qb, ts), f32)
        for h in range(IDX_HEADS):
            score = score + jnp.maximum(_mm_nt(qi_heads[h], ki_t), 0.0) * wi[h]
        bits = pltpu.bitcast(score, jnp.int32)
        key = bits ^ ((bits >> 31) & 0x7FFFFFFF)
        key = jnp.where(kt * ts + col < limit, key, INT_MIN)
        key_hi[kt] = (key >> 16).astype(jnp.int16)
        key_lo[kt] = ((key & 0xFFFF) - HALF_BIAS).astype(jnp.int16)
        return carry

    lax.fori_loop(0, n_stiles, score_tile, 0)

    def fill_lowest(kt, carry):
        lowest = jnp.full((qb, ts), -HALF_BIAS, jnp.int16)
        key_hi[kt] = lowest
        key_lo[kt] = lowest
        return carry

    lax.fori_loop(n_stiles, n_tiles * sub, fill_lowest, 0)

    one16, zero16 = jnp.int16(1), jnp.int16(0)

    def lane_fold(h16):
        part = h16[:, 0:LANES]
        for g in range(1, groups):
            part = part + h16[:, g * LANES:(g + 1) * LANES]
        return part

    def row_total(acc):
        return jnp.sum(acc.astype(jnp.int32), axis=1, keepdims=True)

    zero_acc = jnp.zeros((qb, LANES), jnp.int16)

    def count(hit):
        return row_total(lax.fori_loop(0, n_stiles, lambda kt, acc: acc + lane_fold(hit(kt)), zero_acc))

    def kth_largest_half(ref, want):
        def bit_step(i, u):
            cand = u | (jnp.int32(1) << (15 - i))
            cand16 = (cand - HALF_BIAS).astype(jnp.int16)
            n_ge_cand = count(lambda kt: jnp.where(ref[kt] >= cand16, one16, zero16))
            return jnp.where(n_ge_cand >= want, cand, u)
        return lax.fori_loop(0, 16, bit_step, jnp.zeros((qb, 1), jnp.int32))

    hi_u = kth_largest_half(key_hi, topk)
    thr_hi = (hi_u - HALF_BIAS).astype(jnp.int16)

    def mask_low(kt, acc):
        hi = key_hi[kt]
        key_lo[kt] = jnp.where(hi == thr_hi, key_lo[kt], jnp.int16(-HALF_BIAS))
        return acc + lane_fold(jnp.where(hi > thr_hi, one16, zero16))

    n_hi_gt = row_total(lax.fori_loop(0, n_stiles, mask_low, zero_acc))
    lo_u = kth_largest_half(key_lo, topk - n_hi_gt)
    lo_u = jnp.where(jnp.logical_and(hi_u == 0, lo_u == 0), 1, lo_u)
    thr_lo = (lo_u - HALF_BIAS).astype(jnp.int16)

    def tie_band(kt, low_pred):
        return jnp.where(key_hi[kt] == thr_hi, jnp.where(low_pred(key_lo[kt]), one16, zero16), zero16)


    def band_counts(kt, accs):
        in_band = key_hi[kt] == thr_hi
        lo = key_lo[kt]
        gt = jnp.where(in_band, jnp.where(lo > thr_lo, one16, zero16), zero16)
        ge = jnp.where(in_band, jnp.where(lo >= thr_lo, one16, zero16), zero16)
        return accs[0] + lane_fold(gt), accs[1] + lane_fold(ge)

    acc_gt, acc_ge = lax.fori_loop(0, n_stiles, band_counts, (zero_acc, zero_acc))
    n_gt = n_hi_gt + row_total(acc_gt)
    n_ge = n_hi_gt + row_total(acc_ge)
    need = (topk - n_gt).astype(f32)
    surplus = jnp.max(jnp.where(n_ge > topk, 1, 0))

    q_pairs = []
    for pi in range(DSA_HEADS // 2):
        pair = q_ref[0, :, pi * LANES:(pi + 1) * LANES]
        zero = jnp.zeros_like(pair)
        q_pairs.append(jnp.concatenate([jnp.where(lower_half, pair, zero), jnp.where(lower_half, zero, pair)], axis=0))

    def attend(exact_ties):
        m_s[...] = jnp.full(m_s.shape, M_FLOOR, f32)
        l_s[...] = jnp.zeros(l_s.shape, f32)
        acc_s[...] = jnp.zeros(acc_s.shape, f32)
        tc_w = min(tk, TIE_CHUNK)
        if exact_ties:
            tr = lax.broadcasted_iota(jnp.int32, (tc_w, tc_w), 0)
            tc = lax.broadcasted_iota(jnp.int32, (tc_w, tc_w), 1)
            prefix = jnp.where(tr <= tc, 1.0, 0.0).astype(MXU_DTYPE)

        def body(kt, taken):
            pieces = [kt * sub + i for i in range(sub)]
            if exact_ties:
                hi = jnp.concatenate([key_hi[st] for st in pieces], axis=1).astype(jnp.int32)
                lo = jnp.concatenate([key_lo[st] for st in pieces], axis=1).astype(jnp.int32)
                t_hi = thr_hi.astype(jnp.int32)
                t_lo = thr_lo.astype(jnp.int32)
                in_band = hi == t_hi
                above = jnp.where(hi > t_hi, 1.0, jnp.where(in_band, jnp.where(lo > t_lo, 1.0, 0.0), 0.0))
                tie = jnp.where(in_band, jnp.where(lo == t_lo, 1.0, 0.0), 0.0)
                parts = []
                for c0 in range(0, tk, tc_w):
                    tie_c = tie[:, c0:c0 + tc_w]
                    rank = taken + _mm(tie_c, prefix)
                    parts.append(above[:, c0:c0 + tc_w] + jnp.where(rank <= need, tie_c, 0.0))
                    taken = rank[:, tc_w - 1:tc_w]
                sel = parts[0] if len(parts) == 1 else jnp.concatenate(parts, axis=1)
            else:
                sel = jnp.concatenate(
                    [jnp.where(key_hi[st] > thr_hi, one16, tie_band(st, lambda lo: lo >= thr_lo)) for st in pieces],
                    axis=1).astype(f32)
            bias = (1.0 - sel) * NEG_BIG
            bias2 = jnp.concatenate([bias, bias], axis=0)
            start = pl.multiple_of(kt * tk, tk)
            pairs = range(DSA_HEADS // 2)
            slabs = [slice(pi * LANES, (pi + 1) * LANES) for pi in pairs]
            logits = [_mm_nt(q_pairs[pi], k_ref[0, pl.ds(start, tk), slabs[pi]]) + bias2 for pi in pairs]
            m_old = [m_s[pi] for pi in pairs]
            m_new = [jnp.maximum(m_old[pi], jnp.max(logits[pi], axis=1, keepdims=True)) for pi in pairs]
            alpha = [jnp.exp2(m_old[pi] - m_new[pi]) for pi in pairs]
            p = [jnp.exp2(logits[pi] - m_new[pi]) for pi in pairs]
            for pi in pairs:
                l_s[pi] = alpha[pi] * l_s[pi] + jnp.sum(p[pi], axis=1, keepdims=True)
                acc_s[pi] = alpha[pi] * acc_s[pi] + _mm(p[pi], v_ref[0, pl.ds(start, tk), slabs[pi]])
                m_s[pi] = m_new[pi]
            return taken

        lax.fori_loop(0, n_tiles, body, jnp.zeros((qb, 1), f32))
        for pi in range(DSA_HEADS // 2):
            out = acc_s[pi] / l_s[pi]
            o_ref[0, :, pi * LANES:(pi + 1) * LANES] = jnp.where(lower_half, out[:qb], out[qb:]).astype(o_ref.dtype)

    @pl.when(surplus == 0)
    def _():
        attend(False)

    @pl.when(surplus != 0)
    def _():
        attend(True)


def _dsa_call(qb_arr, qib_arr, sm, k_all, v_all, ki_all, pos0, n_keys, qb, tk):
    b, t, _ = qb_arr.shape
    lp = k_all.shape[1]
    ts = min(tk, DSA_SEARCH_TILE)
    assert lp % tk == 0 and t % qb == 0 and tk % ts == 0
    topk = min(TOPK_MAX, n_keys // 4)
    kern = functools.partial(_dsa_kernel, qb=qb, tk=tk, ts=ts, pos0=pos0, n_keys=n_keys, topk=topk)
    resident = dict(pipeline_mode=pl.Buffered(1))
    return pl.pallas_call(
        kern,
        out_shape=jax.ShapeDtypeStruct((b, t, DSA_W), MXU_DTYPE),
        grid=(b, t // qb),
        in_specs=[pl.BlockSpec((1, qb, DSA_W), lambda i, j: (i, j, 0)),
                  pl.BlockSpec((1, qb, IDX_W), lambda i, j: (i, j, 0)),
                  pl.BlockSpec((1, qb, LANES), lambda i, j: (i, j, 0)),
                  pl.BlockSpec((1, lp, DSA_W), lambda i, j: (i, 0, 0), **resident),
                  pl.BlockSpec((1, lp, DSA_W), lambda i, j: (i, 0, 0), **resident),
                  pl.BlockSpec((1, lp, LANES), lambda i, j: (i, 0, 0), **resident)],
        out_specs=pl.BlockSpec((1, qb, DSA_W), lambda i, j: (i, j, 0)),
        scratch_shapes=[pltpu.VMEM((lp // ts, qb, ts), jnp.int16),
                        pltpu.VMEM((lp // ts, qb, ts), jnp.int16),
                        pltpu.VMEM((DSA_HEADS // 2, 2 * qb, 1), f32),
                        pltpu.VMEM((DSA_HEADS // 2, 2 * qb, 1), f32),
                        pltpu.VMEM((DSA_HEADS // 2, 2 * qb, LANES), f32)],
        compiler_params=_params("parallel", "arbitrary"),
        name="dsa",
    )(qb_arr, qib_arr, sm, k_all, v_all, ki_all)


def _outproj_kernel(h_ref, og_ref, od_ref, w_ref, g_ref, b_ref, o_ref, *, alpha):
    mix = (jnp.dot(og_ref[...], w_ref[0:GDN_VW, :], preferred_element_type=f32)
           + jnp.dot(od_ref[...], w_ref[GDN_VW:, :], preferred_element_type=f32))
    o_ref[...] = _layer_norm(alpha * h_ref[...] + mix, g_ref[...], b_ref[...])


def _outproj_call(h, og, od, w, g, b, alpha):
    n, d = h.shape
    tm = _row_tile(n, 512)
    return pl.pallas_call(
        functools.partial(_outproj_kernel, alpha=alpha),
        out_shape=jax.ShapeDtypeStruct((n, d), f32),
        grid=(n // tm,),
        in_specs=[pl.BlockSpec((tm, d), lambda i: (i, 0)),
                  pl.BlockSpec((tm, GDN_VW), lambda i: (i, 0)),
                  pl.BlockSpec((tm, DSA_W), lambda i: (i, 0)),
                  pl.BlockSpec((GDN_VW + DSA_W, d), lambda i: (0, 0)),
                  pl.BlockSpec((1, d), lambda i: (0, 0)),
                  pl.BlockSpec((1, d), lambda i: (0, 0))],
        out_specs=pl.BlockSpec((tm, d), lambda i: (i, 0)),
        compiler_params=_params("parallel"),
        name="outproj_ln",
    )(h, og, od, w, g.reshape(1, d), b.reshape(1, d))


def _ffn_kernel(x_ref, wg_ref, wu_ref, wd_ref, g_ref, b_ref, o_ref, acc, *, alpha):
    j = pl.program_id(1)

    @pl.when(j == 0)
    def _():
        acc[...] = jnp.zeros(acc.shape, f32)

    xb = x_ref[...].astype(MXU_DTYPE)
    gate = jnp.dot(xb, wg_ref[...], preferred_element_type=f32)
    up = jnp.dot(xb, wu_ref[...], preferred_element_type=f32)
    act = gate * _sigmoid(gate) * up
    acc[...] += jnp.dot(act.astype(MXU_DTYPE), wd_ref[...], preferred_element_type=f32)

    @pl.when(j == pl.num_programs(1) - 1)
    def _():
        o_ref[...] = _layer_norm(alpha * x_ref[...] + acc[...], g_ref[...], b_ref[...])


def _ffn_call(x, wg, wu, wd, g, b, alpha):
    n, d = x.shape
    f = wg.shape[1]
    tm = _row_tile(n, 1024)
    tf = _row_tile(f, 512)
    return pl.pallas_call(
        functools.partial(_ffn_kernel, alpha=alpha),
        out_shape=jax.ShapeDtypeStruct((n, d), f32),
        grid=(n // tm, f // tf),
        in_specs=[pl.BlockSpec((tm, d), lambda i, j: (i, 0)),
                  pl.BlockSpec((d, tf), lambda i, j: (0, j)),
                  pl.BlockSpec((d, tf), lambda i, j: (0, j)),
                  pl.BlockSpec((tf, d), lambda i, j: (j, 0)),
                  pl.BlockSpec((1, d), lambda i, j: (0, 0)),
                  pl.BlockSpec((1, d), lambda i, j: (0, 0))],
        out_specs=pl.BlockSpec((tm, d), lambda i, j: (i, 0)),
        scratch_shapes=[pltpu.VMEM((tm, d), f32)],
        compiler_params=_params("parallel", "arbitrary"),
        name="ffn_ln",
    )(x, wg, wu, wd, g.reshape(1, d), b.reshape(1, d))


def _moe_plan_kernel(x_ref, wr_ref, plan_o, plan_t_o, xb_o, *, n_experts):
    tm = x_ref.shape[0]
    lane = lax.broadcasted_iota(jnp.int32, (tm, LANES), 1)
    x = x_ref[...]
    xb_o[...] = x.astype(xb_o.dtype)
    logits = _mm(x, wr_ref[...])
    logits = jnp.where(lane < n_experts, logits, -jnp.inf)
    m1 = jnp.max(logits, axis=1, keepdims=True)
    i1 = jnp.min(jnp.where(logits == m1, lane, LANES), axis=1, keepdims=True)
    rest = jnp.where(lane == i1, -jnp.inf, logits)
    m2 = jnp.max(rest, axis=1, keepdims=True)
    i2 = jnp.min(jnp.where(rest == m2, lane, LANES), axis=1, keepdims=True)
    e2 = jnp.exp(m2 - m1)
    w1 = 1.0 / (1.0 + e2)
    gates = jnp.where(lane == i1, w1, 0.0) + jnp.where(lane == i2, e2 * w1, 0.0)
    chosen = jnp.where(lane == i1, 1.0, 0.0) + jnp.where(lane == i2, 1.0, 0.0)
    blk = min(tm, MOE_PREFIX_BLOCK)
    r = lax.broadcasted_iota(jnp.int32, (blk, blk), 0)
    c = lax.broadcasted_iota(jnp.int32, (blk, blk), 1)
    tri = jnp.where(r >= c, 1.0, 0.0).astype(MXU_DTYPE)
    carry = jnp.zeros((1, LANES), f32)
    slots = []
    for b0 in range(0, tm, blk):
        m = chosen[b0:b0 + blk]
        inclusive = jnp.dot(tri, m.astype(MXU_DTYPE), preferred_element_type=f32) + carry
        slots.append(inclusive - m)
        carry = inclusive[blk - 1:blk, :]
    slot = jnp.concatenate(slots, axis=0) if len(slots) > 1 else slots[0]
    plan = jnp.where(lane < n_experts, gates,
                     jnp.where(lane < 2 * n_experts, pltpu.roll(slot, n_experts, 1),
                               jnp.where(lane < 3 * n_experts, pltpu.roll(chosen, 2 * n_experts, 1), 0.0)))
    plan_o[...] = plan
    plan_t_o[...] = plan.T[0:MOE_PLAN_ROWS, :]


def _moe_expert_kernel(xb_ref, plan_ref, plan_t_ref, wg_ref, wu_ref, wd_ref, y_ref,
                       xc, acc_c, *, n_experts, slot_block):
    e = pl.program_id(1)
    j = pl.program_id(2)
    tm = xb_ref.shape[0]
    cb = slot_block
    slot_row = plan_t_ref[pl.ds(e + n_experts, 1), :]
    chosen_row = plan_t_ref[pl.ds(e + 2 * n_experts, 1), :]
    count = jnp.max(jnp.where(chosen_row > 0.0, slot_row + 1.0, 0.0)).astype(jnp.int32)
    n_blocks = (count + cb - 1) // cb
    dot = functools.partial(jnp.dot, preferred_element_type=f32)

    def one_hot(sb):
        slot_id = (lax.broadcasted_iota(jnp.int32, (cb, 1), 0) + sb * cb).astype(f32)
        hit = jnp.where(chosen_row > 0.0, jnp.where(slot_row == slot_id, 1.0, 0.0), 0.0)
        return hit.astype(MXU_DTYPE)

    def rows_of(sb):
        return pl.ds(pl.multiple_of(sb * cb, cb), cb)

    @pl.when(jnp.logical_and(e == 0, j == 0))
    def _():
        y_ref[...] = jnp.zeros(y_ref.shape, f32)

    @pl.when(j == 0)
    def _():
        def gather(sb, carry):
            xc[rows_of(sb), :] = dot(one_hot(sb), xb_ref[...]).astype(xc.dtype)
            acc_c[rows_of(sb), :] = jnp.zeros((cb, acc_c.shape[1]), f32)
            return carry
        lax.fori_loop(0, n_blocks, gather, 0)

    def expert(sb, carry):
        xs = xc[rows_of(sb), :]
        gate = dot(xs, wg_ref[0])
        up = dot(xs, wu_ref[0])
        act = gate * _sigmoid(gate) * up
        acc_c[rows_of(sb), :] += dot(act.astype(MXU_DTYPE), wd_ref[0])
        return carry

    lax.fori_loop(0, n_blocks, expert, 0)

    @pl.when(j == pl.num_programs(2) - 1)
    def _():
        lane = lax.broadcasted_iota(jnp.int32, (tm, LANES), 1)
        gate_col = jnp.sum(jnp.where(lane == e, plan_ref[...], 0.0), axis=1, keepdims=True)

        def scatter(sb, carry):
            back = lax.dot_general(one_hot(sb), acc_c[rows_of(sb), :].astype(MXU_DTYPE),
                                   (((0,), (0,)), ((), ())), preferred_element_type=f32)
            y_ref[...] += gate_col * back
            return carry
        lax.fori_loop(0, n_blocks, scatter, 0)


def _residual_ln_kernel(x_ref, y_ref, g_ref, b_ref, o_ref, *, alpha):
    o_ref[...] = _layer_norm(alpha * x_ref[...] + y_ref[...], g_ref[...], b_ref[...])


def _moe_call(x, wr, wg, wu, wd, g, b, alpha):
    n, d = x.shape
    n_experts, _, f = wg.shape
    assert 3 * n_experts <= MOE_PLAN_ROWS
    tm = _row_tile(n, MOE_TOKEN_TILE)
    tf = _row_tile(f, 512)
    cb = min(MOE_SLOT_BLOCK, tm)
    cap = -(-tm // cb) * cb
    plan, plan_t, xb = pl.pallas_call(
        functools.partial(_moe_plan_kernel, n_experts=n_experts),
        out_shape=[jax.ShapeDtypeStruct((n, LANES), f32),
                   jax.ShapeDtypeStruct((MOE_PLAN_ROWS, n), f32),
                   jax.ShapeDtypeStruct((n, d), MXU_DTYPE)],
        grid=(n // tm,),
        in_specs=[pl.BlockSpec((tm, d), lambda i: (i, 0)),
                  pl.BlockSpec((d, LANES), lambda i: (0, 0))],
        out_specs=[pl.BlockSpec((tm, LANES), lambda i: (i, 0)),
                   pl.BlockSpec((MOE_PLAN_ROWS, tm), lambda i: (0, i)),
                   pl.BlockSpec((tm, d), lambda i: (i, 0))],
        compiler_params=_params("parallel"),
        name="moe_plan",
    )(x, wr)
    y = pl.pallas_call(
        functools.partial(_moe_expert_kernel, n_experts=n_experts, slot_block=cb),
        out_shape=jax.ShapeDtypeStruct((n, d), f32),
        grid=(n // tm, n_experts, f // tf),
        in_specs=[pl.BlockSpec((tm, d), lambda i, e, j: (i, 0)),
                  pl.BlockSpec((tm, LANES), lambda i, e, j: (i, 0)),
                  pl.BlockSpec((MOE_PLAN_ROWS, tm), lambda i, e, j: (0, i)),
                  pl.BlockSpec((1, d, tf), lambda i, e, j: (e, 0, j)),
                  pl.BlockSpec((1, d, tf), lambda i, e, j: (e, 0, j)),
                  pl.BlockSpec((1, tf, d), lambda i, e, j: (e, j, 0))],
        out_specs=pl.BlockSpec((tm, d), lambda i, e, j: (i, 0)),
        scratch_shapes=[pltpu.VMEM((cap, d), MXU_DTYPE),
                        pltpu.VMEM((cap, d), f32)],
        compiler_params=_params("parallel", "arbitrary", "arbitrary"),
        name="moe_experts",
    )(xb, plan, plan_t, wg, wu, wd)
    tr = _row_tile(n, 1024)
    return pl.pallas_call(
        functools.partial(_residual_ln_kernel, alpha=alpha),
        out_shape=jax.ShapeDtypeStruct((n, d), f32),
        grid=(n // tr,),
        in_specs=[pl.BlockSpec((tr, d), lambda i: (i, 0)),
                  pl.BlockSpec((tr, d), lambda i: (i, 0)),
                  pl.BlockSpec((1, d), lambda i: (0, 0)),
                  pl.BlockSpec((1, d), lambda i: (0, 0))],
        out_specs=pl.BlockSpec((tr, d), lambda i: (i, 0)),
        compiler_params=_params("parallel"),
        name="moe_residual_ln",
    )(x, y, g.reshape(1, d), b.reshape(1, d))


def _pack_w_in(w_in):
    o_z = GDN_CONV_CH
    o_b = o_z + GDN_VW
    o_a = o_b + GDN_HEADS
    o_qd = o_a + GDN_HEADS
    o_kd = o_qd + DSA_W
    o_vd = o_kd + DSA_W
    o_qi = o_vd + DSA_W
    o_ki = o_qi + IDX_W
    o_wi = o_ki + IDX_DIM
    assert w_in.shape[-1] == o_wi + IDX_HEADS
    sl = lambda a, n: w_in[..., a:a + n]
    pad = jnp.zeros(w_in.shape[:-1] + (LANES - 2 * GDN_HEADS - IDX_HEADS,), w_in.dtype)
    packed = jnp.concatenate([
        sl(0, GDN_CONV_CH), sl(o_z, GDN_VW), sl(o_qd, DSA_W), sl(o_kd, DSA_W), sl(o_vd, DSA_W),
        sl(o_qi, IDX_W), sl(o_ki, IDX_DIM), sl(o_ki, IDX_DIM),
        sl(o_b, GDN_HEADS), sl(o_a, GDN_HEADS), sl(o_wi, IDX_HEADS), pad], axis=-1)
    assert packed.shape[-1] == C_END
    return packed.astype(MXU_DTYPE)


def _rope_tables(pos):
    half = DSA_DH // 8
    inv_freq = ROPE_THETA ** (-jnp.arange(half, dtype=f32) / half)
    ang = pos.astype(f32)[:, None] * inv_freq[None, :]
    cos, sin = jnp.cos(ang), jnp.sin(ang)
    t = pos.shape[0]
    ones = jnp.ones((t, DSA_DH - 2 * half), f32)
    zeros = jnp.zeros((t, DSA_DH - 2 * half), f32)
    z8 = jnp.zeros((t, half), f32)
    cos_t = jnp.concatenate([cos, cos, ones], -1)
    lo = jnp.concatenate([-sin, z8, zeros], -1)
    hi = jnp.concatenate([z8, sin, zeros], -1)
    return tuple(jnp.tile(a, (1, LANES // DSA_DH)) for a in (cos_t, lo, hi))


def _gdn_params(a_log, dt_bias):
    row = lambda v: jnp.zeros((LANES,), f32).at[SM_DECAY:SM_DECAY + GDN_HEADS].set(v.astype(f32))
    return jnp.stack([row(a_log), row(dt_bias)])


def _trunk(x, pos0, conv_bufs, s0s, past_k, past_v, past_ik, chunk, wts, depth):
    b, t, d = x.shape
    n = b * t
    alpha = (2.0 * depth) ** 0.25
    tabs = _rope_tables(pos0 + jnp.arange(t))
    past = 0 if past_k is None else past_k.shape[2]
    n_keys = past + t
    assert t >= CONV_W - 1
    qb = min(t, DSA_Q_BLOCK)
    tk = DSA_KEY_TILE
    lp = -(-n_keys // tk) * tk

    def with_past(new, old, l):
        parts = [] if old is None else [old[l].reshape(b, past, -1).astype(MXU_DTYPE)]
        parts.append(new.reshape(b, t, -1))
        if lp > n_keys:
            parts.append(jnp.zeros((b, lp - n_keys, new.shape[-1]), MXU_DTYPE))
        return parts[0] if len(parts) == 1 else jnp.concatenate(parts, axis=1)

    h = _ln_call(x.reshape(n, d), wts["ln_in_g"], wts["ln_in_b"])
    new_k, new_v, new_ik, new_conv, new_s = [], [], [], [], []
    for l in range(depth):
        (qkv, z, k_f, v_f, ki_f, q_b, k_b, v_b, qi_b, ki_b, sm) = _inproj_call(h, wts["w_in"][l], tabs, t)
        qkv3 = qkv.reshape(b, t, GDN_CONV_CH)
        buf8 = jnp.concatenate([jnp.zeros((b, SUBLANES - (CONV_W - 1), GDN_CONV_CH), f32), conv_bufs[l]], axis=1)
        o_g, s_new = _gdn_call(qkv3, z.reshape(b, t, GDN_VW), sm.reshape(b, t, LANES), wts["conv_w"][l],
                               buf8, s0s[l], _gdn_params(wts["gdn_a_log"][l], wts["gdn_dt_bias"][l]),
                               wts["gdn_norm_w"][l].reshape(1, GDN_DV), chunk)
        ki_dup = None if past_ik is None else jnp.concatenate([past_ik, past_ik], axis=-1)
        o_d = _dsa_call(q_b.reshape(b, t, DSA_W), qi_b.reshape(b, t, IDX_W), sm.reshape(b, t, LANES),
                        with_past(k_b, past_k, l), with_past(v_b, past_v, l), with_past(ki_b, ki_dup, l),
                        past, n_keys, qb, tk)
        h = _outproj_call(h, o_g.reshape(n, GDN_VW), o_d.reshape(n, DSA_W), wts["w_out"][l],
                          wts["ln1_g"][l], wts["ln1_b"][l], alpha)
        if l % 2 == 0:
            i = l // 2
            h = _ffn_call(h, wts["ffn_wg"][i], wts["ffn_wu"][i], wts["ffn_wd"][i],
                          wts["ln2_g"][l], wts["ln2_b"][l], alpha)
        else:
            i = l // 2
            h = _moe_call(h, wts["moe_router"][i], wts["moe_wg"][i], wts["moe_wu"][i], wts["moe_wd"][i],
                          wts["ln2_g"][l], wts["ln2_b"][l], alpha)
        new_k.append(k_f.reshape(b, t, DSA_HEADS, DSA_DH))
        new_v.append(v_f.reshape(b, t, DSA_HEADS, DSA_DH))
        new_ik.append(ki_f.reshape(b, t, IDX_DIM))
        new_conv.append(qkv3[:, t - (CONV_W - 1):])
        new_s.append(s_new)
    return (h.reshape(b, t, d), jnp.stack(new_k), jnp.stack(new_v), jnp.stack(new_ik),
            jnp.stack(new_s), jnp.stack(new_conv))


def kernel(x_prompt, x_sample, cache_k, cache_v, cache_idx_k, state_gdn, state_conv, ln_in_g, ln_in_b, w_in, conv_w, gdn_a_log, gdn_dt_bias, gdn_norm_w, w_out, ln1_g, ln1_b, ln2_g, ln2_b, ffn_wg, ffn_wu, ffn_wd, moe_router, moe_wg, moe_wu, moe_wd):
    depth = w_in.shape[0]
    n_experts = moe_router.shape[-1]
    bp, tp, d = x_prompt.shape
    cast = lambda a: a.astype(MXU_DTYPE)
    router = jnp.concatenate(
        [moe_router.astype(f32), jnp.zeros(moe_router.shape[:-1] + (LANES - n_experts,), f32)], axis=-1)
    wts = dict(ln_in_g=ln_in_g, ln_in_b=ln_in_b, w_in=_pack_w_in(w_in), conv_w=conv_w,
               gdn_a_log=gdn_a_log, gdn_dt_bias=gdn_dt_bias, gdn_norm_w=gdn_norm_w, w_out=cast(w_out),
               ln1_g=ln1_g, ln1_b=ln1_b, ln2_g=ln2_g, ln2_b=ln2_b,
               ffn_wg=cast(ffn_wg), ffn_wu=cast(ffn_wu), ffn_wd=cast(ffn_wd),
               moe_router=router, moe_wg=cast(moe_wg), moe_wu=cast(moe_wu), moe_wd=cast(moe_wd))
    zero_conv = jnp.zeros((depth, bp, CONV_W - 1, GDN_CONV_CH), f32)
    zero_s = jnp.zeros((depth, bp, GDN_HEADS, GDN_DK, GDN_DV), f32)
    y_p, k_p, v_p, ik_p, s_p, conv_p = _trunk(
        x_prompt, 0, zero_conv, zero_s, None, None, None, CHUNK, wts, depth)
    ts = x_sample.shape[1]
    past = cache_k.shape[2]
    y_s, k_s, v_s, ik_s, s_s, conv_s = _trunk(
        x_sample, past, state_conv, state_gdn, cache_k, cache_v, cache_idx_k, ts, wts, depth)
    return (y_p, y_s, k_p, v_p, ik_p, s_p, conv_p, k_s, v_s, ik_s, s_s, conv_s)
```

```python
import functools

import jax
import jax.numpy as jnp
from jax import lax
from jax.experimental import pallas as pl
from jax.experimental.pallas import tpu as pltpu

CHUNK = 64
CONV_W = 4
GDN_HEADS = 4
GDN_DK = 128
GDN_DV = 128
DSA_HEADS = 8
DSA_DH = 64
IDX_HEADS = 4
IDX_DIM = 64
TOPK_MAX = 256
ROPE_THETA = 500000.0
TOP_K_EXPERTS = 2
LN_EPS = 1e-5
RMS_EPS = 1e-6
L2_EPS = 1e-6

GDN_QK = GDN_HEADS * GDN_DK
GDN_VW = GDN_HEADS * GDN_DV
GDN_CONV_CH = 2 * GDN_QK + GDN_VW
DSA_W = DSA_HEADS * DSA_DH
IDX_W = IDX_HEADS * IDX_DIM

LANES = 128
SUBLANES = 8
VMEM_LIMIT = 56 * 1024 * 1024

LN_ROW_TILE = 1024
PROJ_ROW_TILE = 512
FFN_ROW_TILE = 1024
FFN_HIDDEN_TILE = 512
DSA_Q_BLOCK = 256
DSA_KEY_TILE = 2048
DSA_SEARCH_TILE = 1024
GDN_CHUNKS_PER_STEP = 4
MOE_TOKEN_TILE = 2048
MOE_SLOT_BLOCK = 576
MOE_PREFIX_BLOCK = 256
MOE_PLAN_ROWS = 24

MXU_DTYPE = jnp.bfloat16

C_QKV = 0
C_Z = C_QKV + GDN_CONV_CH
C_QD = C_Z + GDN_VW
C_KD = C_QD + DSA_W
C_VD = C_KD + DSA_W
C_QI = C_VD + DSA_W
C_KI = C_QI + IDX_W
C_SM = C_KI + LANES
C_END = C_SM + LANES
SM_BETA = 0
SM_DECAY = GDN_HEADS
SM_WI = 2 * GDN_HEADS

INT_MIN = -2 ** 31
HALF_BIAS = 2 ** 15
TIE_CHUNK = 512
NEG_BIG = -3e38
M_FLOOR = -1e38
LOG2E = 1.4426950408889634

f32 = jnp.float32


def _mm(a, b):
    return jnp.dot(a.astype(MXU_DTYPE), b.astype(MXU_DTYPE), preferred_element_type=f32)


def _mm_nt(a, b):
    return lax.dot_general(a.astype(MXU_DTYPE), b.astype(MXU_DTYPE), (((1,), (1,)), ((), ())),
                           preferred_element_type=f32)


def _mm_tn(a, b):
    return lax.dot_general(a.astype(MXU_DTYPE), b.astype(MXU_DTYPE), (((0,), (0,)), ((), ())),
                           preferred_element_type=f32)


def _split(x):
    hi = x.astype(MXU_DTYPE)
    lo = (x - hi.astype(f32)).astype(MXU_DTYPE)
    return hi, lo


def _sigmoid(x):
    return 1.0 / (1.0 + jnp.exp(-x))


def _layer_norm(r, g, b):
    mu = jnp.mean(r, axis=-1, keepdims=True)
    xc = r - mu
    var = jnp.mean(xc * xc, axis=-1, keepdims=True)
    return xc * lax.rsqrt(var + LN_EPS) * g + b


def _params(*sem):
    return pltpu.CompilerParams(dimension_semantics=sem, vmem_limit_bytes=VMEM_LIMIT)


def _row_tile(n, want):
    t = min(n, want)
    assert n % t == 0, (n, t)
    return t


def _ln_kernel(x_ref, g_ref, b_ref, o_ref):
    o_ref[...] = _layer_norm(x_ref[...], g_ref[...], b_ref[...])


def _ln_call(x, g, b):
    n, d = x.shape
    tm = _row_tile(n, LN_ROW_TILE)
    return pl.pallas_call(
        _ln_kernel,
        out_shape=jax.ShapeDtypeStruct((n, d), f32),
        grid=(n // tm,),
        in_specs=[pl.BlockSpec((tm, d), lambda i: (i, 0)),
                  pl.BlockSpec((1, d), lambda i: (0, 0)),
                  pl.BlockSpec((1, d), lambda i: (0, 0))],
        out_specs=pl.BlockSpec((tm, d), lambda i: (i, 0)),
        compiler_params=_params("parallel"),
        name="ln_in",
    )(x, g.reshape(1, d), b.reshape(1, d))


def _rope(x, cos_t, sin_lo, sin_hi):
    w = x.shape[1]
    reps = w // LANES
    c = jnp.tile(cos_t, (1, reps))
    lo = jnp.tile(sin_lo, (1, reps))
    hi = jnp.tile(sin_hi, (1, reps))
    half = DSA_DH // 8
    from_above = pltpu.roll(x, w - half, 1)
    from_below = pltpu.roll(x, half, 1)
    return x * c + from_above * lo + from_below * hi


def _inproj_kernel(x_ref, w_ref, cos_ref, lo_ref, hi_ref,
                   qkv_o, z_o, k_o, v_o, ki_o, qb_o, kb_o, vb_o, qib_o, kib_o, sm_o):
    xb = x_ref[...].astype(MXU_DTYPE)
    cos_t, lo, hi = cos_ref[...], lo_ref[...], hi_ref[...]

    def proj(c0, c1):
        return jnp.dot(xb, w_ref[:, c0:c1], preferred_element_type=f32)

    qkv_o[...] = proj(C_QKV, C_Z)
    z_o[...] = proj(C_Z, C_QD)
    q = _rope(proj(C_QD, C_KD), cos_t, lo, hi) * (DSA_DH ** -0.5 * LOG2E)
    qb_o[...] = q.astype(qb_o.dtype)
    k = _rope(proj(C_KD, C_VD), cos_t, lo, hi)
    k_o[...] = k
    kb_o[...] = k.astype(kb_o.dtype)
    v = proj(C_VD, C_QI)
    v_o[...] = v
    vb_o[...] = v.astype(vb_o.dtype)
    qi = _rope(proj(C_QI, C_KI), cos_t, lo, hi) * (IDX_DIM ** -0.5)
    qib_o[...] = qi.astype(qib_o.dtype)
    ki = _rope(proj(C_KI, C_SM), cos_t, lo, hi)
    ki_o[...] = ki[:, :IDX_DIM]
    kib_o[...] = ki.astype(kib_o.dtype)
    sm_o[...] = proj(C_SM, C_END)


def _inproj_call(h, w_packed, tabs, seq_len):
    n, d = h.shape
    tm = _row_tile(n, PROJ_ROW_TILE)
    cos_t, lo, hi = tabs
    if tm > seq_len:
        assert tm % seq_len == 0
        cos_t, lo, hi = (jnp.tile(a, (tm // seq_len, 1)) for a in (cos_t, lo, hi))
        nblk = 1
    else:
        assert seq_len % tm == 0
        nblk = seq_len // tm
    tab_spec = pl.BlockSpec((tm, LANES), lambda i: (i % nblk, 0))

    def rows(w):
        return pl.BlockSpec((tm, w), lambda i: (i, 0))

    widths = [(GDN_CONV_CH, f32), (GDN_VW, f32), (DSA_W, f32), (DSA_W, f32), (IDX_DIM, f32),
              (DSA_W, MXU_DTYPE), (DSA_W, MXU_DTYPE), (DSA_W, MXU_DTYPE), (IDX_W, MXU_DTYPE),
              (LANES, MXU_DTYPE), (LANES, f32)]
    return pl.pallas_call(
        _inproj_kernel,
        out_shape=[jax.ShapeDtypeStruct((n, w), dt) for w, dt in widths],
        grid=(n // tm,),
        in_specs=[rows(d),
                  pl.BlockSpec((d, C_END), lambda i: (0, 0), pipeline_mode=pl.Buffered(1)),
                  tab_spec, tab_spec, tab_spec],
        out_specs=[rows(w) for w, _ in widths],
        compiler_params=_params("parallel"),
        name="inproj",
    )(h, w_packed, cos_t, lo, hi)


def _wide_to_blocks(w, chunk):
    size = w.shape[1]
    r = lax.broadcasted_iota(jnp.int32, (size, size), 0)
    c = lax.broadcasted_iota(jnp.int32, (size, size), 1)
    tiled = jnp.concatenate([w] * (size // chunk), axis=0)
    return jnp.where(r // chunk == c // chunk, tiled, jnp.zeros_like(tiled))


def _blocks_to_wide(sq, chunk):
    size = sq.shape[1]
    blk = lax.broadcasted_iota(jnp.int32, (chunk, size), 1) // chunk
    out = jnp.zeros((chunk, size), sq.dtype)
    for h in range(size // chunk):
        out = jnp.where(blk == h, sq[h * chunk:(h + 1) * chunk, :], out)
    return out


def _wide_matmul(x, p, chunk):
    x_hi, x_lo = _split(x)
    p_hi, p_lo = _split(p)
    dot = functools.partial(jnp.dot, preferred_element_type=f32)
    both = dot(jnp.concatenate([x_hi, x_lo], axis=0), _wide_to_blocks(p_hi, chunk))
    return both[:chunk] + both[chunk:] + dot(x_hi, _wide_to_blocks(p_lo, chunk))


def _unit_lower_inverse_wide(a_list, chunk):
    size = a_list[0].shape[1]
    r = lax.broadcasted_iota(jnp.int32, (chunk, size), 0)
    c = lax.broadcasted_iota(jnp.int32, (chunk, size), 1) % chunk

    def same_block(n):
        return (r // n) == (c // n)

    base = SUBLANES
    eye = jnp.where(r == c, 1.0, 0.0)
    a0 = [jnp.where(same_block(base), a, 0.0) for a in a_list]
    p = [_wide_matmul(m, m, chunk) for m in a0]
    x = [eye - m for m in a0]
    x = [xi + _wide_matmul(xi, pi, chunk) for xi, pi in zip(x, p)]
    p = [_wide_matmul(pi, pi, chunk) for pi in p]
    x = [xi + _wide_matmul(xi, pi, chunk) for xi, pi in zip(x, p)]
    n = base
    while n < chunk:
        off = [jnp.where(same_block(2 * n), jnp.where(same_block(n), 0.0, a), 0.0) for a in a_list]
        t = [_wide_matmul(xi, oi, chunk) for xi, oi in zip(x, off)]
        x = [xi - _wide_matmul(ti, xi, chunk) for xi, ti in zip(x, t)]
        n *= 2
    return x


def _chunk_cumsum(g, chunk):
    rows = g.shape[0]
    r = lax.broadcasted_iota(jnp.int32, (rows, rows), 0)
    c = lax.broadcasted_iota(jnp.int32, (rows, rows), 1)
    tri = jnp.where(jnp.logical_and(r // chunk == c // chunk, r >= c), 1.0, 0.0).astype(MXU_DTYPE)
    dot = functools.partial(jnp.dot, preferred_element_type=f32)
    g1 = g.astype(MXU_DTYPE)
    rem = g - g1.astype(f32)
    g2 = rem.astype(MXU_DTYPE)
    g3 = (rem - g2.astype(f32)).astype(MXU_DTYPE)
    return dot(tri, g1) + (dot(tri, g2) + dot(tri, g3))


def _gdn_kernel(qkv_ref, z_ref, sm_ref, cw_ref, cb_ref, s0_ref, par_ref, nw_ref,
                o_ref, s_ref, ubuf, *, chunk, group):
    step = pl.program_id(1)
    rows = chunk * group

    @pl.when(step == 0)
    def _():
        ubuf[0:SUBLANES, :] = cb_ref[0]
        s_ref[0] = s0_ref[0]

    ubuf[SUBLANES:SUBLANES + rows, :] = qkv_ref[0]
    y = ubuf[SUBLANES - 3:SUBLANES - 3 + rows, :] * cw_ref[0:1, :]
    for i in range(1, CONV_W):
        y = y + ubuf[SUBLANES - 3 + i:SUBLANES - 3 + i + rows, :] * cw_ref[i:i + 1, :]
    ubuf[0:SUBLANES, :] = ubuf[rows:rows + SUBLANES, :]
    y = y * _sigmoid(y)

    sm = sm_ref[0]
    beta_all = _sigmoid(sm)
    xg = sm + par_ref[1:2, :]
    softplus = jnp.maximum(xg, 0.0) + jnp.log1p(jnp.exp(-jnp.abs(xg)))
    g_all = -jnp.exp(par_ref[0:1, :]) * softplus
    gc_all = _chunk_cumsum(g_all, chunk)
    gc_rows = gc_all.T

    heads = GDN_HEADS
    size = heads * chunk
    w_row = lax.broadcasted_iota(jnp.int32, (chunk, size), 0)
    w_lane = lax.broadcasted_iota(jnp.int32, (chunk, size), 1)
    w_col = w_lane % chunk
    w_head = w_lane // chunk
    causal_w = w_row >= w_col
    strict_w = w_row > w_col
    z = z_ref[0]
    nw = nw_ref[...]
    dot = functools.partial(jnp.dot, preferred_element_type=f32)

    def stack(a, rs, col0, width):
        return jnp.concatenate([a[rs, col0 + h * width:col0 + (h + 1) * width] for h in range(heads)], axis=0)

    def column_stack(a, rs, lane0):
        return jnp.concatenate([a[rs, lane0 + h:lane0 + h + 1] for h in range(heads)], axis=0)

    def column_wide(a, rs, lane0):
        out = jnp.zeros((chunk, size), f32)
        for h in range(heads):
            out = jnp.where(w_head == h, a[rs, lane0 + h:lane0 + h + 1], out)
        return out

    pre = []
    for g in range(group):
        rs = slice(g * chunk, (g + 1) * chunk)
        last = (g + 1) * chunk - 1
        q = stack(y, rs, 0, GDN_DK)
        k = stack(y, rs, GDN_QK, GDN_DK)
        v = stack(y, rs, 2 * GDN_QK, GDN_DV)
        q = q * lax.rsqrt(jnp.sum(q * q, -1, keepdims=True) + L2_EPS) * (GDN_DK ** -0.5)
        k = k * lax.rsqrt(jnp.sum(k * k, -1, keepdims=True) + L2_EPS)
        beta_s = column_stack(beta_all, rs, SM_BETA)
        gc_s = column_stack(gc_all, rs, SM_DECAY)
        g_last = [gc_all[last:last + 1, SM_DECAY + h:SM_DECAY + h + 1] for h in range(heads)]
        g_last_s = jnp.concatenate([jnp.broadcast_to(gl, (chunk, 1)) for gl in g_last], axis=0)
        gc_row_w = jnp.concatenate([gc_rows[SM_DECAY + h:SM_DECAY + h + 1, rs] for h in range(heads)], axis=1)
        decay_w = jnp.exp(jnp.where(causal_w, column_wide(gc_all, rs, SM_DECAY) - gc_row_w, -jnp.inf))
        kk_w = _blocks_to_wide(_mm_nt(k, k), chunk)
        a_w = jnp.where(strict_w, column_wide(beta_all, rs, SM_BETA) * kk_w * decay_w, 0.0)
        pre.append(dict(q=q, k=k, v=v, beta_s=beta_s, gc_s=gc_s, g_last=g_last, g_last_s=g_last_s,
                        decay_w=decay_w, a_w=a_w))
    t_ws = _unit_lower_inverse_wide([pc["a_w"] for pc in pre], chunk)
    local = []
    for pc, t_w in zip(pre, t_ws):
        q, k, v, beta_s, gc_s = pc["q"], pc["k"], pc["v"], pc["beta_s"], pc["gc_s"]
        rhs = jnp.concatenate([v * beta_s, k * (beta_s * jnp.exp(gc_s))], axis=-1)
        t_hi, t_lo = _split(t_w)
        r_hi, r_lo = _split(rhs)
        tb_hi = _wide_to_blocks(t_hi, chunk)
        both = dot(jnp.concatenate([tb_hi, _wide_to_blocks(t_lo, chunk)], axis=0), r_hi)
        sol = both[:size] + both[size:] + dot(tb_hi, r_lo)
        local.append(dict(u0=sol[:, :GDN_DV], wk=sol[:, GDN_DV:],
                          qk=_mm_nt(q, k) * _wide_to_blocks(pc["decay_w"], chunk),
                          q_dec=q * jnp.exp(gc_s), k_dec=k * jnp.exp(pc["g_last_s"] - gc_s), g_last=pc["g_last"]))

    s = [s_ref[0, h] for h in range(heads)]
    for g in range(group):
        rs = slice(g * chunk, (g + 1) * chunk)
        lc = local[g]
        u_parts, o_parts = [], []
        for h in range(heads):
            hs = slice(h * chunk, (h + 1) * chunk)
            both = _mm(jnp.concatenate([lc["wk"][hs], lc["q_dec"][hs]], axis=0), s[h])
            u_parts.append(lc["u0"][hs] - both[:chunk])
            o_parts.append(both[chunk:])
        u = jnp.concatenate(u_parts, axis=0)
        o = jnp.concatenate(o_parts, axis=0) + _mm(lc["qk"], u)
        for h in range(heads):
            hs = slice(h * chunk, (h + 1) * chunk)
            s[h] = s[h] * jnp.exp(lc["g_last"][h]) + _mm_tn(lc["k_dec"][hs], u[hs])
        z_s = stack(z, rs, 0, GDN_DV)
        o = o * lax.rsqrt(jnp.mean(o * o, -1, keepdims=True) + RMS_EPS) * nw * (z_s * _sigmoid(z_s))
        for h in range(heads):
            o_ref[0, rs, h * GDN_DV:(h + 1) * GDN_DV] = o[h * chunk:(h + 1) * chunk].astype(o_ref.dtype)
    for h in range(heads):
        s_ref[0, h] = s[h]


def _gdn_call(qkv, z, sm, conv_w, conv_buf8, s0, par, norm_w, chunk):
    b, t, _ = qkv.shape
    group = min(GDN_CHUNKS_PER_STEP, t // chunk)
    rows = chunk * group
    assert t % rows == 0
    return pl.pallas_call(
        functools.partial(_gdn_kernel, chunk=chunk, group=group),
        out_shape=[jax.ShapeDtypeStruct((b, t, GDN_VW), MXU_DTYPE),
                   jax.ShapeDtypeStruct((b, GDN_HEADS, GDN_DK, GDN_DV), f32)],
        grid=(b, t // rows),
        in_specs=[pl.BlockSpec((1, rows, GDN_CONV_CH), lambda i, j: (i, j, 0)),
                  pl.BlockSpec((1, rows, GDN_VW), lambda i, j: (i, j, 0)),
                  pl.BlockSpec((1, rows, LANES), lambda i, j: (i, j, 0)),
                  pl.BlockSpec((CONV_W, GDN_CONV_CH), lambda i, j: (0, 0)),
                  pl.BlockSpec((1, SUBLANES, GDN_CONV_CH), lambda i, j: (i, 0, 0)),
                  pl.BlockSpec((1, GDN_HEADS, GDN_DK, GDN_DV), lambda i, j: (i, 0, 0, 0)),
                  pl.BlockSpec((2, LANES), lambda i, j: (0, 0)),
                  pl.BlockSpec((1, GDN_DV), lambda i, j: (0, 0))],
        out_specs=[pl.BlockSpec((1, rows, GDN_VW), lambda i, j: (i, j, 0)),
                   pl.BlockSpec((1, GDN_HEADS, GDN_DK, GDN_DV), lambda i, j: (i, 0, 0, 0))],
        scratch_shapes=[pltpu.VMEM((SUBLANES + rows, GDN_CONV_CH), f32)],
        compiler_params=_params("parallel", "arbitrary"),
        name="gdn",
    )(qkv, z, sm, conv_w, conv_buf8, s0, par, norm_w)


def _dsa_kernel(q_ref, qi_ref, sm_ref, k_ref, v_ref, ki_ref, o_ref,
                key_hi, key_lo, m_s, l_s, acc_s, *, qb, tk, ts, pos0, n_keys, topk):
    blk = pl.program_id(1)
    row = lax.broadcasted_iota(jnp.int32, (qb, 1), 0)
    q_pos = pos0 + blk * qb + row
    limit = jnp.minimum((q_pos // CHUNK + 1) * CHUNK, n_keys)
    last_limit = jnp.minimum(((pos0 + (blk + 1) * qb - 1) // CHUNK + 1) * CHUNK, n_keys)
    n_tiles = (last_limit + tk - 1) // tk
    n_stiles = (last_limit + ts - 1) // ts
    sub = tk // ts
    lane = lax.broadcasted_iota(jnp.int32, (qb, LANES), 1)
    lower_half = lane < DSA_DH
    col = lax.broadcasted_iota(jnp.int32, (qb, ts), 1)
    groups = ts // LANES

    sm = sm_ref[0]
    qi_heads = []
    for h in range(IDX_HEADS):
        pair = qi_ref[0, :, (h // 2) * LANES:(h // 2 + 1) * LANES]
        keep = lower_half if h % 2 == 0 else jnp.logical_not(lower_half)
        qi_heads.append(jnp.where(keep, pair, jnp.zeros_like(pair)))
    wi = [sm[:, SM_WI + h:SM_WI + h + 1] * (IDX_HEADS ** -0.5) for h in range(IDX_HEADS)]

    def score_tile(kt, carry):
        ki_t = ki_ref[0, pl.ds(pl.multiple_of(kt * ts, ts), ts), :]
        score = jnp.zeros((qb, ts), f32)
        for h in range(IDX_HEADS):
            score = score + jnp.maximum(_mm_nt(qi_heads[h], ki_t), 0.0) * wi[h]
        bits = pltpu.bitcast(score, jnp.int32)
        key = bits ^ ((bits >> 31) & 0x7FFFFFFF)
        key = jnp.where(kt * ts + col < limit, key, INT_MIN)
        key_hi[kt] = (key >> 16).astype(jnp.int16)
        key_lo[kt] = ((key & 0xFFFF) - HALF_BIAS).astype(jnp.int16)
        return carry

    lax.fori_loop(0, n_stiles, score_tile, 0)

    def fill_lowest(kt, carry):
        lowest = jnp.full((qb, ts), -HALF_BIAS, jnp.int16)
        key_hi[kt] = lowest
        key_lo[kt] = lowest
        return carry

    lax.fori_loop(n_stiles, n_tiles * sub, fill_lowest, 0)

    one16, zero16 = jnp.int16(1), jnp.int16(0)

    def lane_fold(h16):
        part = h16[:, 0:LANES]
        for g in range(1, groups):
            part = part + h16[:, g * LANES:(g + 1) * LANES]
        return part

    def row_total(acc):
        return jnp.sum(acc.astype(jnp.int32), axis=1, keepdims=True)

    zero_acc = jnp.zeros((qb, LANES), jnp.int16)

    def count(hit):
        return row_total(lax.fori_loop(0, n_stiles, lambda kt, acc: acc + lane_fold(hit(kt)), zero_acc))

    def kth_largest_half(ref, want):
        def bit_step(i, u):
            cand = u | (jnp.int32(1) << (15 - i))
            cand16 = (cand - HALF_BIAS).astype(jnp.int16)
            n_ge_cand = count(lambda kt: jnp.where(ref[kt] >= cand16, one16, zero16))
            return jnp.where(n_ge_cand >= want, cand, u)
        return lax.fori_loop(0, 16, bit_step, jnp.zeros((qb, 1), jnp.int32))

    hi_u = kth_largest_half(key_hi, topk)
    thr_hi = (hi_u - HALF_BIAS).astype(jnp.int16)

    def mask_low(kt, acc):
        hi = key_hi[kt]
        key_lo[kt] = jnp.where(hi == thr_hi, key_lo[kt], jnp.int16(-HALF_BIAS))
        return acc + lane_fold(jnp.where(hi > thr_hi, one16, zero16))

    n_hi_gt = row_total(lax.fori_loop(0, n_stiles, mask_low, zero_acc))
    lo_u = kth_largest_half(key_lo, topk - n_hi_gt)
    lo_u = jnp.where(jnp.logical_and(hi_u == 0, lo_u == 0), 1, lo_u)
    thr_lo = (lo_u - HALF_BIAS).astype(jnp.int16)

    def tie_band(kt, low_pred):
        return jnp.where(key_hi[kt] == thr_hi, jnp.where(low_pred(key_lo[kt]), one16, zero16), zero16)


    def band_counts(kt, accs):
        in_band = key_hi[kt] == thr_hi
        lo = key_lo[kt]
        gt = jnp.where(in_band, jnp.where(lo > thr_lo, one16, zero16), zero16)
        ge = jnp.where(in_band, jnp.where(lo >= thr_lo, one16, zero16), zero16)
        return accs[0] + lane_fold(gt), accs[1] + lane_fold(ge)

    acc_gt, acc_ge = lax.fori_loop(0, n_stiles, band_counts, (zero_acc, zero_acc))
    n_gt = n_hi_gt + row_total(acc_gt)
    n_ge = n_hi_gt + row_total(acc_ge)
    need = (topk - n_gt).astype(f32)
    surplus = jnp.max(jnp.where(n_ge > topk, 1, 0))

    q_pairs = []
    for pi in range(DSA_HEADS // 2):
        pair = q_ref[0, :, pi * LANES:(pi + 1) * LANES]
        zero = jnp.zeros_like(pair)
        q_pairs.append(jnp.concatenate([jnp.where(lower_half, pair, zero), jnp.where(lower_half, zero, pair)], axis=0))

    def attend(exact_ties):
        m_s[...] = jnp.full(m_s.shape, M_FLOOR, f32)
        l_s[...] = jnp.zeros(l_s.shape, f32)
        acc_s[...] = jnp.zeros(acc_s.shape, f32)
        tc_w = min(tk, TIE_CHUNK)
        if exact_ties:
            tr = lax.broadcasted_iota(jnp.int32, (tc_w, tc_w), 0)
            tc = lax.broadcasted_iota(jnp.int32, (tc_w, tc_w), 1)
            prefix = jnp.where(tr <= tc, 1.0, 0.0).astype(MXU_DTYPE)

        def body(kt, taken):
            pieces = [kt * sub + i for i in range(sub)]
            if exact_ties:
                hi = jnp.concatenate([key_hi[st] for st in pieces], axis=1).astype(jnp.int32)
                lo = jnp.concatenate([key_lo[st] for st in pieces], axis=1).astype(jnp.int32)
                t_hi = thr_hi.astype(jnp.int32)
                t_lo = thr_lo.astype(jnp.int32)
                in_band = hi == t_hi
                above = jnp.where(hi > t_hi, 1.0, jnp.where(in_band, jnp.where(lo > t_lo, 1.0, 0.0), 0.0))
                tie = jnp.where(in_band, jnp.where(lo == t_lo, 1.0, 0.0), 0.0)
                parts = []
                for c0 in range(0, tk, tc_w):
                    tie_c = tie[:, c0:c0 + tc_w]
                    rank = taken + _mm(tie_c, prefix)
                    parts.append(above[:, c0:c0 + tc_w] + jnp.where(rank <= need, tie_c, 0.0))
                    taken = rank[:, tc_w - 1:tc_w]
                sel = parts[0] if len(parts) == 1 else jnp.concatenate(parts, axis=1)
            else:
                sel = jnp.concatenate(
                    [jnp.where(key_hi[st] > thr_hi, one16, tie_band(st, lambda lo: lo >= thr_lo)) for st in pieces],
                    axis=1).astype(f32)
            bias = (1.0 - sel) * NEG_BIG
            bias2 = jnp.concatenate([bias, bias], axis=0)
            start = pl.multiple_of(kt * tk, tk)
            pairs = range(DSA_HEADS // 2)
            slabs = [slice(pi * LANES, (pi + 1) * LANES) for pi in pairs]
            logits = [_mm_nt(q_pairs[pi], k_ref[0, pl.ds(start, tk), slabs[pi]]) + bias2 for pi in pairs]
            m_old = [m_s[pi] for pi in pairs]
            m_new = [jnp.maximum(m_old[pi], jnp.max(logits[pi], axis=1, keepdims=True)) for pi in pairs]
            alpha = [jnp.exp2(m_old[pi] - m_new[pi]) for pi in pairs]
            p = [jnp.exp2(logits[pi] - m_new[pi]) for pi in pairs]
            for pi in pairs:
                l_s[pi] = alpha[pi] * l_s[pi] + jnp.sum(p[pi], axis=1, keepdims=True)
                acc_s[pi] = alpha[pi] * acc_s[pi] + _mm(p[pi], v_ref[0, pl.ds(start, tk), slabs[pi]])
                m_s[pi] = m_new[pi]
            return taken

        lax.fori_loop(0, n_tiles, body, jnp.zeros((qb, 1), f32))
        for pi in range(DSA_HEADS // 2):
            out = acc_s[pi] / l_s[pi]
            o_ref[0, :, pi * LANES:(pi + 1) * LANES] = jnp.where(lower_half, out[:qb], out[qb:]).astype(o_ref.dtype)

    @pl.when(surplus == 0)
    def _():
        attend(False)

    @pl.when(surplus != 0)
    def _():
        attend(True)


def _dsa_call(qb_arr, qib_arr, sm, k_all, v_all, ki_all, pos0, n_keys, qb, tk):
    b, t, _ = qb_arr.shape
    lp = k_all.shape[1]
    ts = min(tk, DSA_SEARCH_TILE)
    assert lp % tk == 0 and t % qb == 0 and tk % ts == 0
    topk = min(TOPK_MAX, n_keys // 4)
    kern = functools.partial(_dsa_kernel, qb=qb, tk=tk, ts=ts, pos0=pos0, n_keys=n_keys, topk=topk)
    resident = dict(pipeline_mode=pl.Buffered(1))
    return pl.pallas_call(
        kern,
        out_shape=jax.ShapeDtypeStruct((b, t, DSA_W), MXU_DTYPE),
        grid=(b, t // qb),
        in_specs=[pl.BlockSpec((1, qb, DSA_W), lambda i, j: (i, j, 0)),
                  pl.BlockSpec((1, qb, IDX_W), lambda i, j: (i, j, 0)),
                  pl.BlockSpec((1, qb, LANES), lambda i, j: (i, j, 0)),
                  pl.BlockSpec((1, lp, DSA_W), lambda i, j: (i, 0, 0), **resident),
                  pl.BlockSpec((1, lp, DSA_W), lambda i, j: (i, 0, 0), **resident),
                  pl.BlockSpec((1, lp, LANES), lambda i, j: (i, 0, 0), **resident)],
        out_specs=pl.BlockSpec((1, qb, DSA_W), lambda i, j: (i, j, 0)),
        scratch_shapes=[pltpu.VMEM((lp // ts, qb, ts), jnp.int16),
                        pltpu.VMEM((lp // ts, qb, ts), jnp.int16),
                        pltpu.VMEM((DSA_HEADS // 2, 2 * qb, 1), f32),
                        pltpu.VMEM((DSA_HEADS // 2, 2 * qb, 1), f32),
                        pltpu.VMEM((DSA_HEADS // 2, 2 * qb, LANES), f32)],
        compiler_params=_params("parallel", "arbitrary"),
        name="dsa",
    )(qb_arr, qib_arr, sm, k_all, v_all, ki_all)


def _outproj_kernel(h_ref, og_ref, od_ref, w_ref, g_ref, b_ref, o_ref, *, alpha):
    mix = (jnp.dot(og_ref[...], w_ref[0:GDN_VW, :], preferred_element_type=f32)
           + jnp.dot(od_ref[...], w_ref[GDN_VW:, :], preferred_element_type=f32))
    o_ref[...] = _layer_norm(alpha * h_ref[...] + mix, g_ref[...], b_ref[...])


def _outproj_call(h, og, od, w, g, b, alpha):
    n, d = h.shape
    tm = _row_tile(n, PROJ_ROW_TILE)
    return pl.pallas_call(
        functools.partial(_outproj_kernel, alpha=alpha),
        out_shape=jax.ShapeDtypeStruct((n, d), f32),
        grid=(n // tm,),
        in_specs=[pl.BlockSpec((tm, d), lambda i: (i, 0)),
                  pl.BlockSpec((tm, GDN_VW), lambda i: (i, 0)),
                  pl.BlockSpec((tm, DSA_W), lambda i: (i, 0)),
                  pl.BlockSpec((GDN_VW + DSA_W, d), lambda i: (0, 0)),
                  pl.BlockSpec((1, d), lambda i: (0, 0)),
                  pl.BlockSpec((1, d), lambda i: (0, 0))],
        out_specs=pl.BlockSpec((tm, d), lambda i: (i, 0)),
        compiler_params=_params("parallel"),
        name="outproj_ln",
    )(h, og, od, w, g.reshape(1, d), b.reshape(1, d))


def _ffn_kernel(x_ref, wg_ref, wu_ref, wd_ref, g_ref, b_ref, o_ref, acc, *, alpha):
    j = pl.program_id(1)

    @pl.when(j == 0)
    def _():
        acc[...] = jnp.zeros(acc.shape, f32)

    xb = x_ref[...].astype(MXU_DTYPE)
    gate = jnp.dot(xb, wg_ref[...], preferred_element_type=f32)
    up = jnp.dot(xb, wu_ref[...], preferred_element_type=f32)
    act = gate * _sigmoid(gate) * up
    acc[...] += jnp.dot(act.astype(MXU_DTYPE), wd_ref[...], preferred_element_type=f32)

    @pl.when(j == pl.num_programs(1) - 1)
    def _():
        o_ref[...] = _layer_norm(alpha * x_ref[...] + acc[...], g_ref[...], b_ref[...])


def _ffn_call(x, wg, wu, wd, g, b, alpha):
    n, d = x.shape
    f = wg.shape[1]
    tm = _row_tile(n, FFN_ROW_TILE)
    tf = _row_tile(f, FFN_HIDDEN_TILE)
    return pl.pallas_call(
        functools.partial(_ffn_kernel, alpha=alpha),
        out_shape=jax.ShapeDtypeStruct((n, d), f32),
        grid=(n // tm, f // tf),
        in_specs=[pl.BlockSpec((tm, d), lambda i, j: (i, 0)),
                  pl.BlockSpec((d, tf), lambda i, j: (0, j)),
                  pl.BlockSpec((d, tf), lambda i, j: (0, j)),
                  pl.BlockSpec((tf, d), lambda i, j: (j, 0)),
                  pl.BlockSpec((1, d), lambda i, j: (0, 0)),
                  pl.BlockSpec((1, d), lambda i, j: (0, 0))],
        out_specs=pl.BlockSpec((tm, d), lambda i, j: (i, 0)),
        scratch_shapes=[pltpu.VMEM((tm, d), f32)],
        compiler_params=_params("parallel", "arbitrary"),
        name="ffn_ln",
    )(x, wg, wu, wd, g.reshape(1, d), b.reshape(1, d))


def _moe_plan_kernel(x_ref, wr_ref, plan_o, plan_t_o, xb_o, *, n_experts):
    tm = x_ref.shape[0]
    lane = lax.broadcasted_iota(jnp.int32, (tm, LANES), 1)
    x = x_ref[...]
    xb_o[...] = x.astype(xb_o.dtype)
    logits = _mm(x, wr_ref[...])
    logits = jnp.where(lane < n_experts, logits, -jnp.inf)
    m1 = jnp.max(logits, axis=1, keepdims=True)
    i1 = jnp.min(jnp.where(logits == m1, lane, LANES), axis=1, keepdims=True)
    rest = jnp.where(lane == i1, -jnp.inf, logits)
    m2 = jnp.max(rest, axis=1, keepdims=True)
    i2 = jnp.min(jnp.where(rest == m2, lane, LANES), axis=1, keepdims=True)
    e2 = jnp.exp(m2 - m1)
    w1 = 1.0 / (1.0 + e2)
    gates = jnp.where(lane == i1, w1, 0.0) + jnp.where(lane == i2, e2 * w1, 0.0)
    chosen = jnp.where(lane == i1, 1.0, 0.0) + jnp.where(lane == i2, 1.0, 0.0)
    blk = min(tm, MOE_PREFIX_BLOCK)
    r = lax.broadcasted_iota(jnp.int32, (blk, blk), 0)
    c = lax.broadcasted_iota(jnp.int32, (blk, blk), 1)
    tri = jnp.where(r >= c, 1.0, 0.0).astype(MXU_DTYPE)
    carry = jnp.zeros((1, LANES), f32)
    slots = []
    for b0 in range(0, tm, blk):
        m = chosen[b0:b0 + blk]
        inclusive = jnp.dot(tri, m.astype(MXU_DTYPE), preferred_element_type=f32) + carry
        slots.append(inclusive - m)
        carry = inclusive[blk - 1:blk, :]
    slot = jnp.concatenate(slots, axis=0) if len(slots) > 1 else slots[0]
    plan = jnp.where(lane < n_experts, gates,
                     jnp.where(lane < 2 * n_experts, pltpu.roll(slot, n_experts, 1),
                               jnp.where(lane < 3 * n_experts, pltpu.roll(chosen, 2 * n_experts, 1), 0.0)))
    plan_o[...] = plan
    plan_t_o[...] = plan.T[0:MOE_PLAN_ROWS, :]


def _moe_expert_kernel(xb_ref, plan_ref, plan_t_ref, wg_ref, wu_ref, wd_ref, y_ref,
                       xc, acc_c, *, n_experts, slot_block):
    e = pl.program_id(1)
    j = pl.program_id(2)
    tm = xb_ref.shape[0]
    cb = slot_block
    slot_row = plan_t_ref[pl.ds(e + n_experts, 1), :]
    chosen_row = plan_t_ref[pl.ds(e + 2 * n_experts, 1), :]
    count = jnp.max(jnp.where(chosen_row > 0.0, slot_row + 1.0, 0.0)).astype(jnp.int32)
    n_blocks = (count + cb - 1) // cb
    dot = functools.partial(jnp.dot, preferred_element_type=f32)

    def one_hot(sb):
        slot_id = (lax.broadcasted_iota(jnp.int32, (cb, 1), 0) + sb * cb).astype(f32)
        hit = jnp.where(chosen_row > 0.0, jnp.where(slot_row == slot_id, 1.0, 0.0), 0.0)
        return hit.astype(MXU_DTYPE)

    def rows_of(sb):
        return pl.ds(pl.multiple_of(sb * cb, cb), cb)

    @pl.when(jnp.logical_and(e == 0, j == 0))
    def _():
        y_ref[...] = jnp.zeros(y_ref.shape, f32)

    @pl.when(j == 0)
    def _():
        def gather(sb, carry):
            xc[rows_of(sb), :] = dot(one_hot(sb), xb_ref[...]).astype(xc.dtype)
            acc_c[rows_of(sb), :] = jnp.zeros((cb, acc_c.shape[1]), f32)
            return carry
        lax.fori_loop(0, n_blocks, gather, 0)

    def expert(sb, carry):
        xs = xc[rows_of(sb), :]
        gate = dot(xs, wg_ref[0])
        up = dot(xs, wu_ref[0])
        act = gate * _sigmoid(gate) * up
        acc_c[rows_of(sb), :] += dot(act.astype(MXU_DTYPE), wd_ref[0])
        return carry

    lax.fori_loop(0, n_blocks, expert, 0)

    @pl.when(j == pl.num_programs(2) - 1)
    def _():
        lane = lax.broadcasted_iota(jnp.int32, (tm, LANES), 1)
        gate_col = jnp.sum(jnp.where(lane == e, plan_ref[...], 0.0), axis=1, keepdims=True)

        def scatter(sb, carry):
            back = lax.dot_general(one_hot(sb), acc_c[rows_of(sb), :].astype(MXU_DTYPE),
                                   (((0,), (0,)), ((), ())), preferred_element_type=f32)
            y_ref[...] += gate_col * back
            return carry
        lax.fori_loop(0, n_blocks, scatter, 0)


def _residual_ln_kernel(x_ref, y_ref, g_ref, b_ref, o_ref, *, alpha):
    o_ref[...] = _layer_norm(alpha * x_ref[...] + y_ref[...], g_ref[...], b_ref[...])


def _moe_call(x, wr, wg, wu, wd, g, b, alpha):
    n, d = x.shape
    n_experts, _, f = wg.shape
    assert 3 * n_experts <= MOE_PLAN_ROWS and TOP_K_EXPERTS == 2
    tm = _row_tile(n, MOE_TOKEN_TILE)
    tf = _row_tile(f, FFN_HIDDEN_TILE)
    cb = min(MOE_SLOT_BLOCK, tm)
    cap = -(-tm // cb) * cb
    plan, plan_t, xb = pl.pallas_call(
        functools.partial(_moe_plan_kernel, n_experts=n_experts),
        out_shape=[jax.ShapeDtypeStruct((n, LANES), f32),
                   jax.ShapeDtypeStruct((MOE_PLAN_ROWS, n), f32),
                   jax.ShapeDtypeStruct((n, d), MXU_DTYPE)],
        grid=(n // tm,),
        in_specs=[pl.BlockSpec((tm, d), lambda i: (i, 0)),
                  pl.BlockSpec((d, LANES), lambda i: (0, 0))],
        out_specs=[pl.BlockSpec((tm, LANES), lambda i: (i, 0)),
                   pl.BlockSpec((MOE_PLAN_ROWS, tm), lambda i: (0, i)),
                   pl.BlockSpec((tm, d), lambda i: (i, 0))],
        compiler_params=_params("parallel"),
        name="moe_plan",
    )(x, wr)
    y = pl.pallas_call(
        functools.partial(_moe_expert_kernel, n_experts=n_experts, slot_block=cb),
        out_shape=jax.ShapeDtypeStruct((n, d), f32),
        grid=(n // tm, n_experts, f // tf),
        in_specs=[pl.BlockSpec((tm, d), lambda i, e, j: (i, 0)),
                  pl.BlockSpec((tm, LANES), lambda i, e, j: (i, 0)),
                  pl.BlockSpec((MOE_PLAN_ROWS, tm), lambda i, e, j: (0, i)),
                  pl.BlockSpec((1, d, tf), lambda i, e, j: (e, 0, j)),
                  pl.BlockSpec((1, d, tf), lambda i, e, j: (e, 0, j)),
                  pl.BlockSpec((1, tf, d), lambda i, e, j: (e, j, 0))],
        out_specs=pl.BlockSpec((tm, d), lambda i, e, j: (i, 0)),
        scratch_shapes=[pltpu.VMEM((cap, d), MXU_DTYPE),
                        pltpu.VMEM((cap, d), f32)],
        compiler_params=_params("parallel", "arbitrary", "arbitrary"),
        name="moe_experts",
    )(xb, plan, plan_t, wg, wu, wd)
    tr = _row_tile(n, LN_ROW_TILE)
    return pl.pallas_call(
        functools.partial(_residual_ln_kernel, alpha=alpha),
        out_shape=jax.ShapeDtypeStruct((n, d), f32),
        grid=(n // tr,),
        in_specs=[pl.BlockSpec((tr, d), lambda i: (i, 0)),
                  pl.BlockSpec((tr, d), lambda i: (i, 0)),
                  pl.BlockSpec((1, d), lambda i: (0, 0)),
                  pl.BlockSpec((1, d), lambda i: (0, 0))],
        out_specs=pl.BlockSpec((tr, d), lambda i: (i, 0)),
        compiler_params=_params("parallel"),
        name="moe_residual_ln",
    )(x, y, g.reshape(1, d), b.reshape(1, d))


def _pack_w_in(w_in):
    o_z = GDN_CONV_CH
    o_b = o_z + GDN_VW
    o_a = o_b + GDN_HEADS
    o_qd = o_a + GDN_HEADS
    o_kd = o_qd + DSA_W
    o_vd = o_kd + DSA_W
    o_qi = o_vd + DSA_W
    o_ki = o_qi + IDX_W
    o_wi = o_ki + IDX_DIM
    assert w_in.shape[-1] == o_wi + IDX_HEADS
    sl = lambda a, n: w_in[..., a:a + n]
    pad = jnp.zeros(w_in.shape[:-1] + (LANES - 2 * GDN_HEADS - IDX_HEADS,), w_in.dtype)
    packed = jnp.concatenate([
        sl(0, GDN_CONV_CH), sl(o_z, GDN_VW), sl(o_qd, DSA_W), sl(o_kd, DSA_W), sl(o_vd, DSA_W),
        sl(o_qi, IDX_W), sl(o_ki, IDX_DIM), sl(o_ki, IDX_DIM),
        sl(o_b, GDN_HEADS), sl(o_a, GDN_HEADS), sl(o_wi, IDX_HEADS), pad], axis=-1)
    assert packed.shape[-1] == C_END
    return packed.astype(MXU_DTYPE)


def _rope_tables(pos):
    half = DSA_DH // 8
    inv_freq = ROPE_THETA ** (-jnp.arange(half, dtype=f32) / half)
    ang = pos.astype(f32)[:, None] * inv_freq[None, :]
    cos, sin = jnp.cos(ang), jnp.sin(ang)
    t = pos.shape[0]
    ones = jnp.ones((t, DSA_DH - 2 * half), f32)
    zeros = jnp.zeros((t, DSA_DH - 2 * half), f32)
    z8 = jnp.zeros((t, half), f32)
    cos_t = jnp.concatenate([cos, cos, ones], -1)
    lo = jnp.concatenate([-sin, z8, zeros], -1)
    hi = jnp.concatenate([z8, sin, zeros], -1)
    return tuple(jnp.tile(a, (1, LANES // DSA_DH)) for a in (cos_t, lo, hi))


def _gdn_params(a_log, dt_bias):
    row = lambda v: jnp.zeros((LANES,), f32).at[SM_DECAY:SM_DECAY + GDN_HEADS].set(v.astype(f32))
    return jnp.stack([row(a_log), row(dt_bias)])


def _trunk(x, pos0, conv_bufs, s0s, past_k, past_v, past_ik, chunk, wts, depth):
    b, t, d = x.shape
    n = b * t
    alpha = (2.0 * depth) ** 0.25
    tabs = _rope_tables(pos0 + jnp.arange(t))
    past = 0 if past_k is None else past_k.shape[2]
    n_keys = past + t
    assert t >= CONV_W - 1
    qb = min(t, DSA_Q_BLOCK)
    tk = DSA_KEY_TILE
    lp = -(-n_keys // tk) * tk

    def with_past(new, old, l):
        parts = [] if old is None else [old[l].reshape(b, past, -1).astype(MXU_DTYPE)]
        parts.append(new.reshape(b, t, -1))
        if lp > n_keys:
            parts.append(jnp.zeros((b, lp - n_keys, new.shape[-1]), MXU_DTYPE))
        return parts[0] if len(parts) == 1 else jnp.concatenate(parts, axis=1)

    h = _ln_call(x.reshape(n, d), wts["ln_in_g"], wts["ln_in_b"])
    new_k, new_v, new_ik, new_conv, new_s = [], [], [], [], []
    for l in range(depth):
        (qkv, z, k_f, v_f, ki_f, q_b, k_b, v_b, qi_b, ki_b, sm) = _inproj_call(h, wts["w_in"][l], tabs, t)
        qkv3 = qkv.reshape(b, t, GDN_CONV_CH)
        buf8 = jnp.concatenate([jnp.zeros((b, SUBLANES - (CONV_W - 1), GDN_CONV_CH), f32), conv_bufs[l]], axis=1)
        o_g, s_new = _gdn_call(qkv3, z.reshape(b, t, GDN_VW), sm.reshape(b, t, LANES), wts["conv_w"][l],
                               buf8, s0s[l], _gdn_params(wts["gdn_a_log"][l], wts["gdn_dt_bias"][l]),
                               wts["gdn_norm_w"][l].reshape(1, GDN_DV), chunk)
        ki_dup = None if past_ik is None else jnp.concatenate([past_ik, past_ik], axis=-1)
        o_d = _dsa_call(q_b.reshape(b, t, DSA_W), qi_b.reshape(b, t, IDX_W), sm.reshape(b, t, LANES),
                        with_past(k_b, past_k, l), with_past(v_b, past_v, l), with_past(ki_b, ki_dup, l),
                        past, n_keys, qb, tk)
        h = _outproj_call(h, o_g.reshape(n, GDN_VW), o_d.reshape(n, DSA_W), wts["w_out"][l],
                          wts["ln1_g"][l], wts["ln1_b"][l], alpha)
        if l % 2 == 0:
            i = l // 2
            h = _ffn_call(h, wts["ffn_wg"][i], wts["ffn_wu"][i], wts["ffn_wd"][i],
                          wts["ln2_g"][l], wts["ln2_b"][l], alpha)
        else:
            i = l // 2
            h = _moe_call(h, wts["moe_router"][i], wts["moe_wg"][i], wts["moe_wu"][i], wts["moe_wd"][i],
                          wts["ln2_g"][l], wts["ln2_b"][l], alpha)
        new_k.append(k_f.reshape(b, t, DSA_HEADS, DSA_DH))
        new_v.append(v_f.reshape(b, t, DSA_HEADS, DSA_DH))
        new_ik.append(ki_f.reshape(b, t, IDX_DIM))
        new_conv.append(qkv3[:, t - (CONV_W - 1):])
        new_s.append(s_new)
    return (h.reshape(b, t, d), jnp.stack(new_k), jnp.stack(new_v), jnp.stack(new_ik),
            jnp.stack(new_s), jnp.stack(new_conv))


def kernel(x_prompt, x_sample, cache_k, cache_v, cache_idx_k, state_gdn, state_conv, ln_in_g, ln_in_b, w_in, conv_w, gdn_a_log, gdn_dt_bias, gdn_norm_w, w_out, ln1_g, ln1_b, ln2_g, ln2_b, ffn_wg, ffn_wu, ffn_wd, moe_router, moe_wg, moe_wu, moe_wd):
    depth = w_in.shape[0]
    n_experts = moe_router.shape[-1]
    bp, tp, d = x_prompt.shape
    cast = lambda a: a.astype(MXU_DTYPE)
    router = jnp.concatenate(
        [moe_router.astype(f32), jnp.zeros(moe_router.shape[:-1] + (LANES - n_experts,), f32)], axis=-1)
    wts = dict(ln_in_g=ln_in_g, ln_in_b=ln_in_b, w_in=_pack_w_in(w_in), conv_w=conv_w,
               gdn_a_log=gdn_a_log, gdn_dt_bias=gdn_dt_bias, gdn_norm_w=gdn_norm_w, w_out=cast(w_out),
               ln1_g=ln1_g, ln1_b=ln1_b, ln2_g=ln2_g, ln2_b=ln2_b,
               ffn_wg=cast(ffn_wg), ffn_wu=cast(ffn_wu), ffn_wd=cast(ffn_wd),
               moe_router=router, moe_wg=cast(moe_wg), moe_wu=cast(moe_wu), moe_wd=cast(moe_wd))
    zero_conv = jnp.zeros((depth, bp, CONV_W - 1, GDN_CONV_CH), f32)
    zero_s = jnp.zeros((depth, bp, GDN_HEADS, GDN_DK, GDN_DV), f32)
    y_p, k_p, v_p, ik_p, s_p, conv_p = _trunk(
        x_prompt, 0, zero_conv, zero_s, None, None, None, CHUNK, wts, depth)
    ts = x_sample.shape[1]
    past = cache_k.shape[2]
    y_s, k_s, v_s, ik_s, s_s, conv_s = _trunk(
        x_sample, past, state_conv, state_gdn, cache_k, cache_v, cache_idx_k, ts, wts, depth)
    return (y_p, y_s, k_p, v_p, ik_p, s_p, conv_p, k_s, v_s, ik_s, s_s, conv_s)
```

```python
import functools

import jax
import jax.numpy as jnp
from jax import lax
from jax.experimental import pallas as pl
from jax.experimental.pallas import tpu as pltpu

CHUNK = 64
CONV_W = 4
GDN_HEADS = 4
GDN_DK = 128
GDN_DV = 128
DSA_HEADS = 8
DSA_DH = 64
IDX_HEADS = 4
IDX_DIM = 64
TOPK_MAX = 256
ROPE_THETA = 500000.0
TOP_K_EXPERTS = 2
LN_EPS = 1e-5
RMS_EPS = 1e-6
L2_EPS = 1e-6

GDN_QK = GDN_HEADS * GDN_DK
GDN_VW = GDN_HEADS * GDN_DV
GDN_CONV_CH = 2 * GDN_QK + GDN_VW
DSA_W = DSA_HEADS * DSA_DH
IDX_W = IDX_HEADS * IDX_DIM

LANES = 128
SUBLANES = 8
VMEM_LIMIT = 56 * 1024 * 1024

LN_ROW_TILE = 1024
PROJ_ROW_TILE = 512
FFN_ROW_TILE = 1024
FFN_HIDDEN_TILE = 512
DSA_Q_BLOCK = 256
DSA_KEY_TILE = 2048
DSA_SEARCH_TILE = 1024
GDN_CHUNKS_PER_STEP = 4
MOE_TOKEN_TILE = 2048
MOE_SLOT_BLOCK = 576
MOE_PREFIX_BLOCK = 256
MOE_PLAN_ROWS = 24

MXU_DTYPE = jnp.bfloat16

C_QKV = 0
C_Z = C_QKV + GDN_CONV_CH
C_QD = C_Z + GDN_VW
C_KD = C_QD + DSA_W
C_VD = C_KD + DSA_W
C_QI = C_VD + DSA_W
C_KI = C_QI + IDX_W
C_SM = C_KI + LANES
C_END = C_SM + LANES
SM_BETA = 0
SM_DECAY = GDN_HEADS
SM_WI = 2 * GDN_HEADS

INT_MIN = -2 ** 31
HALF_BIAS = 2 ** 15
TIE_CHUNK = 512
NEG_BIG = -3e38
M_FLOOR = -1e38
LOG2E = 1.4426950408889634

f32 = jnp.float32


def _mm(a, b):
    return jnp.dot(a.astype(MXU_DTYPE), b.astype(MXU_DTYPE), preferred_element_type=f32)


def _mm_nt(a, b):
    return lax.dot_general(a.astype(MXU_DTYPE), b.astype(MXU_DTYPE), (((1,), (1,)), ((), ())),
                           preferred_element_type=f32)


def _mm_tn(a, b):
    return lax.dot_general(a.astype(MXU_DTYPE), b.astype(MXU_DTYPE), (((0,), (0,)), ((), ())),
                           preferred_element_type=f32)


def _split(x):
    hi = x.astype(MXU_DTYPE)
    lo = (x - hi.astype(f32)).astype(MXU_DTYPE)
    return hi, lo


def _sigmoid(x):
    return 1.0 / (1.0 + jnp.exp(-x))


def _layer_norm(r, g, b):
    mu = jnp.mean(r, axis=-1, keepdims=True)
    xc = r - mu
    var = jnp.mean(xc * xc, axis=-1, keepdims=True)
    return xc * lax.rsqrt(var + LN_EPS) * g + b


def _params(*sem):
    return pltpu.CompilerParams(dimension_semantics=sem, vmem_limit_bytes=VMEM_LIMIT)


def _row_tile(n, want):
    t = min(n, want)
    assert n % t == 0, (n, t)
    return t


def _ln_kernel(x_ref, g_ref, b_ref, o_ref):
    o_ref[...] = _layer_norm(x_ref[...], g_ref[...], b_ref[...])


def _ln_call(x, g, b):
    n, d = x.shape
    tm = _row_tile(n, LN_ROW_TILE)
    return pl.pallas_call(
        _ln_kernel,
        out_shape=jax.ShapeDtypeStruct((n, d), f32),
        grid=(n // tm,),
        in_specs=[pl.BlockSpec((tm, d), lambda i: (i, 0)),
                  pl.BlockSpec((1, d), lambda i: (0, 0)),
                  pl.BlockSpec((1, d), lambda i: (0, 0))],
        out_specs=pl.BlockSpec((tm, d), lambda i: (i, 0)),
        compiler_params=_params("parallel"),
        name="ln_in",
    )(x, g.reshape(1, d), b.reshape(1, d))


def _rope(x, cos_t, sin_lo, sin_hi):
    w = x.shape[1]
    reps = w // LANES
    c = jnp.tile(cos_t, (1, reps))
    lo = jnp.tile(sin_lo, (1, reps))
    hi = jnp.tile(sin_hi, (1, reps))
    half = DSA_DH // 8
    from_above = pltpu.roll(x, w - half, 1)
    from_below = pltpu.roll(x, half, 1)
    return x * c + from_above * lo + from_below * hi


N_INPROJ_OUTPUTS = 11


def _inproj_kernel(x_ref, w_ref, cos_ref, lo_ref, hi_ref, *refs):
    qkv_o, z_o, k_o, v_o, ki_o, qb_o, kb_o, vb_o, qib_o, kib_o, sm_o = refs[-N_INPROJ_OUTPUTS:]
    xb = x_ref[...].astype(MXU_DTYPE)
    cos_t, lo, hi = cos_ref[...], lo_ref[...], hi_ref[...]

    def proj(c0, c1):
        return jnp.dot(xb, w_ref[:, c0:c1], preferred_element_type=f32)

    qkv_o[...] = proj(C_QKV, C_Z)
    z_o[...] = proj(C_Z, C_QD)
    q = _rope(proj(C_QD, C_KD), cos_t, lo, hi) * (DSA_DH ** -0.5 * LOG2E)
    qb_o[...] = q.astype(qb_o.dtype)
    k = _rope(proj(C_KD, C_VD), cos_t, lo, hi)
    k_o[...] = k
    kb_o[...] = k.astype(kb_o.dtype)
    v = proj(C_VD, C_QI)
    v_o[...] = v
    vb_o[...] = v.astype(vb_o.dtype)
    qi = _rope(proj(C_QI, C_KI), cos_t, lo, hi) * (IDX_DIM ** -0.5)
    qib_o[...] = qi.astype(qib_o.dtype)
    ki = _rope(proj(C_KI, C_SM), cos_t, lo, hi)
    ki_o[...] = ki[:, :IDX_DIM]
    kib_o[...] = ki.astype(kib_o.dtype)
    sm_o[...] = proj(C_SM, C_END)


def _inproj_call(h, w_packed, tabs, seq_len, layer, depth, stacks):
    n, d = h.shape
    tm = _row_tile(n, PROJ_ROW_TILE)
    cos_t, lo, hi = tabs
    if tm > seq_len:
        assert tm % seq_len == 0
        cos_t, lo, hi = (jnp.tile(a, (tm // seq_len, 1)) for a in (cos_t, lo, hi))
        nblk = 1
    else:
        assert seq_len % tm == 0
        nblk = seq_len // tm
    tab_spec = pl.BlockSpec((tm, LANES), lambda i: (i % nblk, 0))

    def rows(w):
        return pl.BlockSpec((tm, w), lambda i: (i, 0))

    widths = [(GDN_CONV_CH, f32), (GDN_VW, f32), (DSA_W, f32), (DSA_W, f32), (IDX_DIM, f32),
              (DSA_W, MXU_DTYPE), (DSA_W, MXU_DTYPE), (DSA_W, MXU_DTYPE), (IDX_W, MXU_DTYPE),
              (LANES, MXU_DTYPE), (LANES, f32)]
    stacked = (2, 3, 4)
    out_shape = [jax.ShapeDtypeStruct((n, w), dt) for w, dt in widths]
    out_specs = [rows(w) for w, _ in widths]
    for o in stacked:
        w, dt = widths[o]
        out_shape[o] = jax.ShapeDtypeStruct((depth, n, w), dt)
        out_specs[o] = pl.BlockSpec((None, tm, w), lambda i: (layer, i, 0))
    n_in = 5
    assert len(stacks) in (0, len(stacked))
    return pl.pallas_call(
        _inproj_kernel,
        out_shape=out_shape,
        grid=(n // tm,),
        in_specs=[rows(d),
                  pl.BlockSpec((d, C_END), lambda i: (0, 0), pipeline_mode=pl.Buffered(1)),
                  tab_spec, tab_spec, tab_spec] + [pl.BlockSpec(memory_space=pl.ANY) for _ in stacks],
        out_specs=out_specs,
        input_output_aliases={n_in + j: stacked[j] for j in range(len(stacks))},
        compiler_params=_params("parallel"),
        name="inproj",
    )(h, w_packed, cos_t, lo, hi, *stacks)


def _wide_to_blocks(w, chunk):
    size = w.shape[1]
    r = lax.broadcasted_iota(jnp.int32, (size, size), 0)
    c = lax.broadcasted_iota(jnp.int32, (size, size), 1)
    tiled = jnp.concatenate([w] * (size // chunk), axis=0)
    return jnp.where(r // chunk == c // chunk, tiled, jnp.zeros_like(tiled))


def _blocks_to_wide(sq, chunk):
    size = sq.shape[1]
    blk = lax.broadcasted_iota(jnp.int32, (chunk, size), 1) // chunk
    out = jnp.zeros((chunk, size), sq.dtype)
    for h in range(size // chunk):
        out = jnp.where(blk == h, sq[h * chunk:(h + 1) * chunk, :], out)
    return out


def _wide_matmul(x, p, chunk):
    x_hi, x_lo = _split(x)
    p_hi, p_lo = _split(p)
    dot = functools.partial(jnp.dot, preferred_element_type=f32)
    both = dot(jnp.concatenate([x_hi, x_lo], axis=0), _wide_to_blocks(p_hi, chunk))
    return both[:chunk] + both[chunk:] + dot(x_hi, _wide_to_blocks(p_lo, chunk))


def _unit_lower_inverse_wide(a_list, chunk):
    size = a_list[0].shape[1]
    r = lax.broadcasted_iota(jnp.int32, (chunk, size), 0)
    c = lax.broadcasted_iota(jnp.int32, (chunk, size), 1) % chunk

    def same_block(n):
        return (r // n) == (c // n)

    base = SUBLANES
    eye = jnp.where(r == c, 1.0, 0.0)
    a0 = [jnp.where(same_block(base), a, 0.0) for a in a_list]
    p = [_wide_matmul(m, m, chunk) for m in a0]
    x = [eye - m for m in a0]
    x = [xi + _wide_matmul(xi, pi, chunk) for xi, pi in zip(x, p)]
    p = [_wide_matmul(pi, pi, chunk) for pi in p]
    x = [xi + _wide_matmul(xi, pi, chunk) for xi, pi in zip(x, p)]
    n = base
    while n < chunk:
        off = [jnp.where(same_block(2 * n), jnp.where(same_block(n), 0.0, a), 0.0) for a in a_list]
        t = [_wide_matmul(xi, oi, chunk) for xi, oi in zip(x, off)]
        x = [xi - _wide_matmul(ti, xi, chunk) for xi, ti in zip(x, t)]
        n *= 2
    return x


def _chunk_cumsum(g, chunk):
    rows = g.shape[0]
    r = lax.broadcasted_iota(jnp.int32, (rows, rows), 0)
    c = lax.broadcasted_iota(jnp.int32, (rows, rows), 1)
    tri = jnp.where(jnp.logical_and(r // chunk == c // chunk, r >= c), 1.0, 0.0).astype(MXU_DTYPE)
    dot = functools.partial(jnp.dot, preferred_element_type=f32)
    g1 = g.astype(MXU_DTYPE)
    rem = g - g1.astype(f32)
    g2 = rem.astype(MXU_DTYPE)
    g3 = (rem - g2.astype(f32)).astype(MXU_DTYPE)
    return dot(tri, g1) + (dot(tri, g2) + dot(tri, g3))


def _gdn_kernel(qkv_ref, z_ref, sm_ref, cw_ref, cb_ref, s0_ref, par_ref, nw_ref,
                o_ref, s_ref, ubuf, *, chunk, group):
    step = pl.program_id(1)
    rows = chunk * group

    @pl.when(step == 0)
    def _():
        ubuf[0:SUBLANES, :] = cb_ref[0]
        s_ref[0] = s0_ref[0]

    ubuf[SUBLANES:SUBLANES + rows, :] = qkv_ref[0]
    y = ubuf[SUBLANES - 3:SUBLANES - 3 + rows, :] * cw_ref[0:1, :]
    for i in range(1, CONV_W):
        y = y + ubuf[SUBLANES - 3 + i:SUBLANES - 3 + i + rows, :] * cw_ref[i:i + 1, :]
    ubuf[0:SUBLANES, :] = ubuf[rows:rows + SUBLANES, :]
    y = y * _sigmoid(y)

    sm = sm_ref[0]
    beta_all = _sigmoid(sm)
    xg = sm + par_ref[1:2, :]
    softplus = jnp.maximum(xg, 0.0) + jnp.log1p(jnp.exp(-jnp.abs(xg)))
    g_all = -jnp.exp(par_ref[0:1, :]) * softplus
    gc_all = _chunk_cumsum(g_all, chunk)
    gc_rows = gc_all.T

    heads = GDN_HEADS
    size = heads * chunk
    w_row = lax.broadcasted_iota(jnp.int32, (chunk, size), 0)
    w_lane = lax.broadcasted_iota(jnp.int32, (chunk, size), 1)
    w_col = w_lane % chunk
    w_head = w_lane // chunk
    causal_w = w_row >= w_col
    strict_w = w_row > w_col
    z = z_ref[0]
    nw = nw_ref[...]
    dot = functools.partial(jnp.dot, preferred_element_type=f32)

    def stack(a, rs, col0, width):
        return jnp.concatenate([a[rs, col0 + h * width:col0 + (h + 1) * width] for h in range(heads)], axis=0)

    def column_stack(a, rs, lane0):
        return jnp.concatenate([a[rs, lane0 + h:lane0 + h + 1] for h in range(heads)], axis=0)

    def column_wide(a, rs, lane0):
        out = jnp.zeros((chunk, size), f32)
        for h in range(heads):
            out = jnp.where(w_head == h, a[rs, lane0 + h:lane0 + h + 1], out)
        return out

    pre = []
    for g in range(group):
        rs = slice(g * chunk, (g + 1) * chunk)
        last = (g + 1) * chunk - 1
        q = stack(y, rs, 0, GDN_DK)
        k = stack(y, rs, GDN_QK, GDN_DK)
        v = stack(y, rs, 2 * GDN_QK, GDN_DV)
        q = q * lax.rsqrt(jnp.sum(q * q, -1, keepdims=True) + L2_EPS) * (GDN_DK ** -0.5)
        k = k * lax.rsqrt(jnp.sum(k * k, -1, keepdims=True) + L2_EPS)
        beta_s = column_stack(beta_all, rs, SM_BETA)
        gc_s = column_stack(gc_all, rs, SM_DECAY)
        g_last = [gc_all[last:last + 1, SM_DECAY + h:SM_DECAY + h + 1] for h in range(heads)]
        g_last_s = jnp.concatenate([jnp.broadcast_to(gl, (chunk, 1)) for gl in g_last], axis=0)
        gc_row_w = jnp.concatenate([gc_rows[SM_DECAY + h:SM_DECAY + h + 1, rs] for h in range(heads)], axis=1)
        decay_w = jnp.exp(jnp.where(causal_w, column_wide(gc_all, rs, SM_DECAY) - gc_row_w, -jnp.inf))
        kk_w = _blocks_to_wide(_mm_nt(k, k), chunk)
        a_w = jnp.where(strict_w, column_wide(beta_all, rs, SM_BETA) * kk_w * decay_w, 0.0)
        pre.append(dict(q=q, k=k, v=v, beta_s=beta_s, gc_s=gc_s, g_last=g_last, g_last_s=g_last_s,
                        decay_w=decay_w, a_w=a_w))
    t_ws = _unit_lower_inverse_wide([pc["a_w"] for pc in pre], chunk)
    local = []
    for pc, t_w in zip(pre, t_ws):
        q, k, v, beta_s, gc_s = pc["q"], pc["k"], pc["v"], pc["beta_s"], pc["gc_s"]
        rhs = jnp.concatenate([v * beta_s, k * (beta_s * jnp.exp(gc_s))], axis=-1)
        t_hi, t_lo = _split(t_w)
        r_hi, r_lo = _split(rhs)
        tb_hi = _wide_to_blocks(t_hi, chunk)
        both = dot(jnp.concatenate([tb_hi, _wide_to_blocks(t_lo, chunk)], axis=0), r_hi)
        sol = both[:size] + both[size:] + dot(tb_hi, r_lo)
        local.append(dict(u0=sol[:, :GDN_DV], wk=sol[:, GDN_DV:],
                          qk=_mm_nt(q, k) * _wide_to_blocks(pc["decay_w"], chunk),
                          q_dec=q * jnp.exp(gc_s), k_dec=k * jnp.exp(pc["g_last_s"] - gc_s), g_last=pc["g_last"]))

    s = [s_ref[0, h] for h in range(heads)]
    for g in range(group):
        rs = slice(g * chunk, (g + 1) * chunk)
        lc = local[g]
        u_parts, o_parts = [], []
        for h in range(heads):
            hs = slice(h * chunk, (h + 1) * chunk)
            both = _mm(jnp.concatenate([lc["wk"][hs], lc["q_dec"][hs]], axis=0), s[h])
            u_parts.append(lc["u0"][hs] - both[:chunk])
            o_parts.append(both[chunk:])
        u = jnp.concatenate(u_parts, axis=0)
        o = jnp.concatenate(o_parts, axis=0) + _mm(lc["qk"], u)
        for h in range(heads):
            hs = slice(h * chunk, (h + 1) * chunk)
            s[h] = s[h] * jnp.exp(lc["g_last"][h]) + _mm_tn(lc["k_dec"][hs], u[hs])
        z_s = stack(z, rs, 0, GDN_DV)
        o = o * lax.rsqrt(jnp.mean(o * o, -1, keepdims=True) + RMS_EPS) * nw * (z_s * _sigmoid(z_s))
        for h in range(heads):
            o_ref[0, rs, h * GDN_DV:(h + 1) * GDN_DV] = o[h * chunk:(h + 1) * chunk].astype(o_ref.dtype)
    for h in range(heads):
        s_ref[0, h] = s[h]


def _gdn_call(qkv, z, sm, conv_w, conv_buf8, s0, par, norm_w, chunk):
    b, t, _ = qkv.shape
    group = min(GDN_CHUNKS_PER_STEP, t // chunk)
    rows = chunk * group
    assert t % rows == 0
    return pl.pallas_call(
        functools.partial(_gdn_kernel, chunk=chunk, group=group),
        out_shape=[jax.ShapeDtypeStruct((b, t, GDN_VW), MXU_DTYPE),
                   jax.ShapeDtypeStruct((b, GDN_HEADS, GDN_DK, GDN_DV), f32)],
        grid=(b, t // rows),
        in_specs=[pl.BlockSpec((1, rows, GDN_CONV_CH), lambda i, j: (i, j, 0)),
                  pl.BlockSpec((1, rows, GDN_VW), lambda i, j: (i, j, 0)),
                  pl.BlockSpec((1, rows, LANES), lambda i, j: (i, j, 0)),
                  pl.BlockSpec((CONV_W, GDN_CONV_CH), lambda i, j: (0, 0)),
                  pl.BlockSpec((1, SUBLANES, GDN_CONV_CH), lambda i, j: (i, 0, 0)),
                  pl.BlockSpec((1, GDN_HEADS, GDN_DK, GDN_DV), lambda i, j: (i, 0, 0, 0)),
                  pl.BlockSpec((2, LANES), lambda i, j: (0, 0)),
                  pl.BlockSpec((1, GDN_DV), lambda i, j: (0, 0))],
        out_specs=[pl.BlockSpec((1, rows, GDN_VW), lambda i, j: (i, j, 0)),
                   pl.BlockSpec((1, GDN_HEADS, GDN_DK, GDN_DV), lambda i, j: (i, 0, 0, 0))],
        scratch_shapes=[pltpu.VMEM((SUBLANES + rows, GDN_CONV_CH), f32)],
        compiler_params=_params("parallel", "arbitrary"),
        name="gdn",
    )(qkv, z, sm, conv_w, conv_buf8, s0, par, norm_w)


def _dsa_kernel(q_ref, qi_ref, sm_ref, k_ref, v_ref, ki_ref, o_ref,
                key_hi, key_lo, m_s, l_s, acc_s, *, qb, tk, ts, pos0, n_keys, topk):
    blk = pl.program_id(1)
    row = lax.broadcasted_iota(jnp.int32, (qb, 1), 0)
    q_pos = pos0 + blk * qb + row
    limit = jnp.minimum((q_pos // CHUNK + 1) * CHUNK, n_keys)
    last_limit = jnp.minimum(((pos0 + (blk + 1) * qb - 1) // CHUNK + 1) * CHUNK, n_keys)
    n_tiles = (last_limit + tk - 1) // tk
    n_stiles = (last_limit + ts - 1) // ts
    sub = tk // ts
    lane = lax.broadcasted_iota(jnp.int32, (qb, LANES), 1)
    lower_half = lane < DSA_DH
    col = lax.broadcasted_iota(jnp.int32, (qb, ts), 1)
    groups = ts // LANES

    sm = sm_ref[0]
    qi_heads = []
    for h in range(IDX_HEADS):
        pair = qi_ref[0, :, (h // 2) * LANES:(h // 2 + 1) * LANES]
        keep = lower_half if h % 2 == 0 else jnp.logical_not(lower_half)
        qi_heads.append(jnp.where(keep, pair, jnp.zeros_like(pair)))
    wi = [sm[:, SM_WI + h:SM_WI + h + 1] * (IDX_HEADS ** -0.5) for h in range(IDX_HEADS)]

    def score_tile(kt, carry):
        ki_t = ki_ref[0, pl.ds(pl.multiple_of(kt * ts, ts), ts), :]
        score = jnp.zeros((qb, ts), f32)
        for h in range(IDX_HEADS):
            score = score + jnp.maximum(_mm_nt(qi_heads[h], ki_t), 0.0) * wi[h]
        bits = pltpu.bitcast(score, jnp.int32)
        key = bits ^ ((bits >> 31) & 0x7FFFFFFF)
        key = jnp.where(kt * ts + col < limit, key, INT_MIN)
        key_hi[kt] = (key >> 16).astype(jnp.int16)
        key_lo[kt] = ((key & 0xFFFF) - HALF_BIAS).astype(jnp.int16)
        return carry

    lax.fori_loop(0, n_stiles, score_tile, 0)

    def fill_lowest(kt, carry):
        lowest = jnp.full((qb, ts), -HALF_BIAS, jnp.int16)
        key_hi[kt] = lowest
        key_lo[kt] = lowest
        return carry

    lax.fori_loop(n_stiles, n_tiles * sub, fill_lowest, 0)

    one16, zero16 = jnp.int16(1), jnp.int16(0)

    def lane_fold(h16):
        part = h16[:, 0:LANES]
        for g in range(1, groups):
            part = part + h16[:, g * LANES:(g + 1) * LANES]
        return part

    def row_total(acc):
        return jnp.sum(acc.astype(jnp.int32), axis=1, keepdims=True)

    zero_acc = jnp.zeros((qb, LANES), jnp.int16)

    def count(hit):
        return row_total(lax.fori_loop(0, n_stiles, lambda kt, acc: acc + lane_fold(hit(kt)), zero_acc))

    def kth_largest_half(ref, want):
        def bit_step(i, u):
            cand = u | (jnp.int32(1) << (15 - i))
            cand16 = (cand - HALF_BIAS).astype(jnp.int16)
            n_ge_cand = count(lambda kt: jnp.where(ref[kt] >= cand16, one16, zero16))
            return jnp.where(n_ge_cand >= want, cand, u)
        return lax.fori_loop(0, 16, bit_step, jnp.zeros((qb, 1), jnp.int32))

    hi_u = kth_largest_half(key_hi, topk)
    thr_hi = (hi_u - HALF_BIAS).astype(jnp.int16)

    def mask_low(kt, acc):
        hi = key_hi[kt]
        key_lo[kt] = jnp.where(hi == thr_hi, key_lo[kt], jnp.int16(-HALF_BIAS))
        return acc + lane_fold(jnp.where(hi > thr_hi, one16, zero16))

    n_hi_gt = row_total(lax.fori_loop(0, n_stiles, mask_low, zero_acc))
    lo_u = kth_largest_half(key_lo, topk - n_hi_gt)
    lo_u = jnp.where(jnp.logical_and(hi_u == 0, lo_u == 0), 1, lo_u)
    thr_lo = (lo_u - HALF_BIAS).astype(jnp.int16)

    def tie_band(kt, low_pred):
        return jnp.where(key_hi[kt] == thr_hi, jnp.where(low_pred(key_lo[kt]), one16, zero16), zero16)


    def band_counts(kt, accs):
        in_band = key_hi[kt] == thr_hi
        lo = key_lo[kt]
        gt = jnp.where(in_band, jnp.where(lo > thr_lo, one16, zero16), zero16)
        ge = jnp.where(in_band, jnp.where(lo >= thr_lo, one16, zero16), zero16)
        return accs[0] + lane_fold(gt), accs[1] + lane_fold(ge)

    acc_gt, acc_ge = lax.fori_loop(0, n_stiles, band_counts, (zero_acc, zero_acc))
    n_gt = n_hi_gt + row_total(acc_gt)
    n_ge = n_hi_gt + row_total(acc_ge)
    need = (topk - n_gt).astype(f32)
    surplus = jnp.max(jnp.where(n_ge > topk, 1, 0))

    q_pairs = []
    for pi in range(DSA_HEADS // 2):
        pair = q_ref[0, :, pi * LANES:(pi + 1) * LANES]
        zero = jnp.zeros_like(pair)
        q_pairs.append(jnp.concatenate([jnp.where(lower_half, pair, zero), jnp.where(lower_half, zero, pair)], axis=0))

    def attend(exact_ties):
        m_s[...] = jnp.full(m_s.shape, M_FLOOR, f32)
        l_s[...] = jnp.zeros(l_s.shape, f32)
        acc_s[...] = jnp.zeros(acc_s.shape, f32)
        tc_w = min(tk, TIE_CHUNK)
        if exact_ties:
            tr = lax.broadcasted_iota(jnp.int32, (tc_w, tc_w), 0)
            tc = lax.broadcasted_iota(jnp.int32, (tc_w, tc_w), 1)
            prefix = jnp.where(tr <= tc, 1.0, 0.0).astype(MXU_DTYPE)

        def body(kt, taken):
            pieces = [kt * sub + i for i in range(sub)]
            if exact_ties:
                hi = jnp.concatenate([key_hi[st] for st in pieces], axis=1).astype(jnp.int32)
                lo = jnp.concatenate([key_lo[st] for st in pieces], axis=1).astype(jnp.int32)
                t_hi = thr_hi.astype(jnp.int32)
                t_lo = thr_lo.astype(jnp.int32)
                in_band = hi == t_hi
                above = jnp.where(hi > t_hi, 1.0, jnp.where(in_band, jnp.where(lo > t_lo, 1.0, 0.0), 0.0))
                tie = jnp.where(in_band, jnp.where(lo == t_lo, 1.0, 0.0), 0.0)
                parts = []
                for c0 in range(0, tk, tc_w):
                    tie_c = tie[:, c0:c0 + tc_w]
                    rank = taken + _mm(tie_c, prefix)
                    parts.append(above[:, c0:c0 + tc_w] + jnp.where(rank <= need, tie_c, 0.0))
                    taken = rank[:, tc_w - 1:tc_w]
                sel = parts[0] if len(parts) == 1 else jnp.concatenate(parts, axis=1)
            else:
                sel = jnp.concatenate(
                    [jnp.where(key_hi[st] > thr_hi, one16, tie_band(st, lambda lo: lo >= thr_lo)) for st in pieces],
                    axis=1).astype(f32)
            bias = (1.0 - sel) * NEG_BIG
            bias2 = jnp.concatenate([bias, bias], axis=0)
            start = pl.multiple_of(kt * tk, tk)
            pairs = range(DSA_HEADS // 2)
            slabs = [slice(pi * LANES, (pi + 1) * LANES) for pi in pairs]
            logits = [_mm_nt(q_pairs[pi], k_ref[0, pl.ds(start, tk), slabs[pi]]) + bias2 for pi in pairs]
            m_old = [m_s[pi] for pi in pairs]
            m_new = [jnp.maximum(m_old[pi], jnp.max(logits[pi], axis=1, keepdims=True)) for pi in pairs]
            alpha = [jnp.exp2(m_old[pi] - m_new[pi]) for pi in pairs]
            p = [jnp.exp2(logits[pi] - m_new[pi]) for pi in pairs]
            for pi in pairs:
                l_s[pi] = alpha[pi] * l_s[pi] + jnp.sum(p[pi], axis=1, keepdims=True)
                acc_s[pi] = alpha[pi] * acc_s[pi] + _mm(p[pi], v_ref[0, pl.ds(start, tk), slabs[pi]])
                m_s[pi] = m_new[pi]
            return taken

        lax.fori_loop(0, n_tiles, body, jnp.zeros((qb, 1), f32))
        for pi in range(DSA_HEADS // 2):
            out = acc_s[pi] / l_s[pi]
            o_ref[0, :, pi * LANES:(pi + 1) * LANES] = jnp.where(lower_half, out[:qb], out[qb:]).astype(o_ref.dtype)

    @pl.when(surplus == 0)
    def _():
        attend(False)

    @pl.when(surplus != 0)
    def _():
        attend(True)


def _dsa_call(qb_arr, qib_arr, sm, k_all, v_all, ki_all, pos0, n_keys, qb, tk):
    b, t, _ = qb_arr.shape
    lp = k_all.shape[1]
    ts = min(tk, DSA_SEARCH_TILE)
    assert lp % tk == 0 and t % qb == 0 and tk % ts == 0
    topk = min(TOPK_MAX, n_keys // 4)
    kern = functools.partial(_dsa_kernel, qb=qb, tk=tk, ts=ts, pos0=pos0, n_keys=n_keys, topk=topk)
    resident = dict(pipeline_mode=pl.Buffered(1))
    return pl.pallas_call(
        kern,
        out_shape=jax.ShapeDtypeStruct((b, t, DSA_W), MXU_DTYPE),
        grid=(b, t // qb),
        in_specs=[pl.BlockSpec((1, qb, DSA_W), lambda i, j: (i, j, 0)),
                  pl.BlockSpec((1, qb, IDX_W), lambda i, j: (i, j, 0)),
                  pl.BlockSpec((1, qb, LANES), lambda i, j: (i, j, 0)),
                  pl.BlockSpec((1, lp, DSA_W), lambda i, j: (i, 0, 0), **resident),
                  pl.BlockSpec((1, lp, DSA_W), lambda i, j: (i, 0, 0), **resident),
                  pl.BlockSpec((1, lp, LANES), lambda i, j: (i, 0, 0), **resident)],
        out_specs=pl.BlockSpec((1, qb, DSA_W), lambda i, j: (i, j, 0)),
        scratch_shapes=[pltpu.VMEM((lp // ts, qb, ts), jnp.int16),
                        pltpu.VMEM((lp // ts, qb, ts), jnp.int16),
                        pltpu.VMEM((DSA_HEADS // 2, 2 * qb, 1), f32),
                        pltpu.VMEM((DSA_HEADS // 2, 2 * qb, 1), f32),
                        pltpu.VMEM((DSA_HEADS // 2, 2 * qb, LANES), f32)],
        compiler_params=_params("parallel", "arbitrary"),
        name="dsa",
    )(qb_arr, qib_arr, sm, k_all, v_all, ki_all)


def _outproj_kernel(h_ref, og_ref, od_ref, w_ref, g_ref, b_ref, o_ref, *, alpha):
    mix = (jnp.dot(og_ref[...], w_ref[0:GDN_VW, :], preferred_element_type=f32)
           + jnp.dot(od_ref[...], w_ref[GDN_VW:, :], preferred_element_type=f32))
    o_ref[...] = _layer_norm(alpha * h_ref[...] + mix, g_ref[...], b_ref[...])


def _outproj_call(h, og, od, w, g, b, alpha):
    n, d = h.shape
    tm = _row_tile(n, PROJ_ROW_TILE)
    return pl.pallas_call(
        functools.partial(_outproj_kernel, alpha=alpha),
        out_shape=jax.ShapeDtypeStruct((n, d), f32),
        grid=(n // tm,),
        in_specs=[pl.BlockSpec((tm, d), lambda i: (i, 0)),
                  pl.BlockSpec((tm, GDN_VW), lambda i: (i, 0)),
                  pl.BlockSpec((tm, DSA_W), lambda i: (i, 0)),
                  pl.BlockSpec((GDN_VW + DSA_W, d), lambda i: (0, 0)),
                  pl.BlockSpec((1, d), lambda i: (0, 0)),
                  pl.BlockSpec((1, d), lambda i: (0, 0))],
        out_specs=pl.BlockSpec((tm, d), lambda i: (i, 0)),
        compiler_params=_params("parallel"),
        name="outproj_ln",
    )(h, og, od, w, g.reshape(1, d), b.reshape(1, d))


def _ffn_kernel(x_ref, wg_ref, wu_ref, wd_ref, g_ref, b_ref, o_ref, acc, *, alpha):
    j = pl.program_id(1)

    @pl.when(j == 0)
    def _():
        acc[...] = jnp.zeros(acc.shape, f32)

    xb = x_ref[...].astype(MXU_DTYPE)
    gate = jnp.dot(xb, wg_ref[...], preferred_element_type=f32)
    up = jnp.dot(xb, wu_ref[...], preferred_element_type=f32)
    act = gate * _sigmoid(gate) * up
    acc[...] += jnp.dot(act.astype(MXU_DTYPE), wd_ref[...], preferred_element_type=f32)

    @pl.when(j == pl.num_programs(1) - 1)
    def _():
        o_ref[...] = _layer_norm(alpha * x_ref[...] + acc[...], g_ref[...], b_ref[...])


def _ffn_call(x, wg, wu, wd, g, b, alpha):
    n, d = x.shape
    f = wg.shape[1]
    tm = _row_tile(n, FFN_ROW_TILE)
    tf = _row_tile(f, FFN_HIDDEN_TILE)
    return pl.pallas_call(
        functools.partial(_ffn_kernel, alpha=alpha),
        out_shape=jax.ShapeDtypeStruct((n, d), f32),
        grid=(n // tm, f // tf),
        in_specs=[pl.BlockSpec((tm, d), lambda i, j: (i, 0)),
                  pl.BlockSpec((d, tf), lambda i, j: (0, j)),
                  pl.BlockSpec((d, tf), lambda i, j: (0, j)),
                  pl.BlockSpec((tf, d), lambda i, j: (j, 0)),
                  pl.BlockSpec((1, d), lambda i, j: (0, 0)),
                  pl.BlockSpec((1, d), lambda i, j: (0, 0))],
        out_specs=pl.BlockSpec((tm, d), lambda i, j: (i, 0)),
        scratch_shapes=[pltpu.VMEM((tm, d), f32)],
        compiler_params=_params("parallel", "arbitrary"),
        name="ffn_ln",
    )(x, wg, wu, wd, g.reshape(1, d), b.reshape(1, d))


def _moe_plan_kernel(x_ref, wr_ref, plan_o, plan_t_o, xb_o, *, n_experts):
    tm = x_ref.shape[0]
    lane = lax.broadcasted_iota(jnp.int32, (tm, LANES), 1)
    x = x_ref[...]
    xb_o[...] = x.astype(xb_o.dtype)
    logits = _mm(x, wr_ref[...])
    logits = jnp.where(lane < n_experts, logits, -jnp.inf)
    m1 = jnp.max(logits, axis=1, keepdims=True)
    i1 = jnp.min(jnp.where(logits == m1, lane, LANES), axis=1, keepdims=True)
    rest = jnp.where(lane == i1, -jnp.inf, logits)
    m2 = jnp.max(rest, axis=1, keepdims=True)
    i2 = jnp.min(jnp.where(rest == m2, lane, LANES), axis=1, keepdims=True)
    e2 = jnp.exp(m2 - m1)
    w1 = 1.0 / (1.0 + e2)
    gates = jnp.where(lane == i1, w1, 0.0) + jnp.where(lane == i2, e2 * w1, 0.0)
    chosen = jnp.where(lane == i1, 1.0, 0.0) + jnp.where(lane == i2, 1.0, 0.0)
    blk = min(tm, MOE_PREFIX_BLOCK)
    r = lax.broadcasted_iota(jnp.int32, (blk, blk), 0)
    c = lax.broadcasted_iota(jnp.int32, (blk, blk), 1)
    tri = jnp.where(r >= c, 1.0, 0.0).astype(MXU_DTYPE)
    carry = jnp.zeros((1, LANES), f32)
    slots = []
    for b0 in range(0, tm, blk):
        m = chosen[b0:b0 + blk]
        inclusive = jnp.dot(tri, m.astype(MXU_DTYPE), preferred_element_type=f32) + carry
        slots.append(inclusive - m)
        carry = inclusive[blk - 1:blk, :]
    slot = jnp.concatenate(slots, axis=0) if len(slots) > 1 else slots[0]
    plan = jnp.where(lane < n_experts, gates,
                     jnp.where(lane < 2 * n_experts, pltpu.roll(slot, n_experts, 1),
                               jnp.where(lane < 3 * n_experts, pltpu.roll(chosen, 2 * n_experts, 1), 0.0)))
    plan_o[...] = plan
    plan_t_o[...] = plan.T[0:MOE_PLAN_ROWS, :]


def _moe_expert_kernel(xb_ref, plan_ref, plan_t_ref, wg_ref, wu_ref, wd_ref, y_ref,
                       xc, acc_c, *, n_experts, slot_block):
    e = pl.program_id(1)
    j = pl.program_id(2)
    tm = xb_ref.shape[0]
    cb = slot_block
    slot_row = plan_t_ref[pl.ds(e + n_experts, 1), :]
    chosen_row = plan_t_ref[pl.ds(e + 2 * n_experts, 1), :]
    count = jnp.max(jnp.where(chosen_row > 0.0, slot_row + 1.0, 0.0)).astype(jnp.int32)
    n_blocks = (count + cb - 1) // cb
    dot = functools.partial(jnp.dot, preferred_element_type=f32)

    def one_hot(sb):
        slot_id = (lax.broadcasted_iota(jnp.int32, (cb, 1), 0) + sb * cb).astype(f32)
        hit = jnp.where(chosen_row > 0.0, jnp.where(slot_row == slot_id, 1.0, 0.0), 0.0)
        return hit.astype(MXU_DTYPE)

    def rows_of(sb):
        return pl.ds(pl.multiple_of(sb * cb, cb), cb)

    @pl.when(jnp.logical_and(e == 0, j == 0))
    def _():
        y_ref[...] = jnp.zeros(y_ref.shape, f32)

    @pl.when(j == 0)
    def _():
        def gather(sb, carry):
            xc[rows_of(sb), :] = dot(one_hot(sb), xb_ref[...]).astype(xc.dtype)
            acc_c[rows_of(sb), :] = jnp.zeros((cb, acc_c.shape[1]), f32)
            return carry
        lax.fori_loop(0, n_blocks, gather, 0)

    def expert(sb, carry):
        xs = xc[rows_of(sb), :]
        gate = dot(xs, wg_ref[0])
        up = dot(xs, wu_ref[0])
        act = gate * _sigmoid(gate) * up
        acc_c[rows_of(sb), :] += dot(act.astype(MXU_DTYPE), wd_ref[0])
        return carry

    lax.fori_loop(0, n_blocks, expert, 0)

    @pl.when(j == pl.num_programs(2) - 1)
    def _():
        lane = lax.broadcasted_iota(jnp.int32, (tm, LANES), 1)
        gate_col = jnp.sum(jnp.where(lane == e, plan_ref[...], 0.0), axis=1, keepdims=True)

        def scatter(sb, carry):
            back = lax.dot_general(one_hot(sb), acc_c[rows_of(sb), :].astype(MXU_DTYPE),
                                   (((0,), (0,)), ((), ())), preferred_element_type=f32)
            y_ref[...] += gate_col * back
            return carry
        lax.fori_loop(0, n_blocks, scatter, 0)


def _residual_ln_kernel(x_ref, y_ref, g_ref, b_ref, o_ref, *, alpha):
    o_ref[...] = _layer_norm(alpha * x_ref[...] + y_ref[...], g_ref[...], b_ref[...])


def _moe_call(x, wr, wg, wu, wd, g, b, alpha):
    n, d = x.shape
    n_experts, _, f = wg.shape
    assert 3 * n_experts <= MOE_PLAN_ROWS and TOP_K_EXPERTS == 2
    tm = _row_tile(n, MOE_TOKEN_TILE)
    tf = _row_tile(f, FFN_HIDDEN_TILE)
    cb = min(MOE_SLOT_BLOCK, tm)
    cap = -(-tm // cb) * cb
    plan, plan_t, xb = pl.pallas_call(
        functools.partial(_moe_plan_kernel, n_experts=n_experts),
        out_shape=[jax.ShapeDtypeStruct((n, LANES), f32),
                   jax.ShapeDtypeStruct((MOE_PLAN_ROWS, n), f32),
                   jax.ShapeDtypeStruct((n, d), MXU_DTYPE)],
        grid=(n // tm,),
        in_specs=[pl.BlockSpec((tm, d), lambda i: (i, 0)),
                  pl.BlockSpec((d, LANES), lambda i: (0, 0))],
        out_specs=[pl.BlockSpec((tm, LANES), lambda i: (i, 0)),
                   pl.BlockSpec((MOE_PLAN_ROWS, tm), lambda i: (0, i)),
                   pl.BlockSpec((tm, d), lambda i: (i, 0))],
        compiler_params=_params("parallel"),
        name="moe_plan",
    )(x, wr)
    y = pl.pallas_call(
        functools.partial(_moe_expert_kernel, n_experts=n_experts, slot_block=cb),
        out_shape=jax.ShapeDtypeStruct((n, d), f32),
        grid=(n // tm, n_experts, f // tf),
        in_specs=[pl.BlockSpec((tm, d), lambda i, e, j: (i, 0)),
                  pl.BlockSpec((tm, LANES), lambda i, e, j: (i, 0)),
                  pl.BlockSpec((MOE_PLAN_ROWS, tm), lambda i, e, j: (0, i)),
                  pl.BlockSpec((1, d, tf), lambda i, e, j: (e, 0, j)),
                  pl.BlockSpec((1, d, tf), lambda i, e, j: (e, 0, j)),
                  pl.BlockSpec((1, tf, d), lambda i, e, j: (e, j, 0))],
        out_specs=pl.BlockSpec((tm, d), lambda i, e, j: (i, 0)),
        scratch_shapes=[pltpu.VMEM((cap, d), MXU_DTYPE),
                        pltpu.VMEM((cap, d), f32)],
        compiler_params=_params("parallel", "arbitrary", "arbitrary"),
        name="moe_experts",
    )(xb, plan, plan_t, wg, wu, wd)
    tr = _row_tile(n, LN_ROW_TILE)
    return pl.pallas_call(
        functools.partial(_residual_ln_kernel, alpha=alpha),
        out_shape=jax.ShapeDtypeStruct((n, d), f32),
        grid=(n // tr,),
        in_specs=[pl.BlockSpec((tr, d), lambda i: (i, 0)),
                  pl.BlockSpec((tr, d), lambda i: (i, 0)),
                  pl.BlockSpec((1, d), lambda i: (0, 0)),
                  pl.BlockSpec((1, d), lambda i: (0, 0))],
        out_specs=pl.BlockSpec((tr, d), lambda i: (i, 0)),
        compiler_params=_params("parallel"),
        name="moe_residual_ln",
    )(x, y, g.reshape(1, d), b.reshape(1, d))


def _pack_w_in(w_in):
    o_z = GDN_CONV_CH
    o_b = o_z + GDN_VW
    o_a = o_b + GDN_HEADS
    o_qd = o_a + GDN_HEADS
    o_kd = o_qd + DSA_W
    o_vd = o_kd + DSA_W
    o_qi = o_vd + DSA_W
    o_ki = o_qi + IDX_W
    o_wi = o_ki + IDX_DIM
    assert w_in.shape[-1] == o_wi + IDX_HEADS
    sl = lambda a, n: w_in[..., a:a + n]
    pad = jnp.zeros(w_in.shape[:-1] + (LANES - 2 * GDN_HEADS - IDX_HEADS,), w_in.dtype)
    packed = jnp.concatenate([
        sl(0, GDN_CONV_CH), sl(o_z, GDN_VW), sl(o_qd, DSA_W), sl(o_kd, DSA_W), sl(o_vd, DSA_W),
        sl(o_qi, IDX_W), sl(o_ki, IDX_DIM), sl(o_ki, IDX_DIM),
        sl(o_b, GDN_HEADS), sl(o_a, GDN_HEADS), sl(o_wi, IDX_HEADS), pad], axis=-1)
    assert packed.shape[-1] == C_END
    return packed.astype(MXU_DTYPE)


def _rope_tables(pos):
    half = DSA_DH // 8
    inv_freq = ROPE_THETA ** (-jnp.arange(half, dtype=f32) / half)
    ang = pos.astype(f32)[:, None] * inv_freq[None, :]
    cos, sin = jnp.cos(ang), jnp.sin(ang)
    t = pos.shape[0]
    ones = jnp.ones((t, DSA_DH - 2 * half), f32)
    zeros = jnp.zeros((t, DSA_DH - 2 * half), f32)
    z8 = jnp.zeros((t, half), f32)
    cos_t = jnp.concatenate([cos, cos, ones], -1)
    lo = jnp.concatenate([-sin, z8, zeros], -1)
    hi = jnp.concatenate([z8, sin, zeros], -1)
    return tuple(jnp.tile(a, (1, LANES // DSA_DH)) for a in (cos_t, lo, hi))


def _gdn_params(a_log, dt_bias):
    row = lambda v: jnp.zeros((LANES,), f32).at[SM_DECAY:SM_DECAY + GDN_HEADS].set(v.astype(f32))
    return jnp.stack([row(a_log), row(dt_bias)])


def _trunk(x, pos0, conv_bufs, s0s, past_k, past_v, past_ik, chunk, wts, depth):
    b, t, d = x.shape
    n = b * t
    alpha = (2.0 * depth) ** 0.25
    tabs = _rope_tables(pos0 + jnp.arange(t))
    past = 0 if past_k is None else past_k.shape[2]
    n_keys = past + t
    assert t >= CONV_W - 1
    qb = min(t, DSA_Q_BLOCK)
    tk = DSA_KEY_TILE
    lp = -(-n_keys // tk) * tk

    def with_past(new, old, l):
        parts = [] if old is None else [old[l].reshape(b, past, -1).astype(MXU_DTYPE)]
        parts.append(new.reshape(b, t, -1))
        if lp > n_keys:
            parts.append(jnp.zeros((b, lp - n_keys, new.shape[-1]), MXU_DTYPE))
        return parts[0] if len(parts) == 1 else jnp.concatenate(parts, axis=1)

    h = _ln_call(x.reshape(n, d), wts["ln_in_g"], wts["ln_in_b"])
    new_conv, new_s = [], []
    stacks = ()
    for l in range(depth):
        (qkv, z, k_st, v_st, ki_st, q_b, k_b, v_b, qi_b, ki_b, sm) = _inproj_call(
            h, wts["w_in"][l], tabs, t, l, depth, stacks)
        stacks = (k_st, v_st, ki_st)
        qkv3 = qkv.reshape(b, t, GDN_CONV_CH)
        buf8 = jnp.concatenate([jnp.zeros((b, SUBLANES - (CONV_W - 1), GDN_CONV_CH), f32), conv_bufs[l]], axis=1)
        o_g, s_new = _gdn_call(qkv3, z.reshape(b, t, GDN_VW), sm.reshape(b, t, LANES), wts["conv_w"][l],
                               buf8, s0s[l], _gdn_params(wts["gdn_a_log"][l], wts["gdn_dt_bias"][l]),
                               wts["gdn_norm_w"][l].reshape(1, GDN_DV), chunk)
        ki_dup = None if past_ik is None else jnp.concatenate([past_ik, past_ik], axis=-1)
        o_d = _dsa_call(q_b.reshape(b, t, DSA_W), qi_b.reshape(b, t, IDX_W), sm.reshape(b, t, LANES),
                        with_past(k_b, past_k, l), with_past(v_b, past_v, l), with_past(ki_b, ki_dup, l),
                        past, n_keys, qb, tk)
        h = _outproj_call(h, o_g.reshape(n, GDN_VW), o_d.reshape(n, DSA_W), wts["w_out"][l],
                          wts["ln1_g"][l], wts["ln1_b"][l], alpha)
        if l % 2 == 0:
            i = l // 2
            h = _ffn_call(h, wts["ffn_wg"][i], wts["ffn_wu"][i], wts["ffn_wd"][i],
                          wts["ln2_g"][l], wts["ln2_b"][l], alpha)
        else:
            i = l // 2
            h = _moe_call(h, wts["moe_router"][i], wts["moe_wg"][i], wts["moe_wu"][i], wts["moe_wd"][i],
                          wts["ln2_g"][l], wts["ln2_b"][l], alpha)
        new_conv.append(qkv3[:, t - (CONV_W - 1):])
        new_s.append(s_new)
    k_st, v_st, ki_st = stacks
    return (h.reshape(b, t, d), k_st.reshape(depth, b, t, DSA_HEADS, DSA_DH),
            v_st.reshape(depth, b, t, DSA_HEADS, DSA_DH), ki_st.reshape(depth, b, t, IDX_DIM),
            jnp.stack(new_s), jnp.stack(new_conv))


def kernel(x_prompt, x_sample, cache_k, cache_v, cache_idx_k, state_gdn, state_conv, ln_in_g, ln_in_b, w_in, conv_w, gdn_a_log, gdn_dt_bias, gdn_norm_w, w_out, ln1_g, ln1_b, ln2_g, ln2_b, ffn_wg, ffn_wu, ffn_wd, moe_router, moe_wg, moe_wu, moe_wd):
    depth = w_in.shape[0]
    n_experts = moe_router.shape[-1]
    bp, tp, d = x_prompt.shape
    cast = lambda a: a.astype(MXU_DTYPE)
    router = jnp.concatenate(
        [moe_router.astype(f32), jnp.zeros(moe_router.shape[:-1] + (LANES - n_experts,), f32)], axis=-1)
    wts = dict(ln_in_g=ln_in_g, ln_in_b=ln_in_b, w_in=_pack_w_in(w_in), conv_w=conv_w,
               gdn_a_log=gdn_a_log, gdn_dt_bias=gdn_dt_bias, gdn_norm_w=gdn_norm_w, w_out=cast(w_out),
               ln1_g=ln1_g, ln1_b=ln1_b, ln2_g=ln2_g, ln2_b=ln2_b,
               ffn_wg=cast(ffn_wg), ffn_wu=cast(ffn_wu), ffn_wd=cast(ffn_wd),
               moe_router=router, moe_wg=cast(moe_wg), moe_wu=cast(moe_wu), moe_wd=cast(moe_wd))
    zero_conv = jnp.zeros((depth, bp, CONV_W - 1, GDN_CONV_CH), f32)
    zero_s = jnp.zeros((depth, bp, GDN_HEADS, GDN_DK, GDN_DV), f32)
    y_p, k_p, v_p, ik_p, s_p, conv_p = _trunk(
        x_prompt, 0, zero_conv, zero_s, None, None, None, CHUNK, wts, depth)
    ts = x_sample.shape[1]
    past = cache_k.shape[2]
    y_s, k_s, v_s, ik_s, s_s, conv_s = _trunk(
        x_sample, past, state_conv, state_gdn, cache_k, cache_v, cache_idx_k, ts, wts, depth)
    return (y_p, y_s, k_p, v_p, ik_p, s_p, conv_p, k_s, v_s, ik_s, s_s, conv_s)
```

```python
import functools

import jax
import jax.numpy as jnp
from jax import lax
from jax.experimental import pallas as pl
from jax.experimental.pallas import tpu as pltpu

CHUNK = 64
CONV_W = 4
GDN_HEADS = 4
GDN_DK = 128
GDN_DV = 128
DSA_HEADS = 8
DSA_DH = 64
IDX_HEADS = 4
IDX_DIM = 64
TOPK_MAX = 256
ROPE_THETA = 500000.0
TOP_K_EXPERTS = 2
LN_EPS = 1e-5
RMS_EPS = 1e-6
L2_EPS = 1e-6

GDN_QK = GDN_HEADS * GDN_DK
GDN_VW = GDN_HEADS * GDN_DV
GDN_CONV_CH = 2 * GDN_QK + GDN_VW
DSA_W = DSA_HEADS * DSA_DH
IDX_W = IDX_HEADS * IDX_DIM

LANES = 128
SUBLANES = 8
VMEM_LIMIT = 56 * 1024 * 1024

LN_ROW_TILE = 1024
PROJ_ROW_TILE = 512
FFN_ROW_TILE = 1024
FFN_HIDDEN_TILE = 512
DSA_Q_BLOCK = 256
DSA_KEY_TILE = 2048
DSA_SEARCH_TILE = 1024
GDN_CHUNKS_PER_STEP = 4
MOE_TOKEN_TILE = 2048
MOE_SLOT_BLOCK = 576
MOE_PREFIX_BLOCK = 256
MOE_PLAN_ROWS = 24

MXU_DTYPE = jnp.bfloat16

C_QKV = 0
C_Z = C_QKV + GDN_CONV_CH
C_QD = C_Z + GDN_VW
C_KD = C_QD + DSA_W
C_VD = C_KD + DSA_W
C_QI = C_VD + DSA_W
C_KI = C_QI + IDX_W
C_SM = C_KI + LANES
C_END = C_SM + LANES
SM_BETA = 0
SM_DECAY = GDN_HEADS
SM_WI = 2 * GDN_HEADS

INT_MIN = -2 ** 31
HALF_BIAS = 2 ** 15
TIE_CHUNK = 512
NEG_BIG = -3e38
M_FLOOR = -1e38
LOG2E = 1.4426950408889634

f32 = jnp.float32


def _mm(a, b):
    return jnp.dot(a.astype(MXU_DTYPE), b.astype(MXU_DTYPE), preferred_element_type=f32)


def _mm_nt(a, b):
    return lax.dot_general(a.astype(MXU_DTYPE), b.astype(MXU_DTYPE), (((1,), (1,)), ((), ())),
                           preferred_element_type=f32)


def _mm_tn(a, b):
    return lax.dot_general(a.astype(MXU_DTYPE), b.astype(MXU_DTYPE), (((0,), (0,)), ((), ())),
                           preferred_element_type=f32)


def _split(x):
    hi = x.astype(MXU_DTYPE)
    lo = (x - hi.astype(f32)).astype(MXU_DTYPE)
    return hi, lo


def _sigmoid(x):
    return 1.0 / (1.0 + jnp.exp(-x))


def _layer_norm(r, g, b):
    mu = jnp.mean(r, axis=-1, keepdims=True)
    xc = r - mu
    var = jnp.mean(xc * xc, axis=-1, keepdims=True)
    return xc * lax.rsqrt(var + LN_EPS) * g + b


def _params(*sem):
    return pltpu.CompilerParams(dimension_semantics=sem, vmem_limit_bytes=VMEM_LIMIT)


def _row_tile(n, want):
    t = min(n, want)
    assert n % t == 0, (n, t)
    return t


def _ln_kernel(x_ref, g_ref, b_ref, o_ref):
    o_ref[...] = _layer_norm(x_ref[...], g_ref[...], b_ref[...])


def _ln_call(x, g, b):
    n, d = x.shape
    tm = _row_tile(n, LN_ROW_TILE)
    return pl.pallas_call(
        _ln_kernel,
        out_shape=jax.ShapeDtypeStruct((n, d), f32),
        grid=(n // tm,),
        in_specs=[pl.BlockSpec((tm, d), lambda i: (i, 0)),
                  pl.BlockSpec((1, d), lambda i: (0, 0)),
                  pl.BlockSpec((1, d), lambda i: (0, 0))],
        out_specs=pl.BlockSpec((tm, d), lambda i: (i, 0)),
        compiler_params=_params("parallel"),
        name="ln_in",
    )(x, g.reshape(1, d), b.reshape(1, d))


def _rope(x, cos_t, sin_lo, sin_hi):
    w = x.shape[1]
    reps = w // LANES
    c = jnp.tile(cos_t, (1, reps))
    lo = jnp.tile(sin_lo, (1, reps))
    hi = jnp.tile(sin_hi, (1, reps))
    half = DSA_DH // 8
    from_above = pltpu.roll(x, w - half, 1)
    from_below = pltpu.roll(x, half, 1)
    return x * c + from_above * lo + from_below * hi


N_INPROJ_OUTPUTS = 11


def _inproj_kernel(x_ref, w_ref, cos_ref, lo_ref, hi_ref, *refs):
    qkv_o, z_o, k_o, v_o, ki_o, qb_o, kb_o, vb_o, qib_o, kib_o, sm_o = refs[-N_INPROJ_OUTPUTS:]
    xb = x_ref[...].astype(MXU_DTYPE)
    cos_t, lo, hi = cos_ref[...], lo_ref[...], hi_ref[...]

    def proj(c0, c1):
        return jnp.dot(xb, w_ref[:, c0:c1], preferred_element_type=f32)

    qkv_o[...] = proj(C_QKV, C_Z)
    z_o[...] = proj(C_Z, C_QD)
    q = _rope(proj(C_QD, C_KD), cos_t, lo, hi) * (DSA_DH ** -0.5 * LOG2E)
    qb_o[...] = q.astype(qb_o.dtype)
    k = _rope(proj(C_KD, C_VD), cos_t, lo, hi)
    k_o[...] = k
    kb_o[...] = k.astype(kb_o.dtype)
    v = proj(C_VD, C_QI)
    v_o[...] = v
    vb_o[...] = v.astype(vb_o.dtype)
    qi = _rope(proj(C_QI, C_KI), cos_t, lo, hi) * (IDX_DIM ** -0.5)
    qib_o[...] = qi.astype(qib_o.dtype)
    ki = _rope(proj(C_KI, C_SM), cos_t, lo, hi)
    ki_o[...] = ki[:, :IDX_DIM]
    kib_o[...] = ki.astype(kib_o.dtype)
    sm_o[...] = proj(C_SM, C_END)


def _inproj_call(h, w_packed, tabs, seq_len, layer, depth, stacks):
    n, d = h.shape
    tm = _row_tile(n, PROJ_ROW_TILE)
    cos_t, lo, hi = tabs
    if tm > seq_len:
        assert tm % seq_len == 0
        cos_t, lo, hi = (jnp.tile(a, (tm // seq_len, 1)) for a in (cos_t, lo, hi))
        nblk = 1
    else:
        assert seq_len % tm == 0
        nblk = seq_len // tm
    tab_spec = pl.BlockSpec((tm, LANES), lambda i: (i % nblk, 0))

    def rows(w):
        return pl.BlockSpec((tm, w), lambda i: (i, 0))

    widths = [(GDN_CONV_CH, f32), (GDN_VW, f32), (DSA_W, f32), (DSA_W, f32), (IDX_DIM, f32),
              (DSA_W, MXU_DTYPE), (DSA_W, MXU_DTYPE), (DSA_W, MXU_DTYPE), (IDX_W, MXU_DTYPE),
              (LANES, MXU_DTYPE), (LANES, f32)]
    stacked = (2, 3, 4)
    out_shape = [jax.ShapeDtypeStruct((n, w), dt) for w, dt in widths]
    out_specs = [rows(w) for w, _ in widths]
    for o in stacked:
        w, dt = widths[o]
        out_shape[o] = jax.ShapeDtypeStruct((depth, n, w), dt)
        out_specs[o] = pl.BlockSpec((None, tm, w), lambda i: (layer, i, 0))
    n_in = 5
    assert len(stacks) in (0, len(stacked))
    return pl.pallas_call(
        _inproj_kernel,
        out_shape=out_shape,
        grid=(n // tm,),
        in_specs=[rows(d),
                  pl.BlockSpec((d, C_END), lambda i: (0, 0), pipeline_mode=pl.Buffered(1)),
                  tab_spec, tab_spec, tab_spec] + [pl.BlockSpec(memory_space=pl.ANY) for _ in stacks],
        out_specs=out_specs,
        input_output_aliases={n_in + j: stacked[j] for j in range(len(stacks))},
        compiler_params=_params("parallel"),
        name="inproj",
    )(h, w_packed, cos_t, lo, hi, *stacks)


def _wide_to_blocks(w, chunk):
    size = w.shape[1]
    r = lax.broadcasted_iota(jnp.int32, (size, size), 0)
    c = lax.broadcasted_iota(jnp.int32, (size, size), 1)
    tiled = jnp.concatenate([w] * (size // chunk), axis=0)
    return jnp.where(r // chunk == c // chunk, tiled, jnp.zeros_like(tiled))


def _blocks_to_wide(sq, chunk):
    size = sq.shape[1]
    blk = lax.broadcasted_iota(jnp.int32, (chunk, size), 1) // chunk
    out = jnp.zeros((chunk, size), sq.dtype)
    for h in range(size // chunk):
        out = jnp.where(blk == h, sq[h * chunk:(h + 1) * chunk, :], out)
    return out


def _wide_matmul(x, p, chunk):
    x_hi, x_lo = _split(x)
    p_hi, p_lo = _split(p)
    dot = functools.partial(jnp.dot, preferred_element_type=f32)
    both = dot(jnp.concatenate([x_hi, x_lo], axis=0), _wide_to_blocks(p_hi, chunk))
    return both[:chunk] + both[chunk:] + dot(x_hi, _wide_to_blocks(p_lo, chunk))


def _unit_lower_inverse_wide(a_list, chunk):
    size = a_list[0].shape[1]
    r = lax.broadcasted_iota(jnp.int32, (chunk, size), 0)
    c = lax.broadcasted_iota(jnp.int32, (chunk, size), 1) % chunk

    def same_block(n):
        return (r // n) == (c // n)

    base = SUBLANES
    eye = jnp.where(r == c, 1.0, 0.0)
    a0 = [jnp.where(same_block(base), a, 0.0) for a in a_list]
    p = [_wide_matmul(m, m, chunk) for m in a0]
    x = [eye - m for m in a0]
    x = [xi + _wide_matmul(xi, pi, chunk) for xi, pi in zip(x, p)]
    p = [_wide_matmul(pi, pi, chunk) for pi in p]
    x = [xi + _wide_matmul(xi, pi, chunk) for xi, pi in zip(x, p)]
    n = base
    while n < chunk:
        off = [jnp.where(same_block(2 * n), jnp.where(same_block(n), 0.0, a), 0.0) for a in a_list]
        t = [_wide_matmul(xi, oi, chunk) for xi, oi in zip(x, off)]
        x = [xi - _wide_matmul(ti, xi, chunk) for xi, ti in zip(x, t)]
        n *= 2
    return x


def _chunk_cumsum(g, chunk):
    rows = g.shape[0]
    r = lax.broadcasted_iota(jnp.int32, (rows, rows), 0)
    c = lax.broadcasted_iota(jnp.int32, (rows, rows), 1)
    tri = jnp.where(jnp.logical_and(r // chunk == c // chunk, r >= c), 1.0, 0.0).astype(MXU_DTYPE)
    dot = functools.partial(jnp.dot, preferred_element_type=f32)
    g1 = g.astype(MXU_DTYPE)
    rem = g - g1.astype(f32)
    g2 = rem.astype(MXU_DTYPE)
    g3 = (rem - g2.astype(f32)).astype(MXU_DTYPE)
    return dot(tri, g1) + (dot(tri, g2) + dot(tri, g3))


def _gdn_kernel(qkv_ref, z_ref, sm_ref, cw_ref, cb_ref, s0_ref, par_ref, nw_ref,
                o_ref, s_ref, ubuf, *, chunk, group):
    step = pl.program_id(1)
    rows = chunk * group

    @pl.when(step == 0)
    def _():
        ubuf[0:SUBLANES, :] = cb_ref[0]
        s_ref[0] = s0_ref[0]

    ubuf[SUBLANES:SUBLANES + rows, :] = qkv_ref[0]
    y = ubuf[SUBLANES - 3:SUBLANES - 3 + rows, :] * cw_ref[0:1, :]
    for i in range(1, CONV_W):
        y = y + ubuf[SUBLANES - 3 + i:SUBLANES - 3 + i + rows, :] * cw_ref[i:i + 1, :]
    ubuf[0:SUBLANES, :] = ubuf[rows:rows + SUBLANES, :]
    y = y * _sigmoid(y)

    sm = sm_ref[0]
    beta_all = _sigmoid(sm)
    xg = sm + par_ref[1:2, :]
    softplus = jnp.maximum(xg, 0.0) + jnp.log1p(jnp.exp(-jnp.abs(xg)))
    g_all = -jnp.exp(par_ref[0:1, :]) * softplus
    gc_all = _chunk_cumsum(g_all, chunk)
    gc_rows = gc_all.T

    heads = GDN_HEADS
    size = heads * chunk
    w_row = lax.broadcasted_iota(jnp.int32, (chunk, size), 0)
    w_lane = lax.broadcasted_iota(jnp.int32, (chunk, size), 1)
    w_col = w_lane % chunk
    w_head = w_lane // chunk
    causal_w = w_row >= w_col
    strict_w = w_row > w_col
    z = z_ref[0]
    nw = nw_ref[...]
    dot = functools.partial(jnp.dot, preferred_element_type=f32)

    def stack(a, rs, col0, width):
        return jnp.concatenate([a[rs, col0 + h * width:col0 + (h + 1) * width] for h in range(heads)], axis=0)

    def column_stack(a, rs, lane0):
        return jnp.concatenate([a[rs, lane0 + h:lane0 + h + 1] for h in range(heads)], axis=0)

    def column_wide(a, rs, lane0):
        out = jnp.zeros((chunk, size), f32)
        for h in range(heads):
            out = jnp.where(w_head == h, a[rs, lane0 + h:lane0 + h + 1], out)
        return out

    pre = []
    for g in range(group):
        rs = slice(g * chunk, (g + 1) * chunk)
        last = (g + 1) * chunk - 1
        q = stack(y, rs, 0, GDN_DK)
        k = stack(y, rs, GDN_QK, GDN_DK)
        v = stack(y, rs, 2 * GDN_QK, GDN_DV)
        q = q * lax.rsqrt(jnp.sum(q * q, -1, keepdims=True) + L2_EPS) * (GDN_DK ** -0.5)
        k = k * lax.rsqrt(jnp.sum(k * k, -1, keepdims=True) + L2_EPS)
        beta_s = column_stack(beta_all, rs, SM_BETA)
        gc_s = column_stack(gc_all, rs, SM_DECAY)
        g_last = [gc_all[last:last + 1, SM_DECAY + h:SM_DECAY + h + 1] for h in range(heads)]
        g_last_s = jnp.concatenate([jnp.broadcast_to(gl, (chunk, 1)) for gl in g_last], axis=0)
        gc_row_w = jnp.concatenate([gc_rows[SM_DECAY + h:SM_DECAY + h + 1, rs] for h in range(heads)], axis=1)
        decay_w = jnp.exp(jnp.where(causal_w, column_wide(gc_all, rs, SM_DECAY) - gc_row_w, -jnp.inf))
        kk_w = _blocks_to_wide(_mm_nt(k, k), chunk)
        a_w = jnp.where(strict_w, column_wide(beta_all, rs, SM_BETA) * kk_w * decay_w, 0.0)
        pre.append(dict(q=q, k=k, v=v, beta_s=beta_s, gc_s=gc_s, g_last=g_last, g_last_s=g_last_s,
                        decay_w=decay_w, a_w=a_w))
    t_ws = _unit_lower_inverse_wide([pc["a_w"] for pc in pre], chunk)
    local = []
    for pc, t_w in zip(pre, t_ws):
        q, k, v, beta_s, gc_s = pc["q"], pc["k"], pc["v"], pc["beta_s"], pc["gc_s"]
        rhs = jnp.concatenate([v * beta_s, k * (beta_s * jnp.exp(gc_s))], axis=-1)
        t_hi, t_lo = _split(t_w)
        r_hi, r_lo = _split(rhs)
        tb_hi = _wide_to_blocks(t_hi, chunk)
        both = dot(jnp.concatenate([tb_hi, _wide_to_blocks(t_lo, chunk)], axis=0), r_hi)
        sol = both[:size] + both[size:] + dot(tb_hi, r_lo)
        local.append(dict(u0=sol[:, :GDN_DV], wk=sol[:, GDN_DV:],
                          qk=_mm_nt(q, k) * _wide_to_blocks(pc["decay_w"], chunk),
                          q_dec=q * jnp.exp(gc_s), k_dec=k * jnp.exp(pc["g_last_s"] - gc_s), g_last=pc["g_last"]))

    s = [s_ref[0, h] for h in range(heads)]
    for g in range(group):
        rs = slice(g * chunk, (g + 1) * chunk)
        lc = local[g]
        u_parts, o_parts = [], []
        for h in range(heads):
            hs = slice(h * chunk, (h + 1) * chunk)
            both = _mm(jnp.concatenate([lc["wk"][hs], lc["q_dec"][hs]], axis=0), s[h])
            u_parts.append(lc["u0"][hs] - both[:chunk])
            o_parts.append(both[chunk:])
        u = jnp.concatenate(u_parts, axis=0)
        o = jnp.concatenate(o_parts, axis=0) + _mm(lc["qk"], u)
        for h in range(heads):
            hs = slice(h * chunk, (h + 1) * chunk)
            s[h] = s[h] * jnp.exp(lc["g_last"][h]) + _mm_tn(lc["k_dec"][hs], u[hs])
        z_s = stack(z, rs, 0, GDN_DV)
        o = o * lax.rsqrt(jnp.mean(o * o, -1, keepdims=True) + RMS_EPS) * nw * (z_s * _sigmoid(z_s))
        for h in range(heads):
            o_ref[0, rs, h * GDN_DV:(h + 1) * GDN_DV] = o[h * chunk:(h + 1) * chunk].astype(o_ref.dtype)
    for h in range(heads):
        s_ref[0, h] = s[h]


def _gdn_call(qkv, z, sm, conv_w, conv_buf8, s0, par, norm_w, chunk):
    b, t, _ = qkv.shape
    group = min(GDN_CHUNKS_PER_STEP, t // chunk)
    rows = chunk * group
    assert t % rows == 0
    return pl.pallas_call(
        functools.partial(_gdn_kernel, chunk=chunk, group=group),
        out_shape=[jax.ShapeDtypeStruct((b, t, GDN_VW), MXU_DTYPE),
                   jax.ShapeDtypeStruct((b, GDN_HEADS, GDN_DK, GDN_DV), f32)],
        grid=(b, t // rows),
        in_specs=[pl.BlockSpec((1, rows, GDN_CONV_CH), lambda i, j: (i, j, 0)),
                  pl.BlockSpec((1, rows, GDN_VW), lambda i, j: (i, j, 0)),
                  pl.BlockSpec((1, rows, LANES), lambda i, j: (i, j, 0)),
                  pl.BlockSpec((CONV_W, GDN_CONV_CH), lambda i, j: (0, 0)),
                  pl.BlockSpec((1, SUBLANES, GDN_CONV_CH), lambda i, j: (i, 0, 0)),
                  pl.BlockSpec((1, GDN_HEADS, GDN_DK, GDN_DV), lambda i, j: (i, 0, 0, 0)),
                  pl.BlockSpec((2, LANES), lambda i, j: (0, 0)),
                  pl.BlockSpec((1, GDN_DV), lambda i, j: (0, 0))],
        out_specs=[pl.BlockSpec((1, rows, GDN_VW), lambda i, j: (i, j, 0)),
                   pl.BlockSpec((1, GDN_HEADS, GDN_DK, GDN_DV), lambda i, j: (i, 0, 0, 0))],
        scratch_shapes=[pltpu.VMEM((SUBLANES + rows, GDN_CONV_CH), f32)],
        compiler_params=_params("parallel", "arbitrary"),
        name="gdn",
    )(qkv, z, sm, conv_w, conv_buf8, s0, par, norm_w)


def _dsa_kernel(q_ref, qi_ref, sm_ref, k_ref, v_ref, ki_ref, o_ref,
                key_hi, key_lo, m_s, l_s, acc_s, *, qb, tk, ts, pos0, n_keys, topk):
    blk = pl.program_id(1)
    q_lane = lax.broadcasted_iota(jnp.int32, (1, qb), 1)
    limit = jnp.minimum(((pos0 + blk * qb + q_lane) // CHUNK + 1) * CHUNK, n_keys)
    last_limit = jnp.minimum(((pos0 + (blk + 1) * qb - 1) // CHUNK + 1) * CHUNK, n_keys)
    n_tiles = (last_limit + tk - 1) // tk
    n_stiles = (last_limit + ts - 1) // ts
    sub = tk // ts
    lane = lax.broadcasted_iota(jnp.int32, (qb, LANES), 1)
    lower_half = lane < DSA_DH
    key_row = lax.broadcasted_iota(jnp.int32, (ts, 1), 0)
    pack = 2 * SUBLANES

    sm_t = sm_ref[0].T
    qi_heads = []
    for h in range(IDX_HEADS):
        pair = qi_ref[0, :, (h // 2) * LANES:(h // 2 + 1) * LANES]
        keep = lower_half if h % 2 == 0 else jnp.logical_not(lower_half)
        qi_heads.append(jnp.where(keep, pair, jnp.zeros_like(pair)))
    wi = [sm_t[SM_WI + h:SM_WI + h + 1, :] * (IDX_HEADS ** -0.5) for h in range(IDX_HEADS)]

    def score_tile(kt, carry):
        ki_t = ki_ref[0, pl.ds(pl.multiple_of(kt * ts, ts), ts), :]
        score = jnp.zeros((ts, qb), f32)
        for h in range(IDX_HEADS):
            score = score + jnp.maximum(_mm_nt(ki_t, qi_heads[h]), 0.0) * wi[h]
        bits = pltpu.bitcast(score, jnp.int32)
        key = bits ^ ((bits >> 31) & 0x7FFFFFFF)
        key = jnp.where(kt * ts + key_row < limit, key, INT_MIN)
        key_hi[kt] = (key >> 16).astype(jnp.int16)
        key_lo[kt] = ((key & 0xFFFF) - HALF_BIAS).astype(jnp.int16)
        return carry

    lax.fori_loop(0, n_stiles, score_tile, 0)

    def fill_lowest(kt, carry):
        lowest = jnp.full((ts, qb), -HALF_BIAS, jnp.int16)
        key_hi[kt] = lowest
        key_lo[kt] = lowest
        return carry

    lax.fori_loop(n_stiles, n_tiles * sub, fill_lowest, 0)

    one16, zero16 = jnp.int16(1), jnp.int16(0)
    n_acc = 8

    def key_fold(h16):
        parts = [h16[j * pack:(j + 1) * pack, :] for j in range(n_acc)]
        for g in range(n_acc, ts // pack):
            parts[g % n_acc] = parts[g % n_acc] + h16[g * pack:(g + 1) * pack, :]
        while len(parts) > 1:
            parts = [parts[i] + parts[i + 1] for i in range(0, len(parts), 2)]
        return parts[0]

    def query_total(acc):
        return jnp.sum(acc.astype(jnp.int32), axis=0, keepdims=True)

    zero_acc = jnp.zeros((pack, qb), jnp.int16)

    def count(hit):
        return query_total(lax.fori_loop(0, n_stiles, lambda kt, acc: acc + key_fold(hit(kt)), zero_acc))

    def kth_largest_half(ref, want):
        def bit_step(i, u):
            cand = u | (jnp.int32(1) << (15 - i))
            cand16 = (cand - HALF_BIAS).astype(jnp.int16)
            n_ge_cand = count(lambda kt: jnp.where(ref[kt] >= cand16, one16, zero16))
            return jnp.where(n_ge_cand >= want, cand, u)
        return lax.fori_loop(0, 16, bit_step, jnp.zeros((1, qb), jnp.int32))

    hi_u = kth_largest_half(key_hi, topk)
    thr_hi = (hi_u - HALF_BIAS).astype(jnp.int16)

    def mask_low(kt, acc):
        hi = key_hi[kt]
        key_lo[kt] = jnp.where(hi == thr_hi, key_lo[kt], jnp.int16(-HALF_BIAS))
        return acc + key_fold(jnp.where(hi > thr_hi, one16, zero16))

    n_hi_gt = query_total(lax.fori_loop(0, n_stiles, mask_low, zero_acc))
    lo_u = kth_largest_half(key_lo, topk - n_hi_gt)
    lo_u = jnp.where(jnp.logical_and(hi_u == 0, lo_u == 0), 1, lo_u)
    thr_lo = (lo_u - HALF_BIAS).astype(jnp.int16)

    def tie_band(kt, low_pred):
        return jnp.where(key_hi[kt] == thr_hi, jnp.where(low_pred(key_lo[kt]), one16, zero16), zero16)


    def band_counts(kt, accs):
        in_band = key_hi[kt] == thr_hi
        lo = key_lo[kt]
        gt = jnp.where(in_band, jnp.where(lo > thr_lo, one16, zero16), zero16)
        ge = jnp.where(in_band, jnp.where(lo >= thr_lo, one16, zero16), zero16)
        return accs[0] + key_fold(gt), accs[1] + key_fold(ge)

    acc_gt, acc_ge = lax.fori_loop(0, n_stiles, band_counts, (zero_acc, zero_acc))
    n_gt = n_hi_gt + query_total(acc_gt)
    n_ge = n_hi_gt + query_total(acc_ge)
    need = (topk - n_gt).astype(f32)
    surplus = jnp.max(jnp.where(n_ge > topk, 1, 0))

    q_pairs = []
    for pi in range(DSA_HEADS // 2):
        pair = q_ref[0, :, pi * LANES:(pi + 1) * LANES]
        zero = jnp.zeros_like(pair)
        q_pairs.append(jnp.concatenate([jnp.where(lower_half, pair, zero), jnp.where(lower_half, zero, pair)], axis=0))

    def attend(exact_ties):
        m_s[...] = jnp.full(m_s.shape, M_FLOOR, f32)
        l_s[...] = jnp.zeros(l_s.shape, f32)
        acc_s[...] = jnp.zeros(acc_s.shape, f32)
        tc_w = min(ts, TIE_CHUNK)
        if exact_ties:
            tr = lax.broadcasted_iota(jnp.int32, (tc_w, tc_w), 0)
            tc = lax.broadcasted_iota(jnp.int32, (tc_w, tc_w), 1)
            prefix = jnp.where(tr >= tc, 1.0, 0.0).astype(MXU_DTYPE)
            t_hi = thr_hi.astype(jnp.int32)
            t_lo = thr_lo.astype(jnp.int32)

        def body(kt, taken):
            sel_t = []
            for st in [kt * sub + i for i in range(sub)]:
                if exact_ties:
                    hi = key_hi[st].astype(jnp.int32)
                    lo = key_lo[st].astype(jnp.int32)
                    in_band = hi == t_hi
                    above = jnp.where(hi > t_hi, 1.0, jnp.where(in_band, jnp.where(lo > t_lo, 1.0, 0.0), 0.0))
                    tie = jnp.where(in_band, jnp.where(lo == t_lo, 1.0, 0.0), 0.0)
                    parts = []
                    for c0 in range(0, ts, tc_w):
                        tie_c = tie[c0:c0 + tc_w, :]
                        rank = taken + _mm(prefix, tie_c)
                        parts.append(above[c0:c0 + tc_w, :] + jnp.where(rank <= need, tie_c, 0.0))
                        taken = rank[tc_w - 1:tc_w, :]
                    piece = parts[0] if len(parts) == 1 else jnp.concatenate(parts, axis=0)
                else:
                    piece = jnp.where(key_hi[st] > thr_hi, one16, tie_band(st, lambda lo: lo >= thr_lo)).astype(f32)
                sel_t.append(piece.T)
            sel = sel_t[0] if len(sel_t) == 1 else jnp.concatenate(sel_t, axis=1)
            bias = (1.0 - sel) * NEG_BIG
            bias2 = jnp.concatenate([bias, bias], axis=0)
            start = pl.multiple_of(kt * tk, tk)
            pairs = range(DSA_HEADS // 2)
            slabs = [slice(pi * LANES, (pi + 1) * LANES) for pi in pairs]
            logits = [_mm_nt(q_pairs[pi], k_ref[0, pl.ds(start, tk), slabs[pi]]) + bias2 for pi in pairs]
            m_old = [m_s[pi] for pi in pairs]
            m_new = [jnp.maximum(m_old[pi], jnp.max(logits[pi], axis=1, keepdims=True)) for pi in pairs]
            alpha = [jnp.exp2(m_old[pi] - m_new[pi]) for pi in pairs]
            p = [jnp.exp2(logits[pi] - m_new[pi]) for pi in pairs]
            for pi in pairs:
                l_s[pi] = alpha[pi] * l_s[pi] + jnp.sum(p[pi], axis=1, keepdims=True)
                acc_s[pi] = alpha[pi] * acc_s[pi] + _mm(p[pi], v_ref[0, pl.ds(start, tk), slabs[pi]])
                m_s[pi] = m_new[pi]
            return taken

        lax.fori_loop(0, n_tiles, body, jnp.zeros((1, qb), f32))
        for pi in range(DSA_HEADS // 2):
            out = acc_s[pi] / l_s[pi]
            o_ref[0, :, pi * LANES:(pi + 1) * LANES] = jnp.where(lower_half, out[:qb], out[qb:]).astype(o_ref.dtype)

    @pl.when(surplus == 0)
    def _():
        attend(False)

    @pl.when(surplus != 0)
    def _():
        attend(True)


def _dsa_call(qb_arr, qib_arr, sm, k_all, v_all, ki_all, pos0, n_keys, qb, tk):
    b, t, _ = qb_arr.shape
    lp = k_all.shape[1]
    ts = min(tk, DSA_SEARCH_TILE)
    assert lp % tk == 0 and t % qb == 0 and tk % ts == 0
    topk = min(TOPK_MAX, n_keys // 4)
    kern = functools.partial(_dsa_kernel, qb=qb, tk=tk, ts=ts, pos0=pos0, n_keys=n_keys, topk=topk)
    resident = dict(pipeline_mode=pl.Buffered(1))
    return pl.pallas_call(
        kern,
        out_shape=jax.ShapeDtypeStruct((b, t, DSA_W), MXU_DTYPE),
        grid=(b, t // qb),
        in_specs=[pl.BlockSpec((1, qb, DSA_W), lambda i, j: (i, j, 0)),
                  pl.BlockSpec((1, qb, IDX_W), lambda i, j: (i, j, 0)),
                  pl.BlockSpec((1, qb, LANES), lambda i, j: (i, j, 0)),
                  pl.BlockSpec((1, lp, DSA_W), lambda i, j: (i, 0, 0), **resident),
                  pl.BlockSpec((1, lp, DSA_W), lambda i, j: (i, 0, 0), **resident),
                  pl.BlockSpec((1, lp, LANES), lambda i, j: (i, 0, 0), **resident)],
        out_specs=pl.BlockSpec((1, qb, DSA_W), lambda i, j: (i, j, 0)),
        scratch_shapes=[pltpu.VMEM((lp // ts, ts, qb), jnp.int16),
                        pltpu.VMEM((lp // ts, ts, qb), jnp.int16),
                        pltpu.VMEM((DSA_HEADS // 2, 2 * qb, 1), f32),
                        pltpu.VMEM((DSA_HEADS // 2, 2 * qb, 1), f32),
                        pltpu.VMEM((DSA_HEADS // 2, 2 * qb, LANES), f32)],
        compiler_params=_params("parallel", "arbitrary"),
        name="dsa",
    )(qb_arr, qib_arr, sm, k_all, v_all, ki_all)


def _outproj_kernel(h_ref, og_ref, od_ref, w_ref, g_ref, b_ref, o_ref, *, alpha):
    mix = (jnp.dot(og_ref[...], w_ref[0:GDN_VW, :], preferred_element_type=f32)
           + jnp.dot(od_ref[...], w_ref[GDN_VW:, :], preferred_element_type=f32))
    o_ref[...] = _layer_norm(alpha * h_ref[...] + mix, g_ref[...], b_ref[...])


def _outproj_call(h, og, od, w, g, b, alpha):
    n, d = h.shape
    tm = _row_tile(n, PROJ_ROW_TILE)
    return pl.pallas_call(
        functools.partial(_outproj_kernel, alpha=alpha),
        out_shape=jax.ShapeDtypeStruct((n, d), f32),
        grid=(n // tm,),
        in_specs=[pl.BlockSpec((tm, d), lambda i: (i, 0)),
                  pl.BlockSpec((tm, GDN_VW), lambda i: (i, 0)),
                  pl.BlockSpec((tm, DSA_W), lambda i: (i, 0)),
                  pl.BlockSpec((GDN_VW + DSA_W, d), lambda i: (0, 0)),
                  pl.BlockSpec((1, d), lambda i: (0, 0)),
                  pl.BlockSpec((1, d), lambda i: (0, 0))],
        out_specs=pl.BlockSpec((tm, d), lambda i: (i, 0)),
        compiler_params=_params("parallel"),
        name="outproj_ln",
    )(h, og, od, w, g.reshape(1, d), b.reshape(1, d))


def _ffn_kernel(x_ref, wg_ref, wu_ref, wd_ref, g_ref, b_ref, o_ref, acc, *, alpha):
    j = pl.program_id(1)

    @pl.when(j == 0)
    def _():
        acc[...] = jnp.zeros(acc.shape, f32)

    xb = x_ref[...].astype(MXU_DTYPE)
    gate = jnp.dot(xb, wg_ref[...], preferred_element_type=f32)
    up = jnp.dot(xb, wu_ref[...], preferred_element_type=f32)
    act = gate * _sigmoid(gate) * up
    acc[...] += jnp.dot(act.astype(MXU_DTYPE), wd_ref[...], preferred_element_type=f32)

    @pl.when(j == pl.num_programs(1) - 1)
    def _():
        o_ref[...] = _layer_norm(alpha * x_ref[...] + acc[...], g_ref[...], b_ref[...])


def _ffn_call(x, wg, wu, wd, g, b, alpha):
    n, d = x.shape
    f = wg.shape[1]
    tm = _row_tile(n, FFN_ROW_TILE)
    tf = _row_tile(f, FFN_HIDDEN_TILE)
    return pl.pallas_call(
        functools.partial(_ffn_kernel, alpha=alpha),
        out_shape=jax.ShapeDtypeStruct((n, d), f32),
        grid=(n // tm, f // tf),
        in_specs=[pl.BlockSpec((tm, d), lambda i, j: (i, 0)),
                  pl.BlockSpec((d, tf), lambda i, j: (0, j)),
                  pl.BlockSpec((d, tf), lambda i, j: (0, j)),
                  pl.BlockSpec((tf, d), lambda i, j: (j, 0)),
                  pl.BlockSpec((1, d), lambda i, j: (0, 0)),
                  pl.BlockSpec((1, d), lambda i, j: (0, 0))],
        out_specs=pl.BlockSpec((tm, d), lambda i, j: (i, 0)),
        scratch_shapes=[pltpu.VMEM((tm, d), f32)],
        compiler_params=_params("parallel", "arbitrary"),
        name="ffn_ln",
    )(x, wg, wu, wd, g.reshape(1, d), b.reshape(1, d))


def _moe_plan_kernel(x_ref, wr_ref, plan_o, plan_t_o, xb_o, *, n_experts):
    tm = x_ref.shape[0]
    lane = lax.broadcasted_iota(jnp.int32, (tm, LANES), 1)
    x = x_ref[...]
    xb_o[...] = x.astype(xb_o.dtype)
    logits = _mm(x, wr_ref[...])
    logits = jnp.where(lane < n_experts, logits, -jnp.inf)
    m1 = jnp.max(logits, axis=1, keepdims=True)
    i1 = jnp.min(jnp.where(logits == m1, lane, LANES), axis=1, keepdims=True)
    rest = jnp.where(lane == i1, -jnp.inf, logits)
    m2 = jnp.max(rest, axis=1, keepdims=True)
    i2 = jnp.min(jnp.where(rest == m2, lane, LANES), axis=1, keepdims=True)
    e2 = jnp.exp(m2 - m1)
    w1 = 1.0 / (1.0 + e2)
    gates = jnp.where(lane == i1, w1, 0.0) + jnp.where(lane == i2, e2 * w1, 0.0)
    chosen = jnp.where(lane == i1, 1.0, 0.0) + jnp.where(lane == i2, 1.0, 0.0)
    blk = min(tm, MOE_PREFIX_BLOCK)
    r = lax.broadcasted_iota(jnp.int32, (blk, blk), 0)
    c = lax.broadcasted_iota(jnp.int32, (blk, blk), 1)
    tri = jnp.where(r >= c, 1.0, 0.0).astype(MXU_DTYPE)
    carry = jnp.zeros((1, LANES), f32)
    slots = []
    for b0 in range(0, tm, blk):
        m = chosen[b0:b0 + blk]
        inclusive = jnp.dot(tri, m.astype(MXU_DTYPE), preferred_element_type=f32) + carry
        slots.append(inclusive - m)
        carry = inclusive[blk - 1:blk, :]
    slot = jnp.concatenate(slots, axis=0) if len(slots) > 1 else slots[0]
    plan = jnp.where(lane < n_experts, gates,
                     jnp.where(lane < 2 * n_experts, pltpu.roll(slot, n_experts, 1),
                               jnp.where(lane < 3 * n_experts, pltpu.roll(chosen, 2 * n_experts, 1), 0.0)))
    plan_o[...] = plan
    plan_t_o[...] = plan.T[0:MOE_PLAN_ROWS, :]


def _moe_expert_kernel(xb_ref, plan_ref, plan_t_ref, wg_ref, wu_ref, wd_ref, y_ref,
                       xc, acc_c, *, n_experts, slot_block):
    e = pl.program_id(1)
    j = pl.program_id(2)
    tm = xb_ref.shape[0]
    cb = slot_block
    slot_row = plan_t_ref[pl.ds(e + n_experts, 1), :]
    chosen_row = plan_t_ref[pl.ds(e + 2 * n_experts, 1), :]
    count = jnp.max(jnp.where(chosen_row > 0.0, slot_row + 1.0, 0.0)).astype(jnp.int32)
    n_blocks = (count + cb - 1) // cb
    dot = functools.partial(jnp.dot, preferred_element_type=f32)

    def one_hot(sb):
        slot_id = (lax.broadcasted_iota(jnp.int32, (cb, 1), 0) + sb * cb).astype(f32)
        hit = jnp.where(chosen_row > 0.0, jnp.where(slot_row == slot_id, 1.0, 0.0), 0.0)
        return hit.astype(MXU_DTYPE)

    def rows_of(sb):
        return pl.ds(pl.multiple_of(sb * cb, cb), cb)

    @pl.when(jnp.logical_and(e == 0, j == 0))
    def _():
        y_ref[...] = jnp.zeros(y_ref.shape, f32)

    @pl.when(j == 0)
    def _():
        def gather(sb, carry):
            xc[rows_of(sb), :] = dot(one_hot(sb), xb_ref[...]).astype(xc.dtype)
            acc_c[rows_of(sb), :] = jnp.zeros((cb, acc_c.shape[1]), f32)
            return carry
        lax.fori_loop(0, n_blocks, gather, 0)

    def expert(sb, carry):
        xs = xc[rows_of(sb), :]
        gate = dot(xs, wg_ref[0])
        up = dot(xs, wu_ref[0])
        act = gate * _sigmoid(gate) * up
        acc_c[rows_of(sb), :] += dot(act.astype(MXU_DTYPE), wd_ref[0])
        return carry

    lax.fori_loop(0, n_blocks, expert, 0)

    @pl.when(j == pl.num_programs(2) - 1)
    def _():
        lane = lax.broadcasted_iota(jnp.int32, (tm, LANES), 1)
        gate_col = jnp.sum(jnp.where(lane == e, plan_ref[...], 0.0), axis=1, keepdims=True)

        def scatter(sb, carry):
            back = lax.dot_general(one_hot(sb), acc_c[rows_of(sb), :].astype(MXU_DTYPE),
                                   (((0,), (0,)), ((), ())), preferred_element_type=f32)
            y_ref[...] += gate_col * back
            return carry
        lax.fori_loop(0, n_blocks, scatter, 0)


def _residual_ln_kernel(x_ref, y_ref, g_ref, b_ref, o_ref, *, alpha):
    o_ref[...] = _layer_norm(alpha * x_ref[...] + y_ref[...], g_ref[...], b_ref[...])


def _moe_call(x, wr, wg, wu, wd, g, b, alpha):
    n, d = x.shape
    n_experts, _, f = wg.shape
    assert 3 * n_experts <= MOE_PLAN_ROWS and TOP_K_EXPERTS == 2
    tm = _row_tile(n, MOE_TOKEN_TILE)
    tf = _row_tile(f, FFN_HIDDEN_TILE)
    cb = min(MOE_SLOT_BLOCK, tm)
    cap = -(-tm // cb) * cb
    plan, plan_t, xb = pl.pallas_call(
        functools.partial(_moe_plan_kernel, n_experts=n_experts),
        out_shape=[jax.ShapeDtypeStruct((n, LANES), f32),
                   jax.ShapeDtypeStruct((MOE_PLAN_ROWS, n), f32),
                   jax.ShapeDtypeStruct((n, d), MXU_DTYPE)],
        grid=(n // tm,),
        in_specs=[pl.BlockSpec((tm, d), lambda i: (i, 0)),
                  pl.BlockSpec((d, LANES), lambda i: (0, 0))],
        out_specs=[pl.BlockSpec((tm, LANES), lambda i: (i, 0)),
                   pl.BlockSpec((MOE_PLAN_ROWS, tm), lambda i: (0, i)),
                   pl.BlockSpec((tm, d), lambda i: (i, 0))],
        compiler_params=_params("parallel"),
        name="moe_plan",
    )(x, wr)
    y = pl.pallas_call(
        functools.partial(_moe_expert_kernel, n_experts=n_experts, slot_block=cb),
        out_shape=jax.ShapeDtypeStruct((n, d), f32),
        grid=(n // tm, n_experts, f // tf),
        in_specs=[pl.BlockSpec((tm, d), lambda i, e, j: (i, 0)),
                  pl.BlockSpec((tm, LANES), lambda i, e, j: (i, 0)),
                  pl.BlockSpec((MOE_PLAN_ROWS, tm), lambda i, e, j: (0, i)),
                  pl.BlockSpec((1, d, tf), lambda i, e, j: (e, 0, j)),
                  pl.BlockSpec((1, d, tf), lambda i, e, j: (e, 0, j)),
                  pl.BlockSpec((1, tf, d), lambda i, e, j: (e, j, 0))],
        out_specs=pl.BlockSpec((tm, d), lambda i, e, j: (i, 0)),
        scratch_shapes=[pltpu.VMEM((cap, d), MXU_DTYPE),
                        pltpu.VMEM((cap, d), f32)],
        compiler_params=_params("parallel", "arbitrary", "arbitrary"),
        name="moe_experts",
    )(xb, plan, plan_t, wg, wu, wd)
    tr = _row_tile(n, LN_ROW_TILE)
    return pl.pallas_call(
        functools.partial(_residual_ln_kernel, alpha=alpha),
        out_shape=jax.ShapeDtypeStruct((n, d), f32),
        grid=(n // tr,),
        in_specs=[pl.BlockSpec((tr, d), lambda i: (i, 0)),
                  pl.BlockSpec((tr, d), lambda i: (i, 0)),
                  pl.BlockSpec((1, d), lambda i: (0, 0)),
                  pl.BlockSpec((1, d), lambda i: (0, 0))],
        out_specs=pl.BlockSpec((tr, d), lambda i: (i, 0)),
        compiler_params=_params("parallel"),
        name="moe_residual_ln",
    )(x, y, g.reshape(1, d), b.reshape(1, d))


def _pack_w_in(w_in):
    o_z = GDN_CONV_CH
    o_b = o_z + GDN_VW
    o_a = o_b + GDN_HEADS
    o_qd = o_a + GDN_HEADS
    o_kd = o_qd + DSA_W
    o_vd = o_kd + DSA_W
    o_qi = o_vd + DSA_W
    o_ki = o_qi + IDX_W
    o_wi = o_ki + IDX_DIM
    assert w_in.shape[-1] == o_wi + IDX_HEADS
    sl = lambda a, n: w_in[..., a:a + n]
    pad = jnp.zeros(w_in.shape[:-1] + (LANES - 2 * GDN_HEADS - IDX_HEADS,), w_in.dtype)
    packed = jnp.concatenate([
        sl(0, GDN_CONV_CH), sl(o_z, GDN_VW), sl(o_qd, DSA_W), sl(o_kd, DSA_W), sl(o_vd, DSA_W),
        sl(o_qi, IDX_W), sl(o_ki, IDX_DIM), sl(o_ki, IDX_DIM),
        sl(o_b, GDN_HEADS), sl(o_a, GDN_HEADS), sl(o_wi, IDX_HEADS), pad], axis=-1)
    assert packed.shape[-1] == C_END
    return packed.astype(MXU_DTYPE)


def _rope_tables(pos):
    half = DSA_DH // 8
    inv_freq = ROPE_THETA ** (-jnp.arange(half, dtype=f32) / half)
    ang = pos.astype(f32)[:, None] * inv_freq[None, :]
    cos, sin = jnp.cos(ang), jnp.sin(ang)
    t = pos.shape[0]
    ones = jnp.ones((t, DSA_DH - 2 * half), f32)
    zeros = jnp.zeros((t, DSA_DH - 2 * half), f32)
    z8 = jnp.zeros((t, half), f32)
    cos_t = jnp.concatenate([cos, cos, ones], -1)
    lo = jnp.concatenate([-sin, z8, zeros], -1)
    hi = jnp.concatenate([z8, sin, zeros], -1)
    return tuple(jnp.tile(a, (1, LANES // DSA_DH)) for a in (cos_t, lo, hi))


def _gdn_params(a_log, dt_bias):
    row = lambda v: jnp.zeros((LANES,), f32).at[SM_DECAY:SM_DECAY + GDN_HEADS].set(v.astype(f32))
    return jnp.stack([row(a_log), row(dt_bias)])


def _trunk(x, pos0, conv_bufs, s0s, past_k, past_v, past_ik, chunk, wts, depth):
    b, t, d = x.shape
    n = b * t
    alpha = (2.0 * depth) ** 0.25
    tabs = _rope_tables(pos0 + jnp.arange(t))
    past = 0 if past_k is None else past_k.shape[2]
    n_keys = past + t
    assert t >= CONV_W - 1
    qb = min(t, DSA_Q_BLOCK)
    tk = DSA_KEY_TILE
    lp = -(-n_keys // tk) * tk

    def with_past(new, old, l):
        parts = [] if old is None else [old[l].reshape(b, past, -1).astype(MXU_DTYPE)]
        parts.append(new.reshape(b, t, -1))
        if lp > n_keys:
            parts.append(jnp.zeros((b, lp - n_keys, new.shape[-1]), MXU_DTYPE))
        return parts[0] if len(parts) == 1 else jnp.concatenate(parts, axis=1)

    h = _ln_call(x.reshape(n, d), wts["ln_in_g"], wts["ln_in_b"])
    new_conv, new_s = [], []
    stacks = ()
    for l in range(depth):
        (qkv, z, k_st, v_st, ki_st, q_b, k_b, v_b, qi_b, ki_b, sm) = _inproj_call(
            h, wts["w_in"][l], tabs, t, l, depth, stacks)
        stacks = (k_st, v_st, ki_st)
        qkv3 = qkv.reshape(b, t, GDN_CONV_CH)
        buf8 = jnp.concatenate([jnp.zeros((b, SUBLANES - (CONV_W - 1), GDN_CONV_CH), f32), conv_bufs[l]], axis=1)
        o_g, s_new = _gdn_call(qkv3, z.reshape(b, t, GDN_VW), sm.reshape(b, t, LANES), wts["conv_w"][l],
                               buf8, s0s[l], _gdn_params(wts["gdn_a_log"][l], wts["gdn_dt_bias"][l]),
                               wts["gdn_norm_w"][l].reshape(1, GDN_DV), chunk)
        ki_dup = None if past_ik is None else jnp.concatenate([past_ik, past_ik], axis=-1)
        o_d = _dsa_call(q_b.reshape(b, t, DSA_W), qi_b.reshape(b, t, IDX_W), sm.reshape(b, t, LANES),
                        with_past(k_b, past_k, l), with_past(v_b, past_v, l), with_past(ki_b, ki_dup, l),
                        past, n_keys, qb, tk)
        h = _outproj_call(h, o_g.reshape(n, GDN_VW), o_d.reshape(n, DSA_W), wts["w_out"][l],
                          wts["ln1_g"][l], wts["ln1_b"][l], alpha)
        if l % 2 == 0:
            i = l // 2
            h = _ffn_call(h, wts["ffn_wg"][i], wts["ffn_wu"][i], wts["ffn_wd"][i],
                          wts["ln2_g"][l], wts["ln2_b"][l], alpha)
        else:
            i = l // 2
            h = _moe_call(h, wts["moe_router"][i], wts["moe_wg"][i], wts["moe_wu"][i], wts["moe_wd"][i],
                          wts["ln2_g"][l], wts["ln2_b"][l], alpha)
        new_conv.append(qkv3[:, t - (CONV_W - 1):])
        new_s.append(s_new)
    k_st, v_st, ki_st = stacks
    return (h.reshape(b, t, d), k_st.reshape(depth, b, t, DSA_HEADS, DSA_DH),
            v_st.reshape(depth, b, t, DSA_HEADS, DSA_DH), ki_st.reshape(depth, b, t, IDX_DIM),
            jnp.stack(new_s), jnp.stack(new_conv))


def kernel(x_prompt, x_sample, cache_k, cache_v, cache_idx_k, state_gdn, state_conv, ln_in_g, ln_in_b, w_in, conv_w, gdn_a_log, gdn_dt_bias, gdn_norm_w, w_out, ln1_g, ln1_b, ln2_g, ln2_b, ffn_wg, ffn_wu, ffn_wd, moe_router, moe_wg, moe_wu, moe_wd):
    depth = w_in.shape[0]
    n_experts = moe_router.shape[-1]
    bp, tp, d = x_prompt.shape
    cast = lambda a: a.astype(MXU_DTYPE)
    router = jnp.concatenate(
        [moe_router.astype(f32), jnp.zeros(moe_router.shape[:-1] + (LANES - n_experts,), f32)], axis=-1)
    wts = dict(ln_in_g=ln_in_g, ln_in_b=ln_in_b, w_in=_pack_w_in(w_in), conv_w=conv_w,
               gdn_a_log=gdn_a_log, gdn_dt_bias=gdn_dt_bias, gdn_norm_w=gdn_norm_w, w_out=cast(w_out),
               ln1_g=ln1_g, ln1_b=ln1_b, ln2_g=ln2_g, ln2_b=ln2_b,
               ffn_wg=cast(ffn_wg), ffn_wu=cast(ffn_wu), ffn_wd=cast(ffn_wd),
               moe_router=router, moe_wg=cast(moe_wg), moe_wu=cast(moe_wu), moe_wd=cast(moe_wd))
    zero_conv = jnp.zeros((depth, bp, CONV_W - 1, GDN_CONV_CH), f32)
    zero_s = jnp.zeros((depth, bp, GDN_HEADS, GDN_DK, GDN_DV), f32)
    y_p, k_p, v_p, ik_p, s_p, conv_p = _trunk(
        x_prompt, 0, zero_conv, zero_s, None, None, None, CHUNK, wts, depth)
    ts = x_sample.shape[1]
    past = cache_k.shape[2]
    y_s, k_s, v_s, ik_s, s_s, conv_s = _trunk(
        x_sample, past, state_conv, state_gdn, cache_k, cache_v, cache_idx_k, ts, wts, depth)
    return (y_p, y_s, k_p, v_p, ik_p, s_p, conv_p, k_s, v_s, ik_s, s_s, conv_s)
```
